```python
import math
import jax, jax.numpy as jnp
from jax import lax
import numpy as np

D_MODEL = 2048
BATCH = 4
SEQ = 4096
DEPTH = 4

GRID_W = 64
CTX_LEN = 256
N_MIXERS = 3
N_MOD = 9
D_FF = 5504
FFN_RES = 0.5
NORM_EPS = 1e-6
ROPE_THETA = 10000.0
Q_BLOCK = 128
MLA_HEADS = 16
MLA_NOPE = 128
MLA_ROPE = 64
MLA_V = 128
MLA_Q_RANK = 512
MLA_KV_RANK = 512
HY_ORDER = 2
HY_EMB = 33
HY_BANDS = (HY_EMB - 1) // 2
HY_FILTER_W = 64
HY_SHORT = 3
HY_DECAY_TARGET = 1e-2
HY_DECAY_MIN = -math.log(HY_DECAY_TARGET) / 1.5
HY_DECAY_MAX = -math.log(HY_DECAY_TARGET) / 0.3
WIN_HEAD_DIM = 64
WIN_Q_HEADS = D_MODEL // WIN_HEAD_DIM
WIN_KV_HEADS = WIN_Q_HEADS // 8
WIN_GROUP = WIN_Q_HEADS // WIN_KV_HEADS
WINDOW = 128

kernel_name = "hybrid_mla_hyena_swa_diffusion_trunk"


def rmsnorm(h, gain):
    hf = h.astype(jnp.float32)
    y = hf * lax.rsqrt(jnp.mean(hf * hf, axis=-1, keepdims=True) + NORM_EPS)
    return (y * gain.astype(jnp.float32)).astype(h.dtype)


def modulate(h, gain, shift, scale):
    return rmsnorm(h, gain) * (1 + scale) + shift


def swiglu(h, w_gate, w_up, w_down):
    return (jax.nn.silu(h @ w_gate) * (h @ w_up)) @ w_down


def mod_slot(m, slot):
    return m[:, :, 3 * slot], m[:, :, 3 * slot + 1], m[:, :, 3 * slot + 2]


def macaron_ffn(h, m, slot, gain, w_gate, w_up, w_down):
    shift, scale, gate = mod_slot(m, slot)
    return h + FFN_RES * gate * swiglu(modulate(h, gain, shift, scale), w_gate, w_up, w_down)


def _rope_1d(x, pos):
    n = x.shape[-1] // 2
    inv = ROPE_THETA ** (-jnp.arange(n, dtype=jnp.float32) / n)
    ang = pos.astype(jnp.float32)[:, None] * inv[None, :]
    ang = ang.reshape(ang.shape[:1] + (1,) * (x.ndim - 3) + (n,))
    cos, sin = jnp.cos(ang), jnp.sin(ang)
    xf = x.astype(jnp.float32)
    x1, x2 = xf[..., :n], xf[..., n:]
    return jnp.concatenate([x1 * cos - x2 * sin, x2 * cos + x1 * sin], axis=-1).astype(x.dtype)


def axial_rope(x, rows, cols):
    half = x.shape[-1] // 2
    return jnp.concatenate([_rope_1d(x[..., :half], rows), _rope_1d(x[..., half:], cols)], axis=-1)


def softmax_attention(q, k, v, scale):
    s = jnp.einsum('bqhd,bkhd->bhqk', q, k, preferred_element_type=jnp.float32) * scale
    p = jax.nn.softmax(s, axis=-1).astype(v.dtype)
    return jnp.einsum('bhqk,bkhd->bqhd', p, v)


def blocked_attention(q, k, v, scale):
    b, t, h, d = q.shape
    qb = q.reshape(b, t // Q_BLOCK, Q_BLOCK, h, d).swapaxes(0, 1)
    ob = lax.map(lambda qi: softmax_attention(qi, k, v, scale), qb)
    return ob.swapaxes(0, 1).reshape(b, t, h, v.shape[-1])


def mla_q(h, w_dq, g_q, w_uq):
    b, t, _ = h.shape
    return (rmsnorm(h @ w_dq, g_q) @ w_uq).reshape(b, t, MLA_HEADS, MLA_NOPE + MLA_ROPE)


def mla_kv(h, w_dkv, g_kv, w_ukv):
    b, t, _ = h.shape
    a = h @ w_dkv
    c_kv = rmsnorm(a[..., :MLA_KV_RANK], g_kv)
    k_rope = a[..., MLA_KV_RANK:]
    kv = (c_kv @ w_ukv).reshape(b, t, MLA_HEADS, MLA_NOPE + MLA_V)
    return kv[..., :MLA_NOPE], k_rope, kv[..., MLA_NOPE:]


def _mla_keys(k_nope, k_rope):
    kr = jnp.broadcast_to(k_rope[:, :, None, :], k_nope.shape[:3] + (MLA_ROPE,))
    return jnp.concatenate([k_nope, kr], axis=-1)


def mla_mixer(hx, hc, rows, cols, w_dq, g_q, w_uq, w_dkv, g_kv, w_ukv, w_o, need_ctx):
    scale = (MLA_NOPE + MLA_ROPE) ** -0.5
    b, s, _ = hx.shape
    n_c = hc.shape[1]
    q_l = mla_q(hx, w_dq, g_q, w_uq)
    q_l = jnp.concatenate([q_l[..., :MLA_NOPE], axial_rope(q_l[..., MLA_NOPE:], rows, cols)], axis=-1)
    kn_l, kr_l, v_l = mla_kv(hx, w_dkv, g_kv, w_ukv)
    k_l = _mla_keys(kn_l, axial_rope(kr_l, rows, cols))
    kn_c, kr_c, v_c = mla_kv(hc, w_dkv, g_kv, w_ukv)
    k_c = _mla_keys(kn_c, kr_c)
    k_all = jnp.concatenate([k_c, k_l], axis=1)
    v_all = jnp.concatenate([v_c, v_l], axis=1)
    o_l = blocked_attention(q_l, k_all, v_all, scale).reshape(b, s, MLA_HEADS * MLA_V) @ w_o
    o_c = None
    if need_ctx:
        q_c = mla_q(hc, w_dq, g_q, w_uq)
        o_c = softmax_attention(q_c, k_c, v_c, scale).reshape(b, n_c, MLA_HEADS * MLA_V) @ w_o
    return o_l, o_c


def hyena_pos_features(L):
    t = jnp.linspace(0.0, 1.0, L, dtype=jnp.float32)[:, None]
    w = 2.0 * math.pi * jnp.arange(L, dtype=jnp.float32)[:, None] / L
    f = jnp.linspace(1e-4, HY_BANDS - 1, HY_BANDS, dtype=jnp.float32)[None, :]
    return jnp.concatenate([t, jnp.cos(f * w), -jnp.sin(f * w)], axis=-1), t


def hyena_filters(L, f_w1, f_b1, f_w2, f_b2, f_w3, f_b3, f_freq, f_w4, decay):
    z, t = hyena_pos_features(L)
    f32 = jnp.float32
    freq = f_freq.astype(f32)
    h = jnp.sin(freq * (z @ f_w1.astype(f32) + f_b1.astype(f32)))
    h = jnp.sin(freq * (h @ f_w2.astype(f32) + f_b2.astype(f32)))
    h = jnp.sin(freq * (h @ f_w3.astype(f32) + f_b3.astype(f32)))
    h = (h @ f_w4.astype(f32)) * jnp.exp(-t * jnp.abs(decay.astype(f32)))
    h = h.reshape(L, HY_ORDER, 2, D_MODEL)
    fwd, bwd = h[:, :, 0], h[:, :, 1]
    return jnp.concatenate([fwd, jnp.zeros_like(fwd[:1]), bwd[:0:-1]], axis=0)


def long_conv(z, filt, skip):
    L = z.shape[1]
    zf = jnp.fft.rfft(z.astype(jnp.float32), n=2 * L, axis=1)
    ff = jnp.fft.rfft(filt, axis=0)
    y = jnp.fft.irfft(zf * ff[None], n=2 * L, axis=1)[:, :L]
    return (y + z.astype(jnp.float32) * skip.astype(jnp.float32)).astype(z.dtype)


def hyena_seq(h, w_in, conv_w, conv_b, f_w1, f_b1, f_w2, f_b2, f_w3, f_b3, f_freq, f_w4, decay, skip, w_out):
    L = h.shape[1]
    u = h @ w_in
    pad = HY_SHORT // 2
    up = jnp.pad(u, ((0, 0), (pad, pad), (0, 0)))
    u = sum(up[:, j:j + L] * conv_w[j] for j in range(HY_SHORT)) + conv_b
    v, *gates = jnp.split(u, HY_ORDER + 1, axis=-1)
    filt = hyena_filters(L, f_w1, f_b1, f_w2, f_b2, f_w3, f_b3, f_freq, f_w4, decay)
    z = v
    for n in range(HY_ORDER):
        z = gates[n] * long_conv(z, filt[:, n], skip[n])
    return z @ w_out


def sink_softmax(s, sink):
    sk = sink.astype(jnp.float32)[None, :, :, None, None]
    m = jnp.maximum(jnp.max(s, axis=-1, keepdims=True), sk)
    e = jnp.exp(s - m)
    return e / (jnp.sum(e, axis=-1, keepdims=True) + jnp.exp(sk - m))


def window_mixer(hx, hc, rows, cols, w_qkv, sink, w_o, need_ctx):
    b, s, _ = hx.shape
    n_c = hc.shape[1]
    qw = WIN_Q_HEADS * WIN_HEAD_DIM
    kw = WIN_KV_HEADS * WIN_HEAD_DIM
    scale = WIN_HEAD_DIM ** -0.5
    sink_h = sink.reshape(WIN_KV_HEADS, WIN_GROUP)

    def split_q(a):
        return a.reshape(a.shape[0], a.shape[1], WIN_KV_HEADS, WIN_GROUP, WIN_HEAD_DIM)

    def split_h(a):
        return a.reshape(a.shape[0], a.shape[1], WIN_KV_HEADS, WIN_HEAD_DIM)

    qkv_l = hx @ w_qkv
    q_l = axial_rope(split_q(qkv_l[..., :qw]), rows, cols)
    k_l = axial_rope(split_h(qkv_l[..., qw:qw + kw]), rows, cols)
    v_l = split_h(qkv_l[..., qw + kw:])
    kv_c = hc @ w_qkv[:, qw:]
    k_c, v_c = split_h(kv_c[..., :kw]), split_h(kv_c[..., kw:])

    span = Q_BLOCK + 2 * WINDOW
    padw = ((0, 0), (WINDOW, WINDOW), (0, 0), (0, 0))
    kp, vp = jnp.pad(k_l, padw), jnp.pad(v_l, padw)
    n_blk = s // Q_BLOCK
    qb = q_l.reshape(b, n_blk, Q_BLOCK, WIN_KV_HEADS, WIN_GROUP, WIN_HEAD_DIM).swapaxes(0, 1)

    def block(args):
        i, q_i = args
        start = i * Q_BLOCK
        k_i = lax.dynamic_slice_in_dim(kp, start, span, axis=1)
        v_i = lax.dynamic_slice_in_dim(vp, start, span, axis=1)
        q_pos = start + jnp.arange(Q_BLOCK)
        k_pos = start - WINDOW + jnp.arange(span)
        ok = (jnp.abs(q_pos[:, None] - k_pos[None, :]) <= WINDOW) & (k_pos >= 0)[None, :] & (k_pos < s)[None, :]
        s_lat = jnp.einsum('bqkgd,bskd->bkgqs', q_i, k_i, preferred_element_type=jnp.float32) * scale
        s_lat = jnp.where(ok, s_lat, -jnp.inf)
        s_ctx = jnp.einsum('bqkgd,bckd->bkgqc', q_i, k_c, preferred_element_type=jnp.float32) * scale
        p = sink_softmax(jnp.concatenate([s_ctx, s_lat], axis=-1), sink_h).astype(v_i.dtype)
        return (jnp.einsum('bkgqc,bckd->bqkgd', p[..., :n_c], v_c)
                + jnp.einsum('bkgqs,bskd->bqkgd', p[..., n_c:], v_i))

    ob = lax.map(block, (jnp.arange(n_blk), qb))
    o_l = ob.swapaxes(0, 1).reshape(b, s, qw) @ w_o
    o_c = None
    if need_ctx:
        q_c = split_q(hc @ w_qkv[:, :qw])
        s_c = jnp.einsum('bqkgd,bckd->bkgqc', q_c, k_c, preferred_element_type=jnp.float32) * scale
        p_c = sink_softmax(s_c, sink_h).astype(v_c.dtype)
        o_c = jnp.einsum('bkgqc,bckd->bqkgd', p_c, v_c).reshape(b, n_c, qw) @ w_o
    return o_l, o_c


def setup_inputs(seed: int = 0) -> dict:
    key = jax.random.key(seed)
    ks = iter(jax.random.split(key, 64))
    f32 = jnp.float32

    def nrm(shape, fan_in, gain=1.0):
        return jax.random.normal(next(ks), shape, f32) * (gain * fan_in ** -0.5)

    def near_one(shape):
        return 1.0 + 0.05 * jax.random.normal(next(ks), shape, f32)

    def small(shape):
        return 0.02 * jax.random.normal(next(ks), shape, f32)

    n_a = len(range(0, DEPTH, N_MIXERS))
    n_b = len(range(1, DEPTH, N_MIXERS))
    n_c = len(range(2, DEPTH, N_MIXERS))
    D = D_MODEL
    qw = WIN_Q_HEADS * WIN_HEAD_DIM
    kw = WIN_KV_HEADS * WIN_HEAD_DIM
    return {
        "x": jax.random.normal(next(ks), (BATCH, SEQ, D), f32),
        "c": jax.random.normal(next(ks), (BATCH, D), f32),
        "ctx": jax.random.normal(next(ks), (BATCH, CTX_LEN, D), f32),
        "c_ctx": jax.random.normal(next(ks), (D,), f32),
        "mod_w": nrm((DEPTH, D, N_MOD * D), D, 0.5),
        "mod_b": small((DEPTH, N_MOD * D)),
        "norm_g": near_one((DEPTH, 3, D)),
        "final_g": near_one((D,)),
        "ffn_wg": nrm((DEPTH, 2, D, D_FF), D),
        "ffn_wu": nrm((DEPTH, 2, D, D_FF), D),
        "ffn_wd": nrm((DEPTH, 2, D_FF, D), D_FF),
        "mla_w_dq": nrm((n_a, D, MLA_Q_RANK), D),
        "mla_g_q": near_one((n_a, MLA_Q_RANK)),
        "mla_w_uq": nrm((n_a, MLA_Q_RANK, MLA_HEADS * (MLA_NOPE + MLA_ROPE)), MLA_Q_RANK),
        "mla_w_dkv": nrm((n_a, D, MLA_KV_RANK + MLA_ROPE), D),
        "mla_g_kv": near_one((n_a, MLA_KV_RANK)),
        "mla_w_ukv": nrm((n_a, MLA_KV_RANK, MLA_HEADS * (MLA_NOPE + MLA_V)), MLA_KV_RANK),
        "mla_w_o": nrm((n_a, MLA_HEADS * MLA_V, D), MLA_HEADS * MLA_V),
        "hy_w_in": nrm((n_b, D, (HY_ORDER + 1) * D), D),
        "hy_conv_w": nrm((n_b, HY_SHORT, (HY_ORDER + 1) * D), HY_SHORT),
        "hy_conv_b": small((n_b, (HY_ORDER + 1) * D)),
        "hy_f_w1": nrm((n_b, HY_EMB, HY_FILTER_W), HY_EMB),
        "hy_f_b1": small((n_b, HY_FILTER_W)),
        "hy_f_w2": nrm((n_b, HY_FILTER_W, HY_FILTER_W), HY_FILTER_W),
        "hy_f_b2": small((n_b, HY_FILTER_W)),
        "hy_f_w3": nrm((n_b, HY_FILTER_W, HY_FILTER_W), HY_FILTER_W),
        "hy_f_b3": small((n_b, HY_FILTER_W)),
        "hy_f_freq": near_one((n_b, HY_FILTER_W)),
        "hy_f_w4": nrm((n_b, HY_FILTER_W, HY_ORDER * 2 * D), HY_FILTER_W, 0.02),
        "hy_decay": jax.random.uniform(next(ks), (n_b, HY_ORDER * 2 * D), f32, HY_DECAY_MIN, HY_DECAY_MAX),
        "hy_skip": jax.random.normal(next(ks), (n_b, HY_ORDER, D), f32),
        "hy_w_out": nrm((n_b, D, D), D),
        "win_w_qkv": nrm((n_c, D, qw + 2 * kw), D),
        "win_sink": jax.random.normal(next(ks), (n_c, WIN_Q_HEADS), f32),
        "win_w_o": nrm((n_c, qw, D), qw),
    }


def reference(x, c, ctx, c_ctx, mod_w, mod_b, norm_g, final_g, ffn_wg, ffn_wu, ffn_wd,
              mla_w_dq, mla_g_q, mla_w_uq, mla_w_dkv, mla_g_kv, mla_w_ukv, mla_w_o,
              hy_w_in, hy_conv_w, hy_conv_b, hy_f_w1, hy_f_b1, hy_f_w2, hy_f_b2, hy_f_w3, hy_f_b3,
              hy_f_freq, hy_f_w4, hy_decay, hy_skip, hy_w_out,
              win_w_qkv, win_sink, win_w_o):
    b, s, d = x.shape
    n_rows = s // GRID_W
    rows = jnp.repeat(jnp.arange(n_rows, dtype=jnp.int32), GRID_W)
    cols = jnp.arange(n_rows * GRID_W, dtype=jnp.int32) % GRID_W
    sc = jax.nn.silu(c)
    scc = jax.nn.silu(c_ctx)
    for i in range(DEPTH):
        kind, j = i % N_MIXERS, i // N_MIXERS
        need_ctx = i < DEPTH - 1
        ctx_live = need_ctx or kind != 1
        ml = (sc @ mod_w[i] + mod_b[i]).reshape(b, 1, N_MOD, d)
        x = macaron_ffn(x, ml, 0, norm_g[i, 0], ffn_wg[i, 0], ffn_wu[i, 0], ffn_wd[i, 0])
        sh, scl, gt = mod_slot(ml, 1)
        hx = modulate(x, norm_g[i, 1], sh, scl)
        hc = None
        if ctx_live:
            mc = (scc @ mod_w[i] + mod_b[i]).reshape(1, 1, N_MOD, d)
            ctx = macaron_ffn(ctx, mc, 0, norm_g[i, 0], ffn_wg[i, 0], ffn_wu[i, 0], ffn_wd[i, 0])
            shc, sclc, gtc = mod_slot(mc, 1)
            hc = modulate(ctx, norm_g[i, 1], shc, sclc)
        if kind == 0:
            o_l, o_c = mla_mixer(hx, hc, rows, cols, mla_w_dq[j], mla_g_q[j], mla_w_uq[j],
                                 mla_w_dkv[j], mla_g_kv[j], mla_w_ukv[j], mla_w_o[j], need_ctx)
        elif kind == 1:
            hp = (hy_w_in[j], hy_conv_w[j], hy_conv_b[j], hy_f_w1[j], hy_f_b1[j], hy_f_w2[j], hy_f_b2[j],
                  hy_f_w3[j], hy_f_b3[j], hy_f_freq[j], hy_f_w4[j], hy_decay[j], hy_skip[j], hy_w_out[j])
            o_l = hyena_seq(hx, *hp)
            o_c = hyena_seq(hc, *hp) if need_ctx else None
        else:
            o_l, o_c = window_mixer(hx, hc, rows, cols, win_w_qkv[j], win_sink[j], win_w_o[j], need_ctx)
        x = x + gt * o_l
        x = macaron_ffn(x, ml, 2, norm_g[i, 2], ffn_wg[i, 1], ffn_wu[i, 1], ffn_wd[i, 1])
        if need_ctx:
            ctx = ctx + gtc * o_c
            ctx = macaron_ffn(ctx, mc, 2, norm_g[i, 2], ffn_wg[i, 1], ffn_wu[i, 1], ffn_wd[i, 1])
    return rmsnorm(x, final_g)
```

```python
import functools
import math

import numpy as np
import jax
import jax.numpy as jnp
from jax import lax
from jax.experimental import pallas as pl
from jax.experimental.pallas import tpu as pltpu

F32 = jnp.float32
BF16 = jnp.bfloat16

GRID_W = 64
N_MOD = 9
FFN_RES = 0.5
NORM_EPS = 1e-6
ROPE_THETA = 10000.0
N_MIXERS = 3
MLA_HEADS = 16
MLA_NOPE = 128
MLA_ROPE = 64
MLA_V = 128
MLA_RANK = 512
MLA_HEAD_PAD = 256
HY_ORDER = 2
HY_EMB = 33
HY_BANDS = (HY_EMB - 1) // 2
HY_FILTER_W = 64
WIN_HEAD_DIM = 64
WIN_Q_HEADS = 32
WIN_KV_HEADS = 4
WIN_GROUP = WIN_Q_HEADS // WIN_KV_HEADS
WINDOW = 128
Q_BLOCK = 128

LANES = 128
V7X_VMEM_LIMIT = 56 * 1024 * 1024
FFN_TILE = 512
NEG_BIG = -1e30


def _cp(*sem):
    return pltpu.CompilerParams(dimension_semantics=sem, vmem_limit_bytes=V7X_VMEM_LIMIT)


def _single(shape, imap):
    return pl.BlockSpec(shape, imap, pipeline_mode=pl.Buffered(1))


def _dot(a, b):
    return jnp.dot(a, b, preferred_element_type=F32)


def _dot_nt(a, b):
    return lax.dot_general(a, b, (((1,), (1,)), ((), ())), preferred_element_type=F32)


def _split(x):
    hi = x.astype(BF16)
    lo = (x - hi.astype(F32)).astype(BF16)
    return hi, lo


def _dot3(a_hi, a_lo, b_hi, b_lo):
    return _dot(a_hi, b_hi) + (_dot(a_hi, b_lo) + _dot(a_lo, b_hi))


def _modnorm(x, gain, shift, scale):
    r = lax.rsqrt(jnp.mean(x * x, axis=-1, keepdims=True) + NORM_EPS)
    return (x * r * gain) * (1.0 + scale) + shift


def _rms(x, gain):
    return x * lax.rsqrt(jnp.mean(x * x, axis=-1, keepdims=True) + NORM_EPS) * gain


def _rope128(x, cos, sin):
    lane = lax.broadcasted_iota(jnp.int32, x.shape, 1)
    partner = jnp.where(lane % 32 < 16, pltpu.roll(x, LANES - 16, 1), pltpu.roll(x, 16, 1))
    return x * cos + partner * sin


def _mod_kernel(a_ref, w_ref, b_ref, o_ref):
    a = a_ref[...]
    a = a * jax.nn.sigmoid(a)
    a_hi, a_lo = _split(a)
    w_hi, w_lo = _split(w_ref[...])
    o_ref[...] = _dot3(a_hi, a_lo, w_hi, w_lo) + b_ref[...]


def _modulation(cc, mod_w, mod_b):
    depth, d, n = mod_w.shape
    rows = cc.shape[0]
    tn = n // 16
    return pl.pallas_call(
        _mod_kernel,
        out_shape=jax.ShapeDtypeStruct((depth, rows, n), F32),
        grid=(depth, n // tn),
        in_specs=[
            pl.BlockSpec((rows, d), lambda l, j: (0, 0)),
            pl.BlockSpec((None, d, tn), lambda l, j: (l, 0, j)),
            pl.BlockSpec((None, 1, tn), lambda l, j: (l, 0, j)),
        ],
        out_specs=pl.BlockSpec((None, rows, tn), lambda l, j: (l, 0, j)),
        compiler_params=_cp("arbitrary", "arbitrary"),
        name="modulation",
    )(cc, mod_w, mod_b.reshape(depth, 1, n))


def _ffn_kernel(x_ref, g_ref, sh_ref, sc_ref, gt_ref, wg_ref, wu_ref, wd_ref, o_ref, h_scr, acc_scr):
    j = pl.program_id(2)

    @pl.when(j == 0)
    def _():
        h_scr[...] = _modnorm(x_ref[...], g_ref[...], sh_ref[...], sc_ref[...]).astype(BF16)
        acc_scr[...] = jnp.zeros_like(acc_scr)

    h = h_scr[...]
    g = _dot(h, wg_ref[...])
    u = _dot(h, wu_ref[...])
    a = (g * jax.nn.sigmoid(g) * u).astype(BF16)
    acc_scr[...] += _dot(a, wd_ref[...])

    @pl.when(j == pl.num_programs(2) - 1)
    def _():
        o_ref[...] = x_ref[...] + FFN_RES * gt_ref[...] * acc_scr[...]


def _ffn(x, gain, shift, scale, gate, wg, wu, wd, tm=512):
    b, t, d = x.shape
    f = wg.shape[1]
    tm = min(tm, t)
    vec = pl.BlockSpec((None, 1, d), lambda bi, i, j: (bi, 0, 0))
    return pl.pallas_call(
        _ffn_kernel,
        out_shape=jax.ShapeDtypeStruct((b, t, d), F32),
        grid=(b, t // tm, f // FFN_TILE),
        in_specs=[
            pl.BlockSpec((None, tm, d), lambda bi, i, j: (bi, i, 0)),
            pl.BlockSpec((1, d), lambda bi, i, j: (0, 0)),
            vec, vec, vec,
            pl.BlockSpec((d, FFN_TILE), lambda bi, i, j: (0, j)),
            pl.BlockSpec((d, FFN_TILE), lambda bi, i, j: (0, j)),
            pl.BlockSpec((FFN_TILE, d), lambda bi, i, j: (j, 0)),
        ],
        out_specs=pl.BlockSpec((None, tm, d), lambda bi, i, j: (bi, i, 0)),
        scratch_shapes=[pltpu.VMEM((tm, d), BF16), pltpu.VMEM((tm, d), F32)],
        compiler_params=_cp("arbitrary", "arbitrary", "arbitrary"),
        name="ffn",
    )(x, gain.reshape(1, d), shift, scale, gate, wg, wu, wd)


def _nm_matmul_kernel(x_ref, g_ref, sh_ref, sc_ref, w_ref, *rest, rope_chunks, scaled_chunks, out_scale):
    if rope_chunks:
        cos_ref, sin_ref, o_ref, h_scr = rest
    else:
        o_ref, h_scr = rest

    @pl.when(pl.program_id(2) == 0)
    def _():
        h_scr[...] = _modnorm(x_ref[...], g_ref[...], sh_ref[...], sc_ref[...]).astype(BF16)

    y = _dot(h_scr[...], w_ref[...])
    if not rope_chunks:
        o_ref[...] = y.astype(o_ref.dtype)
        return
    cos = cos_ref[...]
    sin = sin_ref[...]
    for c in range(y.shape[1] // LANES):
        yc = y[:, c * LANES:(c + 1) * LANES]
        if c < rope_chunks:
            yc = _rope128(yc, cos, sin)
        if c < scaled_chunks:
            yc = yc * out_scale
        o_ref[:, c * LANES:(c + 1) * LANES] = yc.astype(o_ref.dtype)


def _nm_matmul(x, gain, shift, scale, w, out_dtype, tn, tm=512, rope=None):
    b, t, d = x.shape
    n = w.shape[1]
    tm = min(tm, t)
    vec = pl.BlockSpec((None, 1, d), lambda bi, i, j: (bi, 0, 0))
    in_specs = [
        pl.BlockSpec((None, tm, d), lambda bi, i, j: (bi, i, 0)),
        pl.BlockSpec((1, d), lambda bi, i, j: (0, 0)),
        vec, vec,
        pl.BlockSpec((d, tn), lambda bi, i, j: (0, j)),
    ]
    args = [x, gain.reshape(1, d), shift, scale, w]
    rope_chunks = scaled_chunks = 0
    out_scale = 1.0
    if rope is not None:
        cos, sin, rope_chunks, scaled_chunks, out_scale = rope
        in_specs += [pl.BlockSpec((tm, LANES), lambda bi, i, j: (i, 0))] * 2
        args += [cos, sin]
    return pl.pallas_call(
        functools.partial(_nm_matmul_kernel, rope_chunks=rope_chunks, scaled_chunks=scaled_chunks,
                          out_scale=out_scale),
        out_shape=jax.ShapeDtypeStruct((b, t, n), out_dtype),
        grid=(b, t // tm, n // tn),
        in_specs=in_specs,
        out_specs=pl.BlockSpec((None, tm, tn), lambda bi, i, j: (bi, i, j)),
        scratch_shapes=[pltpu.VMEM((tm, d), BF16)],
        compiler_params=_cp("arbitrary", "arbitrary", "arbitrary"),
        name="norm_matmul",
    )(*args)


def _mm_res_kernel(a_ref, w_ref, r_ref, gt_ref, o_ref):
    o_ref[...] = r_ref[...] + gt_ref[...] * _dot(a_ref[...], w_ref[...])


def _mm_res(a, w, res, gate, tm=512):
    b, t, k = a.shape
    n = w.shape[1]
    tm = min(tm, t)
    return pl.pallas_call(
        _mm_res_kernel,
        out_shape=jax.ShapeDtypeStruct((b, t, n), F32),
        grid=(b, t // tm),
        in_specs=[
            pl.BlockSpec((None, tm, k), lambda bi, i: (bi, i, 0)),
            pl.BlockSpec((k, n), lambda bi, i: (0, 0)),
            pl.BlockSpec((None, tm, n), lambda bi, i: (bi, i, 0)),
            pl.BlockSpec((None, 1, n), lambda bi, i: (bi, 0, 0)),
        ],
        out_specs=pl.BlockSpec((None, tm, n), lambda bi, i: (bi, i, 0)),
        compiler_params=_cp("arbitrary", "arbitrary"),
        name="proj_residual",
    )(a, w, res, gate)


def _final_norm_kernel(x_ref, g_ref, o_ref):
    o_ref[...] = _rms(x_ref[...], g_ref[...])


def _final_norm(x, gain, tm=512):
    b, t, d = x.shape
    return pl.pallas_call(
        _final_norm_kernel,
        out_shape=jax.ShapeDtypeStruct((b, t, d), F32),
        grid=(b, t // tm),
        in_specs=[pl.BlockSpec((None, tm, d), lambda bi, i: (bi, i, 0)),
                  pl.BlockSpec((1, d), lambda bi, i: (0, 0))],
        out_specs=pl.BlockSpec((None, tm, d), lambda bi, i: (bi, i, 0)),
        compiler_params=_cp("arbitrary", "arbitrary"),
        name="final_norm",
    )(x, gain.reshape(1, d))


def _mla_proj_kernel(x_ref, g_ref, sh_ref, sc_ref, wd_ref, gq_ref, gkv_ref, wuq_ref, wukv_ref, cos_ref, sin_ref,
                     q_ref, k_ref, v_ref, *, q_scale):
    h = _modnorm(x_ref[...], g_ref[...], sh_ref[...], sc_ref[...]).astype(BF16)
    a = _dot(h, wd_ref[...])
    cq = _rms(a[:, :MLA_RANK], gq_ref[...]).astype(BF16)
    ckv = _rms(a[:, MLA_RANK:2 * MLA_RANK], gkv_ref[...]).astype(BF16)
    cos = cos_ref[...]
    sin = sin_ref[...]
    kr = _rope128(a[:, 2 * MLA_RANK:], cos, sin).astype(BF16)
    q = _dot(cq, wuq_ref[...])
    kv = _dot(ckv, wukv_ref[...])
    for hd in range(MLA_HEADS):
        base = hd * MLA_HEAD_PAD
        q_ref[hd, :, :LANES] = (q[:, base:base + LANES] * q_scale).astype(BF16)
        q_ref[hd, :, LANES:] = (_rope128(q[:, base + LANES:base + 2 * LANES], cos, sin) * q_scale).astype(BF16)
        k_ref[hd, :, :LANES] = kv[:, hd * MLA_NOPE:(hd + 1) * MLA_NOPE].astype(BF16)
        k_ref[hd, :, LANES:] = kr
        voff = MLA_HEADS * MLA_NOPE + hd * MLA_V
        v_ref[hd] = kv[:, voff:voff + MLA_V].astype(BF16)


def _mla_proj(x, gain, shift, scale, wd, gq, gkv, wuq, wukv, cos, sin, tm=256):
    b, t, d = x.shape
    tm = min(tm, t)
    vec = pl.BlockSpec((None, 1, d), lambda bi, i: (bi, 0, 0))
    const = lambda bi, i: (0, 0)
    head_out = lambda w: pl.BlockSpec((None, MLA_HEADS, tm, w), lambda bi, i: (bi, 0, i, 0))
    return pl.pallas_call(
        functools.partial(_mla_proj_kernel, q_scale=(MLA_NOPE + MLA_ROPE) ** -0.5),
        out_shape=[jax.ShapeDtypeStruct((b, MLA_HEADS, t, MLA_HEAD_PAD), BF16),
                   jax.ShapeDtypeStruct((b, MLA_HEADS, t, MLA_HEAD_PAD), BF16),
                   jax.ShapeDtypeStruct((b, MLA_HEADS, t, MLA_V), BF16)],
        grid=(b, t // tm),
        in_specs=[
            pl.BlockSpec((None, tm, d), lambda bi, i: (bi, i, 0)),
            pl.BlockSpec((1, d), const),
            vec, vec,
            _single(wd.shape, const),
            pl.BlockSpec((1, MLA_RANK), const),
            pl.BlockSpec((1, MLA_RANK), const),
            _single(wuq.shape, const),
            _single(wukv.shape, const),
            pl.BlockSpec((tm, LANES), lambda bi, i: (i, 0)),
            pl.BlockSpec((tm, LANES), lambda bi, i: (i, 0)),
        ],
        out_specs=[head_out(MLA_HEAD_PAD), head_out(MLA_HEAD_PAD), head_out(MLA_V)],
        compiler_params=_cp("arbitrary", "arbitrary"),
        name="mla_proj",
    )(x, gain.reshape(1, d), shift, scale, wd, gq.reshape(1, -1), gkv.reshape(1, -1), wuq, wukv, cos, sin)


def _attn_kernel(*refs, group, chunk, has_sink, head_major):
    if has_sink:
        sink_ref, q_ref, k_ref, v_ref, o_ref, s_scr = refs
    else:
        q_ref, k_ref, v_ref, o_ref, s_scr = refs
    tk = k_ref.shape[0]
    dv = v_ref.shape[1]
    for g in range(group):
        q = q_ref[g]
        for c in range(tk // chunk):
            s_scr[:, c * chunk:(c + 1) * chunk] = _dot_nt(q, k_ref[c * chunk:(c + 1) * chunk, :])
        s = s_scr[...]
        m = jnp.max(s, axis=1, keepdims=True)
        if has_sink:
            sk = sink_ref[pl.program_id(1) * group + g]
            m = jnp.maximum(m, sk)
        p = jnp.exp(s - m)
        l = jnp.sum(p, axis=1, keepdims=True)
        if has_sink:
            l = l + jnp.exp(sk - m)
        o = _dot(p.astype(BF16), v_ref[...]) / l
        if head_major:
            o_ref[g] = o.astype(o_ref.dtype)
        else:
            o_ref[:, g * dv:(g + 1) * dv] = o.astype(o_ref.dtype)


def _attention(q, k, v, tk, sink=None, tq=256, head_major=False):
    b, hq, t, dk = q.shape
    hk = k.shape[1]
    dv = v.shape[3]
    group = hq // hk
    tq = min(tq, t)
    chunk = 256
    in_specs = [
        pl.BlockSpec((None, group, tq, dk), lambda bi, h, i: (bi, h, i, 0)),
        pl.BlockSpec((None, None, tk, dk), lambda bi, h, i: (bi, h, 0, 0)),
        pl.BlockSpec((None, None, tk, dv), lambda bi, h, i: (bi, h, 0, 0)),
    ]
    args = [q, k, v]
    if sink is not None:
        in_specs = [pl.BlockSpec(memory_space=pltpu.SMEM)] + in_specs
        args = [sink] + args
    if head_major:
        out_shape = jax.ShapeDtypeStruct((b, hq, t, dv), BF16)
        out_spec = pl.BlockSpec((None, group, tq, dv), lambda bi, h, i: (bi, h, i, 0))
    else:
        out_shape = jax.ShapeDtypeStruct((b, t, hq * dv), BF16)
        out_spec = pl.BlockSpec((None, tq, group * dv), lambda bi, h, i: (bi, i, h))
    return pl.pallas_call(
        functools.partial(_attn_kernel, group=group, chunk=chunk, has_sink=sink is not None,
                          head_major=head_major),
        out_shape=out_shape,
        grid=(b, hk, t // tq),
        in_specs=in_specs,
        out_specs=out_spec,
        scratch_shapes=[pltpu.VMEM((tq, tk), F32)],
        compiler_params=_cp("arbitrary", "arbitrary", "arbitrary"),
        name="attention",
    )(*args)


def _window_kernel(sink_ref, q_ref, kp_ref, kc_ref, kn_ref, vp_ref, vc_ref, vn_ref, kx_ref, vx_ref, o_ref, *,
                   seq_len):
    i = pl.program_id(2)
    hk = pl.program_id(1)
    n_ctx = kx_ref.shape[0]
    rows = WIN_GROUP * Q_BLOCK
    q = q_ref[...].reshape(rows, WIN_HEAD_DIM)
    kk = jnp.concatenate([kx_ref[...], kp_ref[...], kc_ref[...], kn_ref[...]], axis=0)
    vv = jnp.concatenate([vx_ref[...], vp_ref[...], vc_ref[...], vn_ref[...]], axis=0)
    s = _dot_nt(q, kk)
    r = lax.broadcasted_iota(jnp.int32, s.shape, 0)
    c = lax.broadcasted_iota(jnp.int32, s.shape, 1)
    rel = c - n_ctx - WINDOW - r % Q_BLOCK
    k_pos = i * Q_BLOCK - WINDOW + (c - n_ctx)
    ok = (c < n_ctx) | ((jnp.abs(rel) <= WINDOW) & (k_pos >= 0) & (k_pos < seq_len))
    s = jnp.where(ok, s, NEG_BIG)
    head = lax.broadcasted_iota(jnp.int32, (rows, 1), 0) // Q_BLOCK
    sk = jnp.zeros((rows, 1), F32)
    for g in range(WIN_GROUP):
        sk = jnp.where(head == g, sink_ref[hk * WIN_GROUP + g], sk)
    m = jnp.maximum(jnp.max(s, axis=1, keepdims=True), sk)
    e = jnp.exp(s - m)
    l = jnp.sum(e, axis=1, keepdims=True) + jnp.exp(sk - m)
    o = _dot(e.astype(BF16), vv) / l
    o_ref[...] = o.reshape(WIN_GROUP, Q_BLOCK, WIN_HEAD_DIM).astype(o_ref.dtype)


def _window_attention(q, k, v, kx, vx, sink):
    b, hq, s, hd = q.shape
    n_ctx = kx.shape[2]
    nblk = s // Q_BLOCK
    blk = lambda f: pl.BlockSpec((None, None, Q_BLOCK, hd), f)
    prev = lambda bi, h, i: (bi, h, jnp.maximum(i - 1, 0), 0)
    cur = lambda bi, h, i: (bi, h, i, 0)
    nxt = lambda bi, h, i: (bi, h, jnp.minimum(i + 1, nblk - 1), 0)
    ctx_spec = pl.BlockSpec((None, None, n_ctx, hd), lambda bi, h, i: (bi, h, 0, 0))
    return pl.pallas_call(
        functools.partial(_window_kernel, seq_len=s),
        out_shape=jax.ShapeDtypeStruct((b, hq, s, hd), BF16),
        grid=(b, WIN_KV_HEADS, nblk),
        in_specs=[pl.BlockSpec(memory_space=pltpu.SMEM),
                  pl.BlockSpec((None, WIN_GROUP, Q_BLOCK, hd), lambda bi, h, i: (bi, h, i, 0)),
                  blk(prev), blk(cur), blk(nxt), blk(prev), blk(cur), blk(nxt), ctx_spec, ctx_spec],
        out_specs=pl.BlockSpec((None, WIN_GROUP, Q_BLOCK, hd), lambda bi, h, i: (bi, h, i, 0)),
        compiler_params=_cp("arbitrary", "arbitrary", "arbitrary"),
        name="window_attention",
    )(sink, q, k, k, k, v, v, v, kx, vx)


def _shortconv_kernel(u_ref, w_ref, b_ref, o_ref):
    u = u_ref[...]
    rows = u.shape[0]
    t = lax.broadcasted_iota(jnp.int32, u.shape, 0)
    before = jnp.where(t == 0, 0.0, pltpu.roll(u, 1, 0))
    after = jnp.where(t == rows - 1, 0.0, pltpu.roll(u, rows - 1, 0))
    o_ref[...] = before * w_ref[0:1, :] + u * w_ref[1:2, :] + after * w_ref[2:3, :] + b_ref[...]


def _shortconv(u, w, bias, tc=256):
    b, t, n = u.shape
    return pl.pallas_call(
        _shortconv_kernel,
        out_shape=jax.ShapeDtypeStruct((b, t, n), F32),
        grid=(b, n // tc),
        in_specs=[pl.BlockSpec((None, t, tc), lambda bi, j: (bi, 0, j)),
                  pl.BlockSpec((3, tc), lambda bi, j: (0, j)),
                  pl.BlockSpec((1, tc), lambda bi, j: (0, j))],
        out_specs=pl.BlockSpec((None, t, tc), lambda bi, j: (bi, 0, j)),
        compiler_params=_cp("arbitrary", "arbitrary"),
        name="short_conv",
    )(u, w, bias.reshape(1, n))


def _filter_kernel(z_ref, t_ref, w1_ref, b1_ref, w2_ref, b2_ref, w3_ref, b3_ref, fr_ref, w4_ref, dec_ref, o_ref, *,
                   zero_row):
    def dense(hv, w_ref):
        a_hi, a_lo = _split(hv)
        w_hi, w_lo = _split(w_ref[...])
        return _dot3(a_hi, a_lo, w_hi, w_lo)

    freq = fr_ref[...]
    hv = jnp.sin(freq * (dense(z_ref[...], w1_ref) + b1_ref[...]))
    hv = jnp.sin(freq * (dense(hv, w2_ref) + b2_ref[...]))
    hv = jnp.sin(freq * (dense(hv, w3_ref) + b3_ref[...]))
    out = dense(hv, w4_ref) * jnp.exp(-t_ref[...] * jnp.abs(dec_ref[...]))
    tr = out.shape[0]
    row = pl.program_id(1) * tr + lax.broadcasted_iota(jnp.int32, out.shape, 0)
    o_ref[...] = jnp.where(row == zero_row, 0.0, out)


def _hyena_filter(seq, f_w1, f_b1, f_w2, f_b2, f_w3, f_b3, f_freq, f_w4, decay):
    d = f_w4.shape[1] // (2 * HY_ORDER)
    n = 2 * seq
    t = np.linspace(0.0, 1.0, seq, dtype=np.float32)[:, None]
    w = (2.0 * math.pi * np.arange(seq, dtype=np.float32)[:, None] / seq).astype(np.float32)
    f = np.linspace(1e-4, HY_BANDS - 1, HY_BANDS, dtype=np.float32)[None, :]
    z = np.concatenate([t, np.cos(f * w), -np.sin(f * w)], axis=-1).astype(np.float32)
    lag = np.arange(n)
    lag = np.where(lag < seq, lag, n - lag) % seq
    zc = np.zeros((n, LANES), np.float32)
    zc[:, :HY_EMB] = z[lag]
    tc = t[lag]
    w1 = jnp.zeros((LANES, HY_FILTER_W), F32).at[:HY_EMB].set(f_w1)
    tr = min(512, seq)
    fw = HY_FILTER_W
    row = lambda v: v.reshape(1, -1)
    const = lambda o, i: (0, 0)
    col = lambda o, i: (0, 2 * o + (i * tr) // seq)
    return pl.pallas_call(
        functools.partial(_filter_kernel, zero_row=seq),
        out_shape=jax.ShapeDtypeStruct((HY_ORDER, n, d), F32),
        grid=(HY_ORDER, n // tr),
        in_specs=[pl.BlockSpec((tr, LANES), lambda o, i: (i, 0)),
                  pl.BlockSpec((tr, 1), lambda o, i: (i, 0)),
                  pl.BlockSpec((LANES, fw), const), pl.BlockSpec((1, fw), const),
                  pl.BlockSpec((fw, fw), const), pl.BlockSpec((1, fw), const),
                  pl.BlockSpec((fw, fw), const), pl.BlockSpec((1, fw), const),
                  pl.BlockSpec((1, fw), const),
                  pl.BlockSpec((fw, d), col),
                  pl.BlockSpec((1, d), col)],
        out_specs=pl.BlockSpec((None, tr, d), lambda o, i: (o, i, 0)),
        compiler_params=_cp("arbitrary", "arbitrary"),
        name="hyena_filter",
    )(jnp.asarray(zc), jnp.asarray(tc), w1, row(f_b1), f_w2, row(f_b2), f_w3, row(f_b3), row(f_freq), f_w4,
      row(decay))


def _stack_complex(c):
    return np.block([[c.real, -c.imag], [c.imag, c.real]])


def _hi_lo(m):
    m = jnp.asarray(np.asarray(m, np.float32))
    hi = m.astype(BF16)
    lo = (m - hi.astype(F32)).astype(BF16)
    return hi, lo


def _fft_split(n):
    n2 = 128
    return n // n2, n2


@functools.lru_cache(maxsize=None)
def _fft_matrices_np(n):
    n1, n2 = _fft_split(n)
    h = n1 // 2
    k1 = np.arange(n1)
    bb = np.arange(n2)
    tw = np.exp(-2j * np.pi * np.outer(bb, k1) / n)
    f1 = np.exp(-2j * np.pi * np.outer(k1, np.arange(n1)) / n1)
    la = np.stack([_stack_complex(tw[b][:, None] * f1[:, :h]) for b in range(n2)])
    lf = np.stack([np.concatenate([(tw[b][:, None] * f1).real, (tw[b][:, None] * f1).imag], axis=0)
                   for b in range(n2)])
    f2 = np.exp(-2j * np.pi * np.outer(np.arange(n2), np.arange(n2)) / n2)
    lc = _stack_complex(f2)
    lci = _stack_complex(np.conj(f2))
    f1i = np.exp(2j * np.pi * np.outer(np.arange(h), k1) / n1)
    lai = np.stack([_stack_complex(f1i * np.conj(tw[b])[None, :] / n) for b in range(n2)])
    return la, lf, lc, lci, lai


def _fwd_strided_stage(read_rows, mat_hi_ref, mat_lo_ref, t_ref, n1, n2):
    def body(b, carry):
        hi, lo = _split(read_rows(b))
        out = _dot3(mat_hi_ref[b], mat_lo_ref[b], hi, lo)
        off = pl.multiple_of(b * n1, n1)
        t_ref[0, pl.ds(off, n1), :] = out[:n1]
        t_ref[1, pl.ds(off, n1), :] = out[n1:]
        return carry
    lax.fori_loop(0, n2, body, 0)


def _gather_stacked(ref, start, count, stride):
    return jnp.concatenate([ref[0, pl.ds(start, count, stride=stride), :],
                            ref[1, pl.ds(start, count, stride=stride), :]], axis=0)


def _spectrum_kernel(f_ref, lf_hi, lf_lo, lc_hi, lc_lo, h_ref, t_scr, *, n1, n2):
    _fwd_strided_stage(lambda b: f_ref[pl.ds(b, n1, stride=n2), :], lf_hi, lf_lo, t_scr, n1, n2)
    c_hi = lc_hi[...]
    c_lo = lc_lo[...]

    def body(k1, carry):
        hi, lo = _split(_gather_stacked(t_scr, k1, n2, n1))
        out = _dot3(c_hi, c_lo, hi, lo)
        off = pl.multiple_of(k1 * n2, n2)
        h_ref[0, pl.ds(off, n2), :] = out[:n2]
        h_ref[1, pl.ds(off, n2), :] = out[n2:]
        return carry
    lax.fori_loop(0, n1, body, 0)


def _spectrum(filt, td=LANES):
    order, n, d = filt.shape
    n1, n2 = _fft_split(n)
    _, lf, lc, _, _ = _fft_matrices_np(n)
    lf_hi, lf_lo = _hi_lo(lf)
    lc_hi, lc_lo = _hi_lo(lc)
    c3 = lambda o, j: (0, 0, 0)
    c2 = lambda o, j: (0, 0)
    return pl.pallas_call(
        functools.partial(_spectrum_kernel, n1=n1, n2=n2),
        out_shape=jax.ShapeDtypeStruct((order, 2, n, d), F32),
        grid=(order, d // td),
        in_specs=[pl.BlockSpec((None, n, td), lambda o, j: (o, 0, j)),
                  _single(lf_hi.shape, c3), _single(lf_lo.shape, c3),
                  _single(lc_hi.shape, c2), _single(lc_lo.shape, c2)],
        out_specs=pl.BlockSpec((None, 2, n, td), lambda o, j: (o, 0, 0, j)),
        scratch_shapes=[pltpu.VMEM((2, n, td), F32)],
        compiler_params=_cp("arbitrary", "arbitrary"),
        name="filter_spectrum",
    )(filt, lf_hi, lf_lo, lc_hi, lc_lo)


def _fftconv_kernel(z_ref, g_ref, h_ref, skip_ref, la_hi, la_lo, lc_hi, lc_lo, lci_hi, lci_lo, lai_hi, lai_lo,
                    o_ref, t_scr, u_scr, *, n1, n2):
    half = n1 // 2
    seq = half * n2
    _fwd_strided_stage(lambda b: _gather_stacked(z_ref, b, half, n2), la_hi, la_lo, t_scr, n1, n2)
    c_hi, c_lo = lc_hi[...], lc_lo[...]
    ci_hi, ci_lo = lci_hi[...], lci_lo[...]

    def mid(k1, carry):
        hi, lo = _split(_gather_stacked(t_scr, k1, n2, n1))
        x = _dot3(c_hi, c_lo, hi, lo)
        off = pl.multiple_of(k1 * n2, n2)
        xr, xi = x[:n2], x[n2:]
        hr = h_ref[0, pl.ds(off, n2), :]
        hi_ = h_ref[1, pl.ds(off, n2), :]
        y = jnp.concatenate([xr * hr - xi * hi_, xr * hi_ + xi * hr], axis=0)
        y_hi, y_lo = _split(y)
        w = _dot3(ci_hi, ci_lo, y_hi, y_lo)
        u_scr[0, pl.ds(off, n2), :] = w[:n2]
        u_scr[1, pl.ds(off, n2), :] = w[n2:]
        return carry
    lax.fori_loop(0, n1, mid, 0)

    def inv(b, carry):
        hi, lo = _split(_gather_stacked(u_scr, b, n1, n2))
        out = _dot3(lai_hi[b], lai_lo[b], hi, lo)
        off = pl.multiple_of(b * half, half)
        t_scr[0, pl.ds(off, half), :] = out[:half]
        t_scr[1, pl.ds(off, half), :] = out[half:]
        return carry
    lax.fori_loop(0, n2, inv, 0)

    skip = skip_ref[...]

    def epi(a, carry):
        off = pl.multiple_of(a * n2, n2)
        for plane in range(2):
            y = t_scr[plane, pl.ds(a, n2, stride=half), :]
            zc = z_ref[plane, pl.ds(off, n2), :]
            gc = g_ref[plane, pl.ds(off, n2), :]
            o_ref[plane, pl.ds(off, n2), :] = (gc * (y + zc * skip)).astype(o_ref.dtype)
        return carry
    lax.fori_loop(0, half, epi, 0)
    del seq


def _fftconv(z_arr, z_col, g_arr, g_col, spec, order, skip, out_dtype, td=LANES):
    b, seq, _ = z_arr.shape
    d = spec.shape[3]
    n = 2 * seq
    n1, n2 = _fft_split(n)
    nblk = d // td
    la, _, lc, lci, lai = _fft_matrices_np(n)
    mats = [*_hi_lo(la), *_hi_lo(lc), *_hi_lo(lci), *_hi_lo(lai)]
    mat_specs = [_single(m.shape, (lambda j, p, nd=m.ndim: (0,) * nd)) for m in mats]
    return pl.pallas_call(
        functools.partial(_fftconv_kernel, n1=n1, n2=n2),
        out_shape=jax.ShapeDtypeStruct((b, seq, d), out_dtype),
        grid=(nblk, b // 2),
        in_specs=[_single((2, seq, td), lambda j, p: (p, 0, z_col * nblk + j)),
                  _single((2, seq, td), lambda j, p: (p, 0, g_col * nblk + j)),
                  _single((None, 2, n, td), lambda j, p: (order, 0, 0, j)),
                  pl.BlockSpec((None, 1, td), lambda j, p: (order, 0, j))] + mat_specs,
        out_specs=_single((2, seq, td), lambda j, p: (p, 0, j)),
        scratch_shapes=[pltpu.VMEM((2, n, td), F32), pltpu.VMEM((2, n, td), F32)],
        compiler_params=_cp("arbitrary", "arbitrary"),
        name="fft_conv",
    )(z_arr, g_arr, spec, skip.reshape(HY_ORDER, 1, d), *mats)


@functools.lru_cache(maxsize=None)
def _dense_dft_np(seq):
    n = 2 * seq
    k = np.arange(n)
    f = np.exp(-2j * np.pi * np.outer(k, k) / n)
    fwd = _stack_complex(f[:, :seq])
    flt = np.concatenate([f.real, f.imag], axis=0)
    inv = _stack_complex(np.conj(f)[:seq, :] / n)
    return fwd, flt, inv


def _dense_spectrum_kernel(f_ref, m_hi, m_lo, h_ref):
    hi, lo = _split(f_ref[...])
    out = _dot3(m_hi[...], m_lo[...], hi, lo)
    n = f_ref.shape[0]
    h_ref[0] = out[:n]
    h_ref[1] = out[n:]


def _dense_spectrum(filt, td=256):
    order, n, d = filt.shape
    _, flt, _ = _dense_dft_np(n // 2)
    m_hi, m_lo = _hi_lo(flt)
    c2 = lambda o, j: (0, 0)
    return pl.pallas_call(
        _dense_spectrum_kernel,
        out_shape=jax.ShapeDtypeStruct((order, 2, n, d), F32),
        grid=(order, d // td),
        in_specs=[pl.BlockSpec((None, n, td), lambda o, j: (o, 0, j)),
                  pl.BlockSpec(m_hi.shape, c2), pl.BlockSpec(m_lo.shape, c2)],
        out_specs=pl.BlockSpec((None, 2, n, td), lambda o, j: (o, 0, 0, j)),
        compiler_params=_cp("arbitrary", "arbitrary"),
        name="filter_spectrum_dense",
    )(filt, m_hi, m_lo)


def _dense_conv_kernel(z_ref, g_ref, h_ref, skip_ref, f_hi, f_lo, i_hi, i_lo, o_ref):
    seq = z_ref.shape[1]
    n = 2 * seq
    zz = jnp.concatenate([z_ref[0], z_ref[1]], axis=0)
    hi, lo = _split(zz)
    x = _dot3(f_hi[...], f_lo[...], hi, lo)
    xr, xi = x[:n], x[n:]
    hr, hi_ = h_ref[0], h_ref[1]
    y = jnp.concatenate([xr * hr - xi * hi_, xr * hi_ + xi * hr], axis=0)
    y_hi, y_lo = _split(y)
    w = _dot3(i_hi[...], i_lo[...], y_hi, y_lo)
    skip = skip_ref[...]
    for plane in range(2):
        y_p = w[plane * seq:(plane + 1) * seq]
        o_ref[plane] = (g_ref[plane] * (y_p + z_ref[plane] * skip)).astype(o_ref.dtype)


def _dense_conv(z_arr, z_col, g_arr, g_col, spec, order, skip, out_dtype, td=256):
    b, seq, _ = z_arr.shape
    d = spec.shape[3]
    n = 2 * seq
    nblk = d // td
    fwd, _, inv = _dense_dft_np(seq)
    mats = [*_hi_lo(fwd), *_hi_lo(inv)]
    c2 = lambda j, p: (0, 0)
    return pl.pallas_call(
        _dense_conv_kernel,
        out_shape=jax.ShapeDtypeStruct((b, seq, d), out_dtype),
        grid=(nblk, b // 2),
        in_specs=[pl.BlockSpec((2, seq, td), lambda j, p: (p, 0, z_col * nblk + j)),
                  pl.BlockSpec((2, seq, td), lambda j, p: (p, 0, g_col * nblk + j)),
                  pl.BlockSpec((None, 2, n, td), lambda j, p: (order, 0, 0, j)),
                  pl.BlockSpec((None, 1, td), lambda j, p: (order, 0, j))]
                 + [pl.BlockSpec(m.shape, c2) for m in mats],
        out_specs=pl.BlockSpec((2, seq, td), lambda j, p: (p, 0, j)),
        compiler_params=_cp("arbitrary", "arbitrary"),
        name="dense_conv",
    )(z_arr, g_arr, spec, skip.reshape(HY_ORDER, 1, d), *mats)


DENSE_DFT_MAX_SEQ = 512


def _hyena(x, gain, shift, scale, p):
    b, seq, d = x.shape
    u = _nm_matmul(x, gain, shift, scale, p["w_in"], F32, tn=1024)
    u = _shortconv(u, p["conv_w"], p["conv_b"])
    filt = _hyena_filter(seq, *p["filter"])
    dense = seq <= DENSE_DFT_MAX_SEQ
    spec = _dense_spectrum(filt) if dense else _spectrum(filt)
    conv = _dense_conv if dense else _fftconv
    z1 = conv(u, 0, u, 1, spec, 0, p["skip"], F32)
    return conv(z1, 0, u, 2, spec, 1, p["skip"], BF16)


def _rope_tables(seq):
    t = np.arange(seq)
    pos = np.stack([t // GRID_W, t % GRID_W], axis=1).astype(np.float32)
    n = 16
    inv = jnp.asarray(ROPE_THETA, F32) ** (-jnp.arange(n, dtype=F32) / n)
    lane = np.arange(LANES)
    axis = (lane % 64) // 32
    idx = lane % 16
    sign = np.where(lane % 32 < 16, -1.0, 1.0).astype(np.float32)
    ang = jnp.asarray(pos)[:, axis] * inv[idx][None, :]
    return jnp.cos(ang), jnp.sin(ang) * sign[None, :]


def _identity_tables(seq):
    return jnp.ones((seq, LANES), F32), jnp.zeros((seq, LANES), F32)


def _mla_weights(w_dq, w_uq, w_dkv, w_ukv):
    d = w_dq.shape[0]
    hh = MLA_HEADS
    wd = jnp.concatenate([w_dq, w_dkv, jnp.zeros((d, LANES - MLA_ROPE), F32)], axis=1).astype(BF16)
    uq = w_uq.reshape(MLA_RANK, hh, MLA_NOPE + MLA_ROPE)
    uq = jnp.pad(uq, ((0, 0), (0, 0), (0, MLA_HEAD_PAD - MLA_NOPE - MLA_ROPE)))
    ukv = w_ukv.reshape(MLA_RANK, hh, MLA_NOPE + MLA_V)
    ukv = jnp.concatenate([ukv[:, :, :MLA_NOPE].reshape(MLA_RANK, hh * MLA_NOPE),
                           ukv[:, :, MLA_NOPE:].reshape(MLA_RANK, hh * MLA_V)], axis=1)
    return wd, uq.reshape(MLA_RANK, hh * MLA_HEAD_PAD).astype(BF16), ukv.astype(BF16)


def _ffn_weights(wg, wu, wd):
    f = wg.shape[1]
    pad = (-f) % FFN_TILE
    return (jnp.pad(wg, ((0, 0), (0, pad))).astype(BF16), jnp.pad(wu, ((0, 0), (0, pad))).astype(BF16),
            jnp.pad(wd, ((0, pad), (0, 0))).astype(BF16))


def _to_heads(a, heads):
    b, t, _ = a.shape
    return a.reshape(b, t, heads, -1).transpose(0, 2, 1, 3)


def _from_heads(a):
    b, h, t, e = a.shape
    return a.transpose(0, 2, 1, 3).reshape(b, t, h * e)


def kernel(x, c, ctx, c_ctx, mod_w, mod_b, norm_g, final_g, ffn_wg, ffn_wu, ffn_wd, mla_w_dq, mla_g_q, mla_w_uq, mla_w_dkv, mla_g_kv, mla_w_ukv, mla_w_o, hy_w_in, hy_conv_w, hy_conv_b, hy_f_w1, hy_f_b1, hy_f_w2, hy_f_b2, hy_f_w3, hy_f_b3, hy_f_freq, hy_f_w4, hy_decay, hy_skip, hy_w_out, win_w_qkv, win_sink, win_w_o):
    b, s, d = x.shape
    n_ctx = ctx.shape[1]
    depth = mod_w.shape[0]
    assert b % 2 == 0 and s % 512 == 0 and n_ctx % 256 == 0

    cc = jnp.concatenate([c, c_ctx[None, :], jnp.zeros((8 - (b + 1) % 8, d), F32)], axis=0)
    mods = _modulation(cc, mod_w, mod_b).reshape(depth, cc.shape[0], N_MOD, d)

    cos_l, sin_l = _rope_tables(s)
    cos_c, sin_c = _identity_tables(n_ctx)

    for i in range(depth):
        kind, j = i % N_MIXERS, i // N_MIXERS
        need_ctx = i < depth - 1
        ctx_live = need_ctx or kind != 1
        ml = [mods[i, :b, k][:, None, :] for k in range(N_MOD)]
        mc = [jnp.broadcast_to(mods[i, b, k][None, None, :], (b, 1, d)) for k in range(N_MOD)]
        w1 = _ffn_weights(ffn_wg[i, 0], ffn_wu[i, 0], ffn_wd[i, 0])
        w2 = _ffn_weights(ffn_wg[i, 1], ffn_wu[i, 1], ffn_wd[i, 1])

        x = _ffn(x, norm_g[i, 0], ml[0], ml[1], ml[2], *w1)
        if ctx_live:
            ctx = _ffn(ctx, norm_g[i, 0], mc[0], mc[1], mc[2], *w1)

        g1 = norm_g[i, 1]
        o_c = None
        if kind == 0:
            wd, wuq, wukv = _mla_weights(mla_w_dq[j], mla_w_uq[j], mla_w_dkv[j], mla_w_ukv[j])
            q_l, k_l, v_l = _mla_proj(x, g1, ml[3], ml[4], wd, mla_g_q[j], mla_g_kv[j], wuq, wukv, cos_l, sin_l)
            q_c, k_c, v_c = _mla_proj(ctx, g1, mc[3], mc[4], wd, mla_g_q[j], mla_g_kv[j], wuq, wukv, cos_c, sin_c)
            k_all = jnp.concatenate([k_c, k_l], axis=2)
            v_all = jnp.concatenate([v_c, v_l], axis=2)
            w_o = mla_w_o[j].astype(BF16)
            o_l = _attention(q_l, k_all, v_all, n_ctx + s)
            if need_ctx:
                o_c = _attention(q_c, k_c, v_c, n_ctx)
        elif kind == 1:
            hp = dict(w_in=hy_w_in[j].astype(BF16), conv_w=hy_conv_w[j], conv_b=hy_conv_b[j],
                      filter=(hy_f_w1[j], hy_f_b1[j], hy_f_w2[j], hy_f_b2[j], hy_f_w3[j], hy_f_b3[j],
                              hy_f_freq[j], hy_f_w4[j], hy_decay[j]),
                      skip=hy_skip[j])
            w_o = hy_w_out[j].astype(BF16)
            o_l = _hyena(x, g1, ml[3], ml[4], hp)
            if need_ctx:
                o_c = _hyena(ctx, g1, mc[3], mc[4], hp)
        else:
            qw = WIN_Q_HEADS * WIN_HEAD_DIM
            kw = WIN_KV_HEADS * WIN_HEAD_DIM
            w_qkv = win_w_qkv[j].astype(BF16)
            w_o = win_w_o[j].astype(BF16)
            q_chunks = qw // LANES
            rope_l = (cos_l, sin_l, (qw + kw) // LANES, q_chunks, WIN_HEAD_DIM ** -0.5)
            rope_c = (cos_c, sin_c, (qw + kw) // LANES, q_chunks, WIN_HEAD_DIM ** -0.5)
            qkv_l = _nm_matmul(x, g1, ml[3], ml[4], w_qkv, BF16, tn=qw + 2 * kw, rope=rope_l)
            qkv_c = _nm_matmul(ctx, g1, mc[3], mc[4], w_qkv, BF16, tn=qw + 2 * kw, rope=rope_c)
            q_l = _to_heads(qkv_l[..., :qw], WIN_Q_HEADS)
            k_l = _to_heads(qkv_l[..., qw:qw + kw], WIN_KV_HEADS)
            v_l = _to_heads(qkv_l[..., qw + kw:], WIN_KV_HEADS)
            k_c = _to_heads(qkv_c[..., qw:qw + kw], WIN_KV_HEADS)
            v_c = _to_heads(qkv_c[..., qw + kw:], WIN_KV_HEADS)
            o_l = _from_heads(_window_attention(q_l, k_l, v_l, k_c, v_c, win_sink[j]))
            if need_ctx:
                q_c = _to_heads(qkv_c[..., :qw], WIN_Q_HEADS)
                o_c = _from_heads(_attention(q_c, k_c, v_c, n_ctx, sink=win_sink[j], head_major=True))

        x = _mm_res(o_l, w_o, x, ml[5])
        x = _ffn(x, norm_g[i, 2], ml[6], ml[7], ml[8], *w2)
        if need_ctx:
            ctx = _mm_res(o_c, w_o, ctx, mc[5])
            ctx = _ffn(ctx, norm_g[i, 2], mc[6], mc[7], mc[8], *w2)
    return _final_norm(x, final_g)
```

```python
import functools
import math

import numpy as np
import jax
import jax.numpy as jnp
from jax import lax
from jax.experimental import pallas as pl
from jax.experimental.pallas import tpu as pltpu

F32 = jnp.float32
BF16 = jnp.bfloat16

GRID_W = 64
N_MOD = 9
FFN_RES = 0.5
NORM_EPS = 1e-6
ROPE_THETA = 10000.0
N_MIXERS = 3
MLA_HEADS = 16
MLA_NOPE = 128
MLA_ROPE = 64
MLA_V = 128
MLA_RANK = 512
MLA_HEAD_PAD = 256
HY_ORDER = 2
HY_EMB = 33
HY_BANDS = (HY_EMB - 1) // 2
HY_FILTER_W = 64
WIN_HEAD_DIM = 64
WIN_Q_HEADS = 32
WIN_KV_HEADS = 4
WIN_GROUP = WIN_Q_HEADS // WIN_KV_HEADS
WINDOW = 128
Q_BLOCK = 128

LANES = 128
V7X_VMEM_LIMIT = 56 * 1024 * 1024
FFN_TILE = 512
NEG_BIG = -1e30


def _cp(*sem):
    return pltpu.CompilerParams(dimension_semantics=sem, vmem_limit_bytes=V7X_VMEM_LIMIT)


def _single(shape, imap):
    return pl.BlockSpec(shape, imap, pipeline_mode=pl.Buffered(1))


def _dot(a, b):
    return jnp.dot(a, b, preferred_element_type=F32)


def _dot_nt(a, b):
    return lax.dot_general(a, b, (((1,), (1,)), ((), ())), preferred_element_type=F32)


def _split(x):
    hi = x.astype(BF16)
    lo = (x - hi.astype(F32)).astype(BF16)
    return hi, lo


def _dot3(a_hi, a_lo, b_hi, b_lo):
    return _dot(a_hi, b_hi) + (_dot(a_hi, b_lo) + _dot(a_lo, b_hi))


def _modnorm(x, gain, shift, scale):
    r = lax.rsqrt(jnp.mean(x * x, axis=-1, keepdims=True) + NORM_EPS)
    return (x * r * gain) * (1.0 + scale) + shift


def _rms(x, gain):
    return x * lax.rsqrt(jnp.mean(x * x, axis=-1, keepdims=True) + NORM_EPS) * gain


def _rope128(x, cos, sin):
    lane = lax.broadcasted_iota(jnp.int32, x.shape, 1)
    partner = jnp.where(lane % 32 < 16, pltpu.roll(x, LANES - 16, 1), pltpu.roll(x, 16, 1))
    return x * cos + partner * sin


def _mod_kernel(a_ref, w_ref, b_ref, o_ref):
    a = a_ref[...]
    a = a * jax.nn.sigmoid(a)
    a_hi, a_lo = _split(a)
    w_hi, w_lo = _split(w_ref[...])
    o_ref[...] = _dot3(a_hi, a_lo, w_hi, w_lo) + b_ref[...]


def _modulation(cc, mod_w, mod_b):
    depth, d, n = mod_w.shape
    rows = cc.shape[0]
    tn = n // 16
    return pl.pallas_call(
        _mod_kernel,
        out_shape=jax.ShapeDtypeStruct((depth, rows, n), F32),
        grid=(depth, n // tn),
        in_specs=[
            pl.BlockSpec((rows, d), lambda l, j: (0, 0)),
            pl.BlockSpec((None, d, tn), lambda l, j: (l, 0, j)),
            pl.BlockSpec((None, 1, tn), lambda l, j: (l, 0, j)),
        ],
        out_specs=pl.BlockSpec((None, rows, tn), lambda l, j: (l, 0, j)),
        compiler_params=_cp("arbitrary", "arbitrary"),
        name="modulation",
    )(cc, mod_w, mod_b.reshape(depth, 1, n))


def _ffn_kernel(x_ref, g_ref, sh_ref, sc_ref, gt_ref, wg_ref, wu_ref, wd_ref, o_ref, h_scr, acc_scr):
    j = pl.program_id(2)

    @pl.when(j == 0)
    def _():
        h_scr[...] = _modnorm(x_ref[...], g_ref[...], sh_ref[...], sc_ref[...]).astype(BF16)
        acc_scr[...] = jnp.zeros_like(acc_scr)

    h = h_scr[...]
    g = _dot(h, wg_ref[...])
    u = _dot(h, wu_ref[...])
    a = (g * jax.nn.sigmoid(g) * u).astype(BF16)
    acc_scr[...] += _dot(a, wd_ref[...])

    @pl.when(j == pl.num_programs(2) - 1)
    def _():
        o_ref[...] = x_ref[...] + FFN_RES * gt_ref[...] * acc_scr[...]


def _ffn(x, gain, shift, scale, gate, wg, wu, wd, tm=512):
    b, t, d = x.shape
    f = wg.shape[1]
    tm = min(tm, t)
    vec = pl.BlockSpec((None, 1, d), lambda bi, i, j: (bi, 0, 0))
    return pl.pallas_call(
        _ffn_kernel,
        out_shape=jax.ShapeDtypeStruct((b, t, d), F32),
        grid=(b, t // tm, f // FFN_TILE),
        in_specs=[
            pl.BlockSpec((None, tm, d), lambda bi, i, j: (bi, i, 0)),
            pl.BlockSpec((1, d), lambda bi, i, j: (0, 0)),
            vec, vec, vec,
            pl.BlockSpec((d, FFN_TILE), lambda bi, i, j: (0, j)),
            pl.BlockSpec((d, FFN_TILE), lambda bi, i, j: (0, j)),
            pl.BlockSpec((FFN_TILE, d), lambda bi, i, j: (j, 0)),
        ],
        out_specs=pl.BlockSpec((None, tm, d), lambda bi, i, j: (bi, i, 0)),
        scratch_shapes=[pltpu.VMEM((tm, d), BF16), pltpu.VMEM((tm, d), F32)],
        compiler_params=_cp("arbitrary", "arbitrary", "arbitrary"),
        name="ffn",
    )(x, gain.reshape(1, d), shift, scale, gate, wg, wu, wd)


def _nm_matmul_kernel(x_ref, g_ref, sh_ref, sc_ref, w_ref, *rest, rope_chunks, scaled_chunks, out_scale):
    if rope_chunks:
        cos_ref, sin_ref, o_ref, h_scr = rest
    else:
        o_ref, h_scr = rest

    @pl.when(pl.program_id(2) == 0)
    def _():
        h_scr[...] = _modnorm(x_ref[...], g_ref[...], sh_ref[...], sc_ref[...]).astype(BF16)

    y = _dot(h_scr[...], w_ref[...])
    if not rope_chunks:
        o_ref[...] = y.astype(o_ref.dtype)
        return
    cos = cos_ref[...]
    sin = sin_ref[...]
    for c in range(y.shape[1] // LANES):
        yc = y[:, c * LANES:(c + 1) * LANES]
        if c < rope_chunks:
            yc = _rope128(yc, cos, sin)
        if c < scaled_chunks:
            yc = yc * out_scale
        o_ref[:, c * LANES:(c + 1) * LANES] = yc.astype(o_ref.dtype)


def _nm_matmul(x, gain, shift, scale, w, out_dtype, tn, tm=512, rope=None):
    b, t, d = x.shape
    n = w.shape[1]
    tm = min(tm, t)
    vec = pl.BlockSpec((None, 1, d), lambda bi, i, j: (bi, 0, 0))
    in_specs = [
        pl.BlockSpec((None, tm, d), lambda bi, i, j: (bi, i, 0)),
        pl.BlockSpec((1, d), lambda bi, i, j: (0, 0)),
        vec, vec,
        pl.BlockSpec((d, tn), lambda bi, i, j: (0, j)),
    ]
    args = [x, gain.reshape(1, d), shift, scale, w]
    rope_chunks = scaled_chunks = 0
    out_scale = 1.0
    if rope is not None:
        cos, sin, rope_chunks, scaled_chunks, out_scale = rope
        in_specs += [pl.BlockSpec((tm, LANES), lambda bi, i, j: (i, 0))] * 2
        args += [cos, sin]
    return pl.pallas_call(
        functools.partial(_nm_matmul_kernel, rope_chunks=rope_chunks, scaled_chunks=scaled_chunks,
                          out_scale=out_scale),
        out_shape=jax.ShapeDtypeStruct((b, t, n), out_dtype),
        grid=(b, t // tm, n // tn),
        in_specs=in_specs,
        out_specs=pl.BlockSpec((None, tm, tn), lambda bi, i, j: (bi, i, j)),
        scratch_shapes=[pltpu.VMEM((tm, d), BF16)],
        compiler_params=_cp("arbitrary", "arbitrary", "arbitrary"),
        name="norm_matmul",
    )(*args)


def _mm_res_kernel(a_ref, w_ref, r_ref, gt_ref, o_ref):
    o_ref[...] = r_ref[...] + gt_ref[...] * _dot(a_ref[...], w_ref[...])


def _mm_res(a, w, res, gate, tm=512):
    b, t, k = a.shape
    n = w.shape[1]
    tm = min(tm, t)
    return pl.pallas_call(
        _mm_res_kernel,
        out_shape=jax.ShapeDtypeStruct((b, t, n), F32),
        grid=(b, t // tm),
        in_specs=[
            pl.BlockSpec((None, tm, k), lambda bi, i: (bi, i, 0)),
            pl.BlockSpec((k, n), lambda bi, i: (0, 0)),
            pl.BlockSpec((None, tm, n), lambda bi, i: (bi, i, 0)),
            pl.BlockSpec((None, 1, n), lambda bi, i: (bi, 0, 0)),
        ],
        out_specs=pl.BlockSpec((None, tm, n), lambda bi, i: (bi, i, 0)),
        compiler_params=_cp("arbitrary", "arbitrary"),
        name="proj_residual",
    )(a, w, res, gate)


def _final_norm_kernel(x_ref, g_ref, o_ref):
    o_ref[...] = _rms(x_ref[...], g_ref[...])


def _final_norm(x, gain, tm=512):
    b, t, d = x.shape
    return pl.pallas_call(
        _final_norm_kernel,
        out_shape=jax.ShapeDtypeStruct((b, t, d), F32),
        grid=(b, t // tm),
        in_specs=[pl.BlockSpec((None, tm, d), lambda bi, i: (bi, i, 0)),
                  pl.BlockSpec((1, d), lambda bi, i: (0, 0))],
        out_specs=pl.BlockSpec((None, tm, d), lambda bi, i: (bi, i, 0)),
        compiler_params=_cp("arbitrary", "arbitrary"),
        name="final_norm",
    )(x, gain.reshape(1, d))


def _mla_proj_kernel(x_ref, g_ref, sh_ref, sc_ref, wd_ref, gq_ref, gkv_ref, wuq_ref, wukv_ref, cos_ref, sin_ref,
                     q_ref, k_ref, v_ref, *, q_scale):
    h = _modnorm(x_ref[...], g_ref[...], sh_ref[...], sc_ref[...]).astype(BF16)
    a = _dot(h, wd_ref[...])
    cq = _rms(a[:, :MLA_RANK], gq_ref[...]).astype(BF16)
    ckv = _rms(a[:, MLA_RANK:2 * MLA_RANK], gkv_ref[...]).astype(BF16)
    cos = cos_ref[...]
    sin = sin_ref[...]
    kr = _rope128(a[:, 2 * MLA_RANK:], cos, sin).astype(BF16)
    q = _dot(cq, wuq_ref[...])
    kv = _dot(ckv, wukv_ref[...])
    for hd in range(MLA_HEADS):
        base = hd * MLA_HEAD_PAD
        q_ref[hd, :, :LANES] = (q[:, base:base + LANES] * q_scale).astype(BF16)
        q_ref[hd, :, LANES:] = (_rope128(q[:, base + LANES:base + 2 * LANES], cos, sin) * q_scale).astype(BF16)
        k_ref[hd, :, :LANES] = kv[:, hd * MLA_NOPE:(hd + 1) * MLA_NOPE].astype(BF16)
        k_ref[hd, :, LANES:] = kr
        voff = MLA_HEADS * MLA_NOPE + hd * MLA_V
        v_ref[hd] = kv[:, voff:voff + MLA_V].T.astype(BF16)


def _mla_proj(x, gain, shift, scale, wd, gq, gkv, wuq, wukv, cos, sin, tm=256):
    b, t, d = x.shape
    tm = min(tm, t)
    vec = pl.BlockSpec((None, 1, d), lambda bi, i: (bi, 0, 0))
    const = lambda bi, i: (0, 0)
    head_out = lambda w: pl.BlockSpec((None, MLA_HEADS, tm, w), lambda bi, i: (bi, 0, i, 0))
    return pl.pallas_call(
        functools.partial(_mla_proj_kernel, q_scale=(MLA_NOPE + MLA_ROPE) ** -0.5),
        out_shape=[jax.ShapeDtypeStruct((b, MLA_HEADS, t, MLA_HEAD_PAD), BF16),
                   jax.ShapeDtypeStruct((b, MLA_HEADS, t, MLA_HEAD_PAD), BF16),
                   jax.ShapeDtypeStruct((b, MLA_HEADS, MLA_V, t), BF16)],
        grid=(b, t // tm),
        in_specs=[
            pl.BlockSpec((None, tm, d), lambda bi, i: (bi, i, 0)),
            pl.BlockSpec((1, d), const),
            vec, vec,
            _single(wd.shape, const),
            pl.BlockSpec((1, MLA_RANK), const),
            pl.BlockSpec((1, MLA_RANK), const),
            _single(wuq.shape, const),
            _single(wukv.shape, const),
            pl.BlockSpec((tm, LANES), lambda bi, i: (i, 0)),
            pl.BlockSpec((tm, LANES), lambda bi, i: (i, 0)),
        ],
        out_specs=[head_out(MLA_HEAD_PAD), head_out(MLA_HEAD_PAD),
                   pl.BlockSpec((None, MLA_HEADS, MLA_V, tm), lambda bi, i: (bi, 0, 0, i))],
        compiler_params=_cp("arbitrary", "arbitrary"),
        name="mla_proj",
    )(x, gain.reshape(1, d), shift, scale, wd, gq.reshape(1, -1), gkv.reshape(1, -1), wuq, wukv, cos, sin)


ATTN_CHUNK = 256
ATTN_SUB = 256


def _attn_kernel(q_ref, *refs):
    o_ref, s_scr = refs[-2:]
    nseg = (len(refs) - 2) // 2
    k_refs, vt_refs = refs[:nseg], refs[nseg:2 * nseg]
    tq = q_ref.shape[0]
    sub = s_scr.shape[2]
    for u in range(tq // sub):
        q = q_ref[u * sub:(u + 1) * sub, :]
        base = 0
        for k_ref in k_refs:
            for c in range(k_ref.shape[0] // ATTN_CHUNK):
                rows = slice(c * ATTN_CHUNK, (c + 1) * ATTN_CHUNK)
                s_scr[u, base + rows.start:base + rows.stop, :] = _dot_nt(k_ref[rows, :], q)
            base += k_ref.shape[0]
    for u in range(tq // sub):
        s = s_scr[u]
        m = jnp.max(s, axis=0, keepdims=True)
        p = jnp.exp(s - m)
        l = jnp.sum(p, axis=0, keepdims=True)
        pb = p.astype(BF16)
        ot = None
        base = 0
        for vt_ref in vt_refs:
            part = _dot(vt_ref[...], pb[base:base + vt_ref.shape[1], :])
            ot = part if ot is None else ot + part
            base += vt_ref.shape[1]
        o_ref[u * sub:(u + 1) * sub, :] = (ot / l).T.astype(o_ref.dtype)


def _attention(q, ks, vts, tq=512):
    b, h, t, dk = q.shape
    dv = vts[0].shape[2]
    tk = sum(k.shape[2] for k in ks)
    tq = min(tq, t)
    sub = min(ATTN_SUB, tq)
    whole = lambda a: pl.BlockSpec((None, None) + a.shape[2:], lambda bi, hi, i: (bi, hi, 0, 0))
    return pl.pallas_call(
        _attn_kernel,
        out_shape=jax.ShapeDtypeStruct((b, t, h * dv), BF16),
        grid=(b, h, t // tq),
        in_specs=[pl.BlockSpec((None, None, tq, dk), lambda bi, hi, i: (bi, hi, i, 0))]
                 + [whole(k) for k in ks] + [whole(v) for v in vts],
        out_specs=pl.BlockSpec((None, tq, dv), lambda bi, hi, i: (bi, i, hi)),
        scratch_shapes=[pltpu.VMEM((tq // sub, tk, sub), F32)],
        compiler_params=_cp("arbitrary", "arbitrary", "arbitrary"),
        name="attention",
    )(q, *ks, *vts)


def _sink_attn_kernel(sink_ref, q_ref, k_ref, v_ref, o_ref):
    group = q_ref.shape[0]
    k = k_ref[...]
    v = v_ref[...]
    for g in range(group):
        s = _dot_nt(q_ref[g], k)
        sk = sink_ref[pl.program_id(1) * group + g]
        m = jnp.maximum(jnp.max(s, axis=1, keepdims=True), sk)
        p = jnp.exp(s - m)
        l = jnp.sum(p, axis=1, keepdims=True) + jnp.exp(sk - m)
        o_ref[g] = (_dot(p.astype(BF16), v) / l).astype(o_ref.dtype)


def _sink_attention(q, k, v, sink):
    b, hq, t, d = q.shape
    hk = k.shape[1]
    group = hq // hk
    kv_spec = pl.BlockSpec((None, None, t, d), lambda bi, h: (bi, h, 0, 0))
    q_spec = pl.BlockSpec((None, group, t, d), lambda bi, h: (bi, h, 0, 0))
    return pl.pallas_call(
        _sink_attn_kernel,
        out_shape=jax.ShapeDtypeStruct((b, hq, t, d), BF16),
        grid=(b, hk),
        in_specs=[pl.BlockSpec(memory_space=pltpu.SMEM), q_spec, kv_spec, kv_spec],
        out_specs=q_spec,
        compiler_params=_cp("arbitrary", "arbitrary"),
        name="sink_attention",
    )(sink, q, k, v)


def _window_kernel(sink_ref, q_ref, kp_ref, kc_ref, kn_ref, vp_ref, vc_ref, vn_ref, kx_ref, vx_ref, o_ref, *,
                   seq_len):
    i = pl.program_id(2)
    hk = pl.program_id(1)
    n_ctx = kx_ref.shape[0]
    rows = WIN_GROUP * Q_BLOCK
    q = q_ref[...].reshape(rows, WIN_HEAD_DIM)
    kk = jnp.concatenate([kx_ref[...], kp_ref[...], kc_ref[...], kn_ref[...]], axis=0)
    vv = jnp.concatenate([vx_ref[...], vp_ref[...], vc_ref[...], vn_ref[...]], axis=0)
    s = _dot_nt(q, kk)
    r = lax.broadcasted_iota(jnp.int32, s.shape, 0)
    c = lax.broadcasted_iota(jnp.int32, s.shape, 1)
    rel = c - n_ctx - WINDOW - r % Q_BLOCK
    k_pos = i * Q_BLOCK - WINDOW + (c - n_ctx)
    ok = (c < n_ctx) | ((jnp.abs(rel) <= WINDOW) & (k_pos >= 0) & (k_pos < seq_len))
    s = jnp.where(ok, s, NEG_BIG)
    head = lax.broadcasted_iota(jnp.int32, (rows, 1), 0) // Q_BLOCK
    sk = jnp.zeros((rows, 1), F32)
    for g in range(WIN_GROUP):
        sk = jnp.where(head == g, sink_ref[hk * WIN_GROUP + g], sk)
    m = jnp.maximum(jnp.max(s, axis=1, keepdims=True), sk)
    e = jnp.exp(s - m)
    l = jnp.sum(e, axis=1, keepdims=True) + jnp.exp(sk - m)
    o = _dot(e.astype(BF16), vv) / l
    o_ref[...] = o.reshape(WIN_GROUP, Q_BLOCK, WIN_HEAD_DIM).astype(o_ref.dtype)


def _window_attention(q, k, v, kx, vx, sink):
    b, hq, s, hd = q.shape
    n_ctx = kx.shape[2]
    nblk = s // Q_BLOCK
    blk = lambda f: pl.BlockSpec((None, None, Q_BLOCK, hd), f)
    prev = lambda bi, h, i: (bi, h, jnp.maximum(i - 1, 0), 0)
    cur = lambda bi, h, i: (bi, h, i, 0)
    nxt = lambda bi, h, i: (bi, h, jnp.minimum(i + 1, nblk - 1), 0)
    ctx_spec = pl.BlockSpec((None, None, n_ctx, hd), lambda bi, h, i: (bi, h, 0, 0))
    return pl.pallas_call(
        functools.partial(_window_kernel, seq_len=s),
        out_shape=jax.ShapeDtypeStruct((b, hq, s, hd), BF16),
        grid=(b, WIN_KV_HEADS, nblk),
        in_specs=[pl.BlockSpec(memory_space=pltpu.SMEM),
                  pl.BlockSpec((None, WIN_GROUP, Q_BLOCK, hd), lambda bi, h, i: (bi, h, i, 0)),
                  blk(prev), blk(cur), blk(nxt), blk(prev), blk(cur), blk(nxt), ctx_spec, ctx_spec],
        out_specs=pl.BlockSpec((None, WIN_GROUP, Q_BLOCK, hd), lambda bi, h, i: (bi, h, i, 0)),
        compiler_params=_cp("arbitrary", "arbitrary", "arbitrary"),
        name="window_attention",
    )(sink, q, k, k, k, v, v, v, kx, vx)


def _shortconv_kernel(u_ref, w_ref, b_ref, o_ref):
    u = u_ref[...]
    rows = u.shape[0]
    t = lax.broadcasted_iota(jnp.int32, u.shape, 0)
    before = jnp.where(t == 0, 0.0, pltpu.roll(u, 1, 0))
    after = jnp.where(t == rows - 1, 0.0, pltpu.roll(u, rows - 1, 0))
    o_ref[...] = before * w_ref[0:1, :] + u * w_ref[1:2, :] + after * w_ref[2:3, :] + b_ref[...]


def _shortconv(u, w, bias, tc=256):
    b, t, n = u.shape
    return pl.pallas_call(
        _shortconv_kernel,
        out_shape=jax.ShapeDtypeStruct((b, t, n), F32),
        grid=(b, n // tc),
        in_specs=[pl.BlockSpec((None, t, tc), lambda bi, j: (bi, 0, j)),
                  pl.BlockSpec((3, tc), lambda bi, j: (0, j)),
                  pl.BlockSpec((1, tc), lambda bi, j: (0, j))],
        out_specs=pl.BlockSpec((None, t, tc), lambda bi, j: (bi, 0, j)),
        compiler_params=_cp("arbitrary", "arbitrary"),
        name="short_conv",
    )(u, w, bias.reshape(1, n))


def _filter_kernel(z_ref, t_ref, w1_ref, b1_ref, w2_ref, b2_ref, w3_ref, b3_ref, fr_ref, w4_ref, dec_ref, o_ref, *,
                   zero_row):
    def dense(hv, w_ref):
        a_hi, a_lo = _split(hv)
        w_hi, w_lo = _split(w_ref[...])
        return _dot3(a_hi, a_lo, w_hi, w_lo)

    freq = fr_ref[...]
    hv = jnp.sin(freq * (dense(z_ref[...], w1_ref) + b1_ref[...]))
    hv = jnp.sin(freq * (dense(hv, w2_ref) + b2_ref[...]))
    hv = jnp.sin(freq * (dense(hv, w3_ref) + b3_ref[...]))
    out = dense(hv, w4_ref) * jnp.exp(-t_ref[...] * jnp.abs(dec_ref[...]))
    tr = out.shape[0]
    row = pl.program_id(1) * tr + lax.broadcasted_iota(jnp.int32, out.shape, 0)
    o_ref[...] = jnp.where(row == zero_row, 0.0, out)


def _hyena_filter(seq, f_w1, f_b1, f_w2, f_b2, f_w3, f_b3, f_freq, f_w4, decay):
    d = f_w4.shape[1] // (2 * HY_ORDER)
    n = 2 * seq
    t = np.linspace(0.0, 1.0, seq, dtype=np.float32)[:, None]
    w = (2.0 * math.pi * np.arange(seq, dtype=np.float32)[:, None] / seq).astype(np.float32)
    f = np.linspace(1e-4, HY_BANDS - 1, HY_BANDS, dtype=np.float32)[None, :]
    z = np.concatenate([t, np.cos(f * w), -np.sin(f * w)], axis=-1).astype(np.float32)
    lag = np.arange(n)
    lag = np.where(lag < seq, lag, n - lag) % seq
    zc = np.zeros((n, LANES), np.float32)
    zc[:, :HY_EMB] = z[lag]
    tc = t[lag]
    w1 = jnp.zeros((LANES, HY_FILTER_W), F32).at[:HY_EMB].set(f_w1)
    tr = min(512, seq)
    fw = HY_FILTER_W
    row = lambda v: v.reshape(1, -1)
    const = lambda o, i: (0, 0)
    col = lambda o, i: (0, 2 * o + (i * tr) // seq)
    return pl.pallas_call(
        functools.partial(_filter_kernel, zero_row=seq),
        out_shape=jax.ShapeDtypeStruct((HY_ORDER, n, d), F32),
        grid=(HY_ORDER, n // tr),
        in_specs=[pl.BlockSpec((tr, LANES), lambda o, i: (i, 0)),
                  pl.BlockSpec((tr, 1), lambda o, i: (i, 0)),
                  pl.BlockSpec((LANES, fw), const), pl.BlockSpec((1, fw), const),
                  pl.BlockSpec((fw, fw), const), pl.BlockSpec((1, fw), const),
                  pl.BlockSpec((fw, fw), const), pl.BlockSpec((1, fw), const),
                  pl.BlockSpec((1, fw), const),
                  pl.BlockSpec((fw, d), col),
                  pl.BlockSpec((1, d), col)],
        out_specs=pl.BlockSpec((None, tr, d), lambda o, i: (o, i, 0)),
        compiler_params=_cp("arbitrary", "arbitrary"),
        name="hyena_filter",
    )(jnp.asarray(zc), jnp.asarray(tc), w1, row(f_b1), f_w2, row(f_b2), f_w3, row(f_b3), row(f_freq), f_w4,
      row(decay))


def _stack_complex(c):
    return np.block([[c.real, -c.imag], [c.imag, c.real]])


def _hi_lo(m):
    m = jnp.asarray(np.asarray(m, np.float32))
    hi = m.astype(BF16)
    lo = (m - hi.astype(F32)).astype(BF16)
    return hi, lo


def _fft_split(n):
    n2 = 128
    return n // n2, n2


@functools.lru_cache(maxsize=None)
def _fft_matrices_np(n):
    n1, n2 = _fft_split(n)
    h = n1 // 2
    k1 = np.arange(n1)
    bb = np.arange(n2)
    tw = np.exp(-2j * np.pi * np.outer(bb, k1) / n)
    f1 = np.exp(-2j * np.pi * np.outer(k1, np.arange(n1)) / n1)
    la = np.stack([_stack_complex(tw[b][:, None] * f1[:, :h]) for b in range(n2)])
    lf = np.stack([np.concatenate([(tw[b][:, None] * f1).real, (tw[b][:, None] * f1).imag], axis=0)
                   for b in range(n2)])
    f2 = np.exp(-2j * np.pi * np.outer(np.arange(n2), np.arange(n2)) / n2)
    lc = _stack_complex(f2)
    lci = _stack_complex(np.conj(f2))
    f1i = np.exp(2j * np.pi * np.outer(np.arange(h), k1) / n1)
    lai = np.stack([_stack_complex(f1i * np.conj(tw[b])[None, :] / n) for b in range(n2)])
    return la, lf, lc, lci, lai


FFT_UNROLL = 8


def _fwd_strided_stage(read_rows, mat_hi_ref, mat_lo_ref, t_ref, n1, n2):
    def body(b, carry):
        hi, lo = _split(read_rows(b))
        out = _dot3(mat_hi_ref[b], mat_lo_ref[b], hi, lo)
        off = pl.multiple_of(b * n1, n1)
        t_ref[0, pl.ds(off, n1), :] = out[:n1]
        t_ref[1, pl.ds(off, n1), :] = out[n1:]
        return carry
    lax.fori_loop(0, n2, body, 0, unroll=FFT_UNROLL)


def _gather_stacked(ref, start, count, stride):
    return jnp.concatenate([ref[0, pl.ds(start, count, stride=stride), :],
                            ref[1, pl.ds(start, count, stride=stride), :]], axis=0)


def _gather_pair(ref, k1, count, stride):
    return jnp.concatenate([_gather_stacked(ref, k1, count, stride),
                            _gather_stacked(ref, k1 + 1, count, stride)], axis=1)


def _load_pair(ref, plane, off, rows):
    blk = ref[plane, pl.ds(off, 2 * rows), :]
    return jnp.concatenate([blk[:rows], blk[rows:]], axis=1)


def _store_pair(ref, plane, off, rows, val):
    td = val.shape[1] // 2
    ref[plane, pl.ds(off, rows), :] = val[:, :td]
    ref[plane, pl.ds(off + rows, rows), :] = val[:, td:]


def _spectrum_kernel(f_ref, lf_hi, lf_lo, lc_hi, lc_lo, h_ref, t_scr, *, n1, n2):
    _fwd_strided_stage(lambda b: f_ref[pl.ds(b, n1, stride=n2), :], lf_hi, lf_lo, t_scr, n1, n2)

    def body(i, carry):
        hi, lo = _split(_gather_pair(t_scr, 2 * i, n2, n1))
        out = _dot3(lc_hi[...], lc_lo[...], hi, lo)
        off = pl.multiple_of(2 * i * n2, 2 * n2)
        _store_pair(h_ref, 0, off, n2, out[:n2])
        _store_pair(h_ref, 1, off, n2, out[n2:])
        return carry
    lax.fori_loop(0, n1 // 2, body, 0, unroll=2)


def _spectrum(filt, td=LANES):
    order, n, d = filt.shape
    n1, n2 = _fft_split(n)
    _, lf, lc, _, _ = _fft_matrices_np(n)
    lf_hi, lf_lo = _hi_lo(lf)
    lc_hi, lc_lo = _hi_lo(lc)
    c3 = lambda o, j: (0, 0, 0)
    c2 = lambda o, j: (0, 0)
    return pl.pallas_call(
        functools.partial(_spectrum_kernel, n1=n1, n2=n2),
        out_shape=jax.ShapeDtypeStruct((order, 2, n, d), F32),
        grid=(order, d // td),
        in_specs=[pl.BlockSpec((None, n, td), lambda o, j: (o, 0, j)),
                  _single(lf_hi.shape, c3), _single(lf_lo.shape, c3),
                  _single(lc_hi.shape, c2), _single(lc_lo.shape, c2)],
        out_specs=pl.BlockSpec((None, 2, n, td), lambda o, j: (o, 0, 0, j)),
        scratch_shapes=[pltpu.VMEM((2, n, td), F32)],
        compiler_params=_cp("arbitrary", "arbitrary"),
        name="filter_spectrum",
    )(filt, lf_hi, lf_lo, lc_hi, lc_lo)


def _fftconv_kernel(z_ref, g_ref, h_ref, skip_ref, la_hi, la_lo, lc_hi, lc_lo, lci_hi, lci_lo, lai_hi, lai_lo,
                    o_ref, t_scr, u_scr, *, n1, n2):
    half = n1 // 2
    seq = half * n2
    _fwd_strided_stage(lambda b: _gather_stacked(z_ref, b, half, n2), la_hi, la_lo, t_scr, n1, n2)

    def mid(i, carry):
        hi, lo = _split(_gather_pair(t_scr, 2 * i, n2, n1))
        x = _dot3(lc_hi[...], lc_lo[...], hi, lo)
        off = pl.multiple_of(2 * i * n2, 2 * n2)
        xr, xi = x[:n2], x[n2:]
        hr = _load_pair(h_ref, 0, off, n2)
        hi_ = _load_pair(h_ref, 1, off, n2)
        y = jnp.concatenate([xr * hr - xi * hi_, xr * hi_ + xi * hr], axis=0)
        y_hi, y_lo = _split(y)
        w = _dot3(lci_hi[...], lci_lo[...], y_hi, y_lo)
        _store_pair(u_scr, 0, off, n2, w[:n2])
        _store_pair(u_scr, 1, off, n2, w[n2:])
        return carry
    lax.fori_loop(0, n1 // 2, mid, 0, unroll=2)

    def inv(b, carry):
        hi, lo = _split(_gather_stacked(u_scr, b, n1, n2))
        out = _dot3(lai_hi[b], lai_lo[b], hi, lo)
        off = pl.multiple_of(b * half, half)
        t_scr[0, pl.ds(off, half), :] = out[:half]
        t_scr[1, pl.ds(off, half), :] = out[half:]
        return carry
    lax.fori_loop(0, n2, inv, 0, unroll=FFT_UNROLL)

    skip = skip_ref[...]

    def epi(a, carry):
        off = pl.multiple_of(a * n2, n2)
        for plane in range(2):
            y = t_scr[plane, pl.ds(a, n2, stride=half), :]
            zc = z_ref[plane, pl.ds(off, n2), :]
            gc = g_ref[plane, pl.ds(off, n2), :]
            o_ref[plane, pl.ds(off, n2), :] = (gc * (y + zc * skip)).astype(o_ref.dtype)
        return carry
    lax.fori_loop(0, half, epi, 0)
    del seq


def _fftconv(z_arr, z_col, g_arr, g_col, spec, order, skip, out_dtype, td=LANES):
    b, seq, _ = z_arr.shape
    d = spec.shape[3]
    n = 2 * seq
    n1, n2 = _fft_split(n)
    nblk = d // td
    la, _, lc, lci, lai = _fft_matrices_np(n)
    mats = [*_hi_lo(la), *_hi_lo(lc), *_hi_lo(lci), *_hi_lo(lai)]
    mat_specs = [_single(m.shape, (lambda j, p, nd=m.ndim: (0,) * nd)) for m in mats]
    return pl.pallas_call(
        functools.partial(_fftconv_kernel, n1=n1, n2=n2),
        out_shape=jax.ShapeDtypeStruct((b, seq, d), out_dtype),
        grid=(nblk, b // 2),
        in_specs=[_single((2, seq, td), lambda j, p: (p, 0, z_col * nblk + j)),
                  _single((2, seq, td), lambda j, p: (p, 0, g_col * nblk + j)),
                  _single((None, 2, n, td), lambda j, p: (order, 0, 0, j)),
                  pl.BlockSpec((None, 1, td), lambda j, p: (order, 0, j))] + mat_specs,
        out_specs=_single((2, seq, td), lambda j, p: (p, 0, j)),
        scratch_shapes=[pltpu.VMEM((2, n, td), F32), pltpu.VMEM((2, n, td), F32)],
        compiler_params=_cp("arbitrary", "arbitrary"),
        name="fft_conv",
    )(z_arr, g_arr, spec, skip.reshape(HY_ORDER, 1, d), *mats)


@functools.lru_cache(maxsize=None)
def _dense_dft_np(seq):
    n = 2 * seq
    k = np.arange(n)
    f = np.exp(-2j * np.pi * np.outer(k, k) / n)
    fwd = _stack_complex(f[:, :seq])
    flt = np.concatenate([f.real, f.imag], axis=0)
    inv = _stack_complex(np.conj(f)[:seq, :] / n)
    return fwd, flt, inv


def _dense_spectrum_kernel(f_ref, m_hi, m_lo, h_ref):
    hi, lo = _split(f_ref[...])
    out = _dot3(m_hi[...], m_lo[...], hi, lo)
    n = f_ref.shape[0]
    h_ref[0] = out[:n]
    h_ref[1] = out[n:]


def _dense_spectrum(filt, td=256):
    order, n, d = filt.shape
    _, flt, _ = _dense_dft_np(n // 2)
    m_hi, m_lo = _hi_lo(flt)
    c2 = lambda o, j: (0, 0)
    return pl.pallas_call(
        _dense_spectrum_kernel,
        out_shape=jax.ShapeDtypeStruct((order, 2, n, d), F32),
        grid=(order, d // td),
        in_specs=[pl.BlockSpec((None, n, td), lambda o, j: (o, 0, j)),
                  pl.BlockSpec(m_hi.shape, c2), pl.BlockSpec(m_lo.shape, c2)],
        out_specs=pl.BlockSpec((None, 2, n, td), lambda o, j: (o, 0, 0, j)),
        compiler_params=_cp("arbitrary", "arbitrary"),
        name="filter_spectrum_dense",
    )(filt, m_hi, m_lo)


def _dense_conv_kernel(z_ref, g_ref, h_ref, skip_ref, f_hi, f_lo, i_hi, i_lo, o_ref):
    seq = z_ref.shape[1]
    n = 2 * seq
    zz = jnp.concatenate([z_ref[0], z_ref[1]], axis=0)
    hi, lo = _split(zz)
    x = _dot3(f_hi[...], f_lo[...], hi, lo)
    xr, xi = x[:n], x[n:]
    hr, hi_ = h_ref[0], h_ref[1]
    y = jnp.concatenate([xr * hr - xi * hi_, xr * hi_ + xi * hr], axis=0)
    y_hi, y_lo = _split(y)
    w = _dot3(i_hi[...], i_lo[...], y_hi, y_lo)
    skip = skip_ref[...]
    for plane in range(2):
        y_p = w[plane * seq:(plane + 1) * seq]
        o_ref[plane] = (g_ref[plane] * (y_p + z_ref[plane] * skip)).astype(o_ref.dtype)


def _dense_conv(z_arr, z_col, g_arr, g_col, spec, order, skip, out_dtype, td=256):
    b, seq, _ = z_arr.shape
    d = spec.shape[3]
    n = 2 * seq
    nblk = d // td
    fwd, _, inv = _dense_dft_np(seq)
    mats = [*_hi_lo(fwd), *_hi_lo(inv)]
    c2 = lambda j, p: (0, 0)
    return pl.pallas_call(
        _dense_conv_kernel,
        out_shape=jax.ShapeDtypeStruct((b, seq, d), out_dtype),
        grid=(nblk, b // 2),
        in_specs=[pl.BlockSpec((2, seq, td), lambda j, p: (p, 0, z_col * nblk + j)),
                  pl.BlockSpec((2, seq, td), lambda j, p: (p, 0, g_col * nblk + j)),
                  pl.BlockSpec((None, 2, n, td), lambda j, p: (order, 0, 0, j)),
                  pl.BlockSpec((None, 1, td), lambda j, p: (order, 0, j))]
                 + [pl.BlockSpec(m.shape, c2) for m in mats],
        out_specs=pl.BlockSpec((2, seq, td), lambda j, p: (p, 0, j)),
        compiler_params=_cp("arbitrary", "arbitrary"),
        name="dense_conv",
    )(z_arr, g_arr, spec, skip.reshape(HY_ORDER, 1, d), *mats)


DENSE_DFT_MAX_SEQ = 512


def _hyena(x, gain, shift, scale, p):
    b, seq, d = x.shape
    u = _nm_matmul(x, gain, shift, scale, p["w_in"], F32, tn=1024)
    u = _shortconv(u, p["conv_w"], p["conv_b"])
    filt = _hyena_filter(seq, *p["filter"])
    dense = seq <= DENSE_DFT_MAX_SEQ
    spec = _dense_spectrum(filt) if dense else _spectrum(filt)
    conv = _dense_conv if dense else _fftconv
    z1 = conv(u, 0, u, 1, spec, 0, p["skip"], F32)
    return conv(z1, 0, u, 2, spec, 1, p["skip"], BF16)


def _rope_tables(seq):
    t = np.arange(seq)
    pos = np.stack([t // GRID_W, t % GRID_W], axis=1).astype(np.float32)
    n = 16
    inv = jnp.asarray(ROPE_THETA, F32) ** (-jnp.arange(n, dtype=F32) / n)
    lane = np.arange(LANES)
    axis = (lane % 64) // 32
    idx = lane % 16
    sign = np.where(lane % 32 < 16, -1.0, 1.0).astype(np.float32)
    ang = jnp.asarray(pos)[:, axis] * inv[idx][None, :]
    return jnp.cos(ang), jnp.sin(ang) * sign[None, :]


def _identity_tables(seq):
    return jnp.ones((seq, LANES), F32), jnp.zeros((seq, LANES), F32)


def _mla_weights(w_dq, w_uq, w_dkv, w_ukv):
    d = w_dq.shape[0]
    hh = MLA_HEADS
    wd = jnp.concatenate([w_dq, w_dkv, jnp.zeros((d, LANES - MLA_ROPE), F32)], axis=1).astype(BF16)
    uq = w_uq.reshape(MLA_RANK, hh, MLA_NOPE + MLA_ROPE)
    uq = jnp.pad(uq, ((0, 0), (0, 0), (0, MLA_HEAD_PAD - MLA_NOPE - MLA_ROPE)))
    ukv = w_ukv.reshape(MLA_RANK, hh, MLA_NOPE + MLA_V)
    ukv = jnp.concatenate([ukv[:, :, :MLA_NOPE].reshape(MLA_RANK, hh * MLA_NOPE),
                           ukv[:, :, MLA_NOPE:].reshape(MLA_RANK, hh * MLA_V)], axis=1)
    return wd, uq.reshape(MLA_RANK, hh * MLA_HEAD_PAD).astype(BF16), ukv.astype(BF16)


def _ffn_weights(wg, wu, wd):
    f = wg.shape[1]
    pad = (-f) % FFN_TILE
    return (jnp.pad(wg, ((0, 0), (0, pad))).astype(BF16), jnp.pad(wu, ((0, 0), (0, pad))).astype(BF16),
            jnp.pad(wd, ((0, pad), (0, 0))).astype(BF16))


def _to_heads(a, heads):
    b, t, _ = a.shape
    return a.reshape(b, t, heads, -1).transpose(0, 2, 1, 3)


def _from_heads(a):
    b, h, t, e = a.shape
    return a.transpose(0, 2, 1, 3).reshape(b, t, h * e)


def kernel(x, c, ctx, c_ctx, mod_w, mod_b, norm_g, final_g, ffn_wg, ffn_wu, ffn_wd, mla_w_dq, mla_g_q, mla_w_uq, mla_w_dkv, mla_g_kv, mla_w_ukv, mla_w_o, hy_w_in, hy_conv_w, hy_conv_b, hy_f_w1, hy_f_b1, hy_f_w2, hy_f_b2, hy_f_w3, hy_f_b3, hy_f_freq, hy_f_w4, hy_decay, hy_skip, hy_w_out, win_w_qkv, win_sink, win_w_o):
    b, s, d = x.shape
    n_ctx = ctx.shape[1]
    depth = mod_w.shape[0]
    assert b % 2 == 0 and s % 512 == 0 and n_ctx % 256 == 0

    cc = jnp.concatenate([c, c_ctx[None, :], jnp.zeros((8 - (b + 1) % 8, d), F32)], axis=0)
    mods = _modulation(cc, mod_w, mod_b).reshape(depth, cc.shape[0], N_MOD, d)

    cos_l, sin_l = _rope_tables(s)
    cos_c, sin_c = _identity_tables(n_ctx)

    for i in range(depth):
        kind, j = i % N_MIXERS, i // N_MIXERS
        need_ctx = i < depth - 1
        ctx_live = need_ctx or kind != 1
        ml = [mods[i, :b, k][:, None, :] for k in range(N_MOD)]
        mc = [jnp.broadcast_to(mods[i, b, k][None, None, :], (b, 1, d)) for k in range(N_MOD)]
        w1 = _ffn_weights(ffn_wg[i, 0], ffn_wu[i, 0], ffn_wd[i, 0])
        w2 = _ffn_weights(ffn_wg[i, 1], ffn_wu[i, 1], ffn_wd[i, 1])

        x = _ffn(x, norm_g[i, 0], ml[0], ml[1], ml[2], *w1)
        if ctx_live:
            ctx = _ffn(ctx, norm_g[i, 0], mc[0], mc[1], mc[2], *w1)

        g1 = norm_g[i, 1]
        o_c = None
        if kind == 0:
            wd, wuq, wukv = _mla_weights(mla_w_dq[j], mla_w_uq[j], mla_w_dkv[j], mla_w_ukv[j])
            q_l, k_l, v_l = _mla_proj(x, g1, ml[3], ml[4], wd, mla_g_q[j], mla_g_kv[j], wuq, wukv, cos_l, sin_l)
            q_c, k_c, v_c = _mla_proj(ctx, g1, mc[3], mc[4], wd, mla_g_q[j], mla_g_kv[j], wuq, wukv, cos_c, sin_c)
            w_o = mla_w_o[j].astype(BF16)
            o_l = _attention(q_l, [k_c, k_l], [v_c, v_l])
            if need_ctx:
                o_c = _attention(q_c, [k_c], [v_c])
        elif kind == 1:
            hp = dict(w_in=hy_w_in[j].astype(BF16), conv_w=hy_conv_w[j], conv_b=hy_conv_b[j],
                      filter=(hy_f_w1[j], hy_f_b1[j], hy_f_w2[j], hy_f_b2[j], hy_f_w3[j], hy_f_b3[j],
                              hy_f_freq[j], hy_f_w4[j], hy_decay[j]),
                      skip=hy_skip[j])
            w_o = hy_w_out[j].astype(BF16)
            o_l = _hyena(x, g1, ml[3], ml[4], hp)
            if need_ctx:
                o_c = _hyena(ctx, g1, mc[3], mc[4], hp)
        else:
            qw = WIN_Q_HEADS * WIN_HEAD_DIM
            kw = WIN_KV_HEADS * WIN_HEAD_DIM
            w_qkv = win_w_qkv[j].astype(BF16)
            w_o = win_w_o[j].astype(BF16)
            q_chunks = qw // LANES
            rope_l = (cos_l, sin_l, (qw + kw) // LANES, q_chunks, WIN_HEAD_DIM ** -0.5)
            rope_c = (cos_c, sin_c, (qw + kw) // LANES, q_chunks, WIN_HEAD_DIM ** -0.5)
            qkv_l = _nm_matmul(x, g1, ml[3], ml[4], w_qkv, BF16, tn=qw + 2 * kw, rope=rope_l)
            qkv_c = _nm_matmul(ctx, g1, mc[3], mc[4], w_qkv, BF16, tn=qw + 2 * kw, rope=rope_c)
            q_l = _to_heads(qkv_l[..., :qw], WIN_Q_HEADS)
            k_l = _to_heads(qkv_l[..., qw:qw + kw], WIN_KV_HEADS)
            v_l = _to_heads(qkv_l[..., qw + kw:], WIN_KV_HEADS)
            k_c = _to_heads(qkv_c[..., qw:qw + kw], WIN_KV_HEADS)
            v_c = _to_heads(qkv_c[..., qw + kw:], WIN_KV_HEADS)
            o_l = _from_heads(_window_attention(q_l, k_l, v_l, k_c, v_c, win_sink[j]))
            if need_ctx:
                q_c = _to_heads(qkv_c[..., :qw], WIN_Q_HEADS)
                o_c = _from_heads(_sink_attention(q_c, k_c, v_c, win_sink[j]))

        x = _mm_res(o_l, w_o, x, ml[5])
        x = _ffn(x, norm_g[i, 2], ml[6], ml[7], ml[8], *w2)
        if need_ctx:
            ctx = _mm_res(o_c, w_o, ctx, mc[5])
            ctx = _ffn(ctx, norm_g[i, 2], mc[6], mc[7], mc[8], *w2)
    return _final_norm(x, final_g)
```

```python
import functools
import math

import numpy as np
import jax
import jax.numpy as jnp
from jax import lax
from jax.experimental import pallas as pl
from jax.experimental.pallas import tpu as pltpu

F32 = jnp.float32
BF16 = jnp.bfloat16

GRID_W = 64
N_MOD = 9
FFN_RES = 0.5
NORM_EPS = 1e-6
ROPE_THETA = 10000.0
N_MIXERS = 3
MLA_HEADS = 16
MLA_NOPE = 128
MLA_ROPE = 64
MLA_V = 128
MLA_RANK = 512
MLA_HEAD_PAD = 256
HY_ORDER = 2
HY_EMB = 33
HY_BANDS = (HY_EMB - 1) // 2
HY_FILTER_W = 64
WIN_HEAD_DIM = 64
WIN_Q_HEADS = 32
WIN_KV_HEADS = 4
WIN_GROUP = WIN_Q_HEADS // WIN_KV_HEADS
WINDOW = 128
Q_BLOCK = 128

LANES = 128
V7X_VMEM_LIMIT = 56 * 1024 * 1024
FFN_TILE = 512
NEG_BIG = -1e30


def _cp(*sem):
    return pltpu.CompilerParams(dimension_semantics=sem, vmem_limit_bytes=V7X_VMEM_LIMIT)


def _single(shape, imap):
    return pl.BlockSpec(shape, imap, pipeline_mode=pl.Buffered(1))


def _dot(a, b):
    return jnp.dot(a, b, preferred_element_type=F32)


def _dot_nt(a, b):
    return lax.dot_general(a, b, (((1,), (1,)), ((), ())), preferred_element_type=F32)


def _split(x):
    hi = x.astype(BF16)
    lo = (x - hi.astype(F32)).astype(BF16)
    return hi, lo


def _dot3(a_hi, a_lo, b_hi, b_lo):
    return _dot(a_hi, b_hi) + (_dot(a_hi, b_lo) + _dot(a_lo, b_hi))


def _modnorm(x, gain, shift, scale):
    r = lax.rsqrt(jnp.mean(x * x, axis=-1, keepdims=True) + NORM_EPS)
    return (x * r * gain) * (1.0 + scale) + shift


def _rms(x, gain):
    return x * lax.rsqrt(jnp.mean(x * x, axis=-1, keepdims=True) + NORM_EPS) * gain


def _rope128(x, cos, sin):
    lane = lax.broadcasted_iota(jnp.int32, x.shape, 1)
    partner = jnp.where(lane % 32 < 16, pltpu.roll(x, LANES - 16, 1), pltpu.roll(x, 16, 1))
    return x * cos + partner * sin


def _mod_kernel(a_ref, w_ref, b_ref, o_ref):
    a = a_ref[...]
    a = a * jax.nn.sigmoid(a)
    a_hi, a_lo = _split(a)
    w_hi, w_lo = _split(w_ref[...])
    o_ref[...] = _dot3(a_hi, a_lo, w_hi, w_lo) + b_ref[...]


def _modulation(cc, mod_w, mod_b):
    depth, d, n = mod_w.shape
    rows = cc.shape[0]
    tn = n // 16
    return pl.pallas_call(
        _mod_kernel,
        out_shape=jax.ShapeDtypeStruct((depth, rows, n), F32),
        grid=(depth, n // tn),
        in_specs=[
            pl.BlockSpec((rows, d), lambda l, j: (0, 0)),
            pl.BlockSpec((None, d, tn), lambda l, j: (l, 0, j)),
            pl.BlockSpec((None, 1, tn), lambda l, j: (l, 0, j)),
        ],
        out_specs=pl.BlockSpec((None, rows, tn), lambda l, j: (l, 0, j)),
        compiler_params=_cp("arbitrary", "arbitrary"),
        name="modulation",
    )(cc, mod_w, mod_b.reshape(depth, 1, n))


def _ffn_kernel(x_ref, g_ref, sh_ref, sc_ref, gt_ref, wg_ref, wu_ref, wd_ref, o_ref, h_scr, acc_scr):
    j = pl.program_id(2)

    @pl.when(j == 0)
    def _():
        h_scr[...] = _modnorm(x_ref[...], g_ref[...], sh_ref[...], sc_ref[...]).astype(BF16)
        acc_scr[...] = jnp.zeros_like(acc_scr)

    h = h_scr[...]
    g = _dot(h, wg_ref[...])
    u = _dot(h, wu_ref[...])
    a = (g * jax.nn.sigmoid(g) * u).astype(BF16)
    acc_scr[...] += _dot(a, wd_ref[...])

    @pl.when(j == pl.num_programs(2) - 1)
    def _():
        o_ref[...] = x_ref[...] + FFN_RES * gt_ref[...] * acc_scr[...]


def _ffn(x, gain, shift, scale, gate, wg, wu, wd, layer, slot, tm=512):
    b, t, d = x.shape
    f = wg.shape[3]
    tm = min(tm, t)
    vec = pl.BlockSpec((None, 1, d), lambda bi, i, j: (bi, 0, 0))
    return pl.pallas_call(
        _ffn_kernel,
        out_shape=jax.ShapeDtypeStruct((b, t, d), F32),
        grid=(b, t // tm, f // FFN_TILE),
        in_specs=[
            pl.BlockSpec((None, tm, d), lambda bi, i, j: (bi, i, 0)),
            pl.BlockSpec((1, d), lambda bi, i, j: (0, 0)),
            vec, vec, vec,
            pl.BlockSpec((None, None, d, FFN_TILE), lambda bi, i, j: (layer, slot, 0, j)),
            pl.BlockSpec((None, None, d, FFN_TILE), lambda bi, i, j: (layer, slot, 0, j)),
            pl.BlockSpec((None, None, FFN_TILE, d), lambda bi, i, j: (layer, slot, j, 0)),
        ],
        out_specs=pl.BlockSpec((None, tm, d), lambda bi, i, j: (bi, i, 0)),
        scratch_shapes=[pltpu.VMEM((tm, d), BF16), pltpu.VMEM((tm, d), F32)],
        compiler_params=_cp("arbitrary", "arbitrary", "arbitrary"),
        name="ffn",
    )(x, gain.reshape(1, d), shift, scale, gate, wg, wu, wd)


def _nm_matmul_kernel(x_ref, g_ref, sh_ref, sc_ref, w_ref, *rest, rope_chunks, scaled_chunks, out_scale):
    if rope_chunks:
        cos_ref, sin_ref, o_ref, h_scr = rest
    else:
        o_ref, h_scr = rest

    @pl.when(pl.program_id(2) == 0)
    def _():
        h_scr[...] = _modnorm(x_ref[...], g_ref[...], sh_ref[...], sc_ref[...]).astype(BF16)

    y = _dot(h_scr[...], w_ref[...])
    if not rope_chunks:
        o_ref[...] = y.astype(o_ref.dtype)
        return
    cos = cos_ref[...]
    sin = sin_ref[...]
    for c in range(y.shape[1] // LANES):
        yc = y[:, c * LANES:(c + 1) * LANES]
        if c < rope_chunks:
            yc = _rope128(yc, cos, sin)
        if c < scaled_chunks:
            yc = yc * out_scale
        o_ref[:, c * LANES:(c + 1) * LANES] = yc.astype(o_ref.dtype)


def _nm_matmul(x, gain, shift, scale, w, out_dtype, tn, tm=512, rope=None):
    b, t, d = x.shape
    n = w.shape[1]
    tm = min(tm, t)
    vec = pl.BlockSpec((None, 1, d), lambda bi, i, j: (bi, 0, 0))
    in_specs = [
        pl.BlockSpec((None, tm, d), lambda bi, i, j: (bi, i, 0)),
        pl.BlockSpec((1, d), lambda bi, i, j: (0, 0)),
        vec, vec,
        pl.BlockSpec((d, tn), lambda bi, i, j: (0, j)),
    ]
    args = [x, gain.reshape(1, d), shift, scale, w]
    rope_chunks = scaled_chunks = 0
    out_scale = 1.0
    if rope is not None:
        cos, sin, rope_chunks, scaled_chunks, out_scale = rope
        in_specs += [pl.BlockSpec((tm, LANES), lambda bi, i, j: (i, 0))] * 2
        args += [cos, sin]
    return pl.pallas_call(
        functools.partial(_nm_matmul_kernel, rope_chunks=rope_chunks, scaled_chunks=scaled_chunks,
                          out_scale=out_scale),
        out_shape=jax.ShapeDtypeStruct((b, t, n), out_dtype),
        grid=(b, t // tm, n // tn),
        in_specs=in_specs,
        out_specs=pl.BlockSpec((None, tm, tn), lambda bi, i, j: (bi, i, j)),
        scratch_shapes=[pltpu.VMEM((tm, d), BF16)],
        compiler_params=_cp("arbitrary", "arbitrary", "arbitrary"),
        name="norm_matmul",
    )(*args)


def _mm_res_kernel(a_ref, w_ref, r_ref, gt_ref, o_ref):
    o_ref[...] = r_ref[...] + gt_ref[...] * _dot(a_ref[...], w_ref[...])


def _mm_res(a, w, res, gate, tm=512):
    b, t, k = a.shape
    n = w.shape[1]
    tm = min(tm, t)
    return pl.pallas_call(
        _mm_res_kernel,
        out_shape=jax.ShapeDtypeStruct((b, t, n), F32),
        grid=(b, t // tm),
        in_specs=[
            pl.BlockSpec((None, tm, k), lambda bi, i: (bi, i, 0)),
            pl.BlockSpec((k, n), lambda bi, i: (0, 0)),
            pl.BlockSpec((None, tm, n), lambda bi, i: (bi, i, 0)),
            pl.BlockSpec((None, 1, n), lambda bi, i: (bi, 0, 0)),
        ],
        out_specs=pl.BlockSpec((None, tm, n), lambda bi, i: (bi, i, 0)),
        compiler_params=_cp("arbitrary", "arbitrary"),
        name="proj_residual",
    )(a, w, res, gate)


def _final_norm_kernel(x_ref, g_ref, o_ref):
    o_ref[...] = _rms(x_ref[...], g_ref[...])


def _final_norm(x, gain, tm=512):
    b, t, d = x.shape
    return pl.pallas_call(
        _final_norm_kernel,
        out_shape=jax.ShapeDtypeStruct((b, t, d), F32),
        grid=(b, t // tm),
        in_specs=[pl.BlockSpec((None, tm, d), lambda bi, i: (bi, i, 0)),
                  pl.BlockSpec((1, d), lambda bi, i: (0, 0))],
        out_specs=pl.BlockSpec((None, tm, d), lambda bi, i: (bi, i, 0)),
        compiler_params=_cp("arbitrary", "arbitrary"),
        name="final_norm",
    )(x, gain.reshape(1, d))


def _mla_proj_kernel(x_ref, g_ref, sh_ref, sc_ref, wd_ref, gq_ref, gkv_ref, wuq_ref, wukv_ref, cos_ref, sin_ref,
                     q_ref, k_ref, v_ref, *, q_scale):
    h = _modnorm(x_ref[...], g_ref[...], sh_ref[...], sc_ref[...]).astype(BF16)
    a = _dot(h, wd_ref[...])
    cq = _rms(a[:, :MLA_RANK], gq_ref[...]).astype(BF16)
    ckv = _rms(a[:, MLA_RANK:2 * MLA_RANK], gkv_ref[...]).astype(BF16)
    cos = cos_ref[...]
    sin = sin_ref[...]
    kr = _rope128(a[:, 2 * MLA_RANK:], cos, sin).astype(BF16)
    q = _dot(cq, wuq_ref[...])
    kv = _dot(ckv, wukv_ref[...])
    for hd in range(MLA_HEADS):
        base = hd * MLA_HEAD_PAD
        q_ref[hd, :, :LANES] = (q[:, base:base + LANES] * q_scale).astype(BF16)
        q_ref[hd, :, LANES:] = (_rope128(q[:, base + LANES:base + 2 * LANES], cos, sin) * q_scale).astype(BF16)
        k_ref[hd, :, :LANES] = kv[:, hd * MLA_NOPE:(hd + 1) * MLA_NOPE].astype(BF16)
        k_ref[hd, :, LANES:] = kr
        voff = MLA_HEADS * MLA_NOPE + hd * MLA_V
        v_ref[hd] = kv[:, voff:voff + MLA_V].T.astype(BF16)


def _mla_proj(x, gain, shift, scale, wd, gq, gkv, wuq, wukv, cos, sin, tm=256):
    b, t, d = x.shape
    tm = min(tm, t)
    vec = pl.BlockSpec((None, 1, d), lambda bi, i: (bi, 0, 0))
    const = lambda bi, i: (0, 0)
    head_out = lambda w: pl.BlockSpec((None, MLA_HEADS, tm, w), lambda bi, i: (bi, 0, i, 0))
    return pl.pallas_call(
        functools.partial(_mla_proj_kernel, q_scale=(MLA_NOPE + MLA_ROPE) ** -0.5),
        out_shape=[jax.ShapeDtypeStruct((b, MLA_HEADS, t, MLA_HEAD_PAD), BF16),
                   jax.ShapeDtypeStruct((b, MLA_HEADS, t, MLA_HEAD_PAD), BF16),
                   jax.ShapeDtypeStruct((b, MLA_HEADS, MLA_V, t), BF16)],
        grid=(b, t // tm),
        in_specs=[
            pl.BlockSpec((None, tm, d), lambda bi, i: (bi, i, 0)),
            pl.BlockSpec((1, d), const),
            vec, vec,
            _single(wd.shape, const),
            pl.BlockSpec((1, MLA_RANK), const),
            pl.BlockSpec((1, MLA_RANK), const),
            _single(wuq.shape, const),
            _single(wukv.shape, const),
            pl.BlockSpec((tm, LANES), lambda bi, i: (i, 0)),
            pl.BlockSpec((tm, LANES), lambda bi, i: (i, 0)),
        ],
        out_specs=[head_out(MLA_HEAD_PAD), head_out(MLA_HEAD_PAD),
                   pl.BlockSpec((None, MLA_HEADS, MLA_V, tm), lambda bi, i: (bi, 0, 0, i))],
        compiler_params=_cp("arbitrary", "arbitrary"),
        name="mla_proj",
    )(x, gain.reshape(1, d), shift, scale, wd, gq.reshape(1, -1), gkv.reshape(1, -1), wuq, wukv, cos, sin)


ATTN_CHUNK = 256
ATTN_SUB = 256


def _attn_kernel(q_ref, *refs):
    o_ref, s_scr = refs[-2:]
    nseg = (len(refs) - 2) // 2
    k_refs, vt_refs = refs[:nseg], refs[nseg:2 * nseg]
    tq = q_ref.shape[0]
    sub = s_scr.shape[2]
    for u in range(tq // sub):
        q = q_ref[u * sub:(u + 1) * sub, :]
        base = 0
        for k_ref in k_refs:
            for c in range(k_ref.shape[0] // ATTN_CHUNK):
                rows = slice(c * ATTN_CHUNK, (c + 1) * ATTN_CHUNK)
                s_scr[u, base + rows.start:base + rows.stop, :] = _dot_nt(k_ref[rows, :], q)
            base += k_ref.shape[0]
    for u in range(tq // sub):
        s = s_scr[u]
        m = jnp.max(s, axis=0, keepdims=True)
        p = jnp.exp(s - m)
        l = jnp.sum(p, axis=0, keepdims=True)
        pb = p.astype(BF16)
        ot = None
        base = 0
        for vt_ref in vt_refs:
            part = _dot(vt_ref[...], pb[base:base + vt_ref.shape[1], :])
            ot = part if ot is None else ot + part
            base += vt_ref.shape[1]
        o_ref[u * sub:(u + 1) * sub, :] = (ot / l).T.astype(o_ref.dtype)


def _attention(q, ks, vts, tq=1024):
    b, h, t, dk = q.shape
    dv = vts[0].shape[2]
    tk = sum(k.shape[2] for k in ks)
    tq = min(tq, t)
    sub = min(ATTN_SUB, tq)
    whole = lambda a: pl.BlockSpec((None, None) + a.shape[2:], lambda bi, hi, i: (bi, hi, 0, 0))
    return pl.pallas_call(
        _attn_kernel,
        out_shape=jax.ShapeDtypeStruct((b, t, h * dv), BF16),
        grid=(b, h, t // tq),
        in_specs=[pl.BlockSpec((None, None, tq, dk), lambda bi, hi, i: (bi, hi, i, 0))]
                 + [whole(k) for k in ks] + [whole(v) for v in vts],
        out_specs=pl.BlockSpec((None, tq, dv), lambda bi, hi, i: (bi, i, hi)),
        scratch_shapes=[pltpu.VMEM((tq // sub, tk, sub), F32)],
        compiler_params=_cp("arbitrary", "arbitrary", "arbitrary"),
        name="attention",
    )(q, *ks, *vts)


def _sink_attn_kernel(sink_ref, q_ref, k_ref, v_ref, o_ref):
    group = q_ref.shape[0]
    k = k_ref[...]
    v = v_ref[...]
    for g in range(group):
        s = _dot_nt(q_ref[g], k)
        sk = sink_ref[pl.program_id(1) * group + g]
        m = jnp.maximum(jnp.max(s, axis=1, keepdims=True), sk)
        p = jnp.exp(s - m)
        l = jnp.sum(p, axis=1, keepdims=True) + jnp.exp(sk - m)
        o_ref[g] = (_dot(p.astype(BF16), v) / l).astype(o_ref.dtype)


def _sink_attention(q, k, v, sink):
    b, hq, t, d = q.shape
    hk = k.shape[1]
    group = hq // hk
    kv_spec = pl.BlockSpec((None, None, t, d), lambda bi, h: (bi, h, 0, 0))
    q_spec = pl.BlockSpec((None, group, t, d), lambda bi, h: (bi, h, 0, 0))
    return pl.pallas_call(
        _sink_attn_kernel,
        out_shape=jax.ShapeDtypeStruct((b, hq, t, d), BF16),
        grid=(b, hk),
        in_specs=[pl.BlockSpec(memory_space=pltpu.SMEM), q_spec, kv_spec, kv_spec],
        out_specs=q_spec,
        compiler_params=_cp("arbitrary", "arbitrary"),
        name="sink_attention",
    )(sink, q, k, v)


def _window_kernel(sink_ref, bias_ref, qt_ref, kp_ref, kc_ref, kn_ref, kx_ref, vp_ref, vc_ref, vn_ref, vx_ref, o_ref):
    hk = pl.program_id(1)
    kk = jnp.concatenate([kx_ref[...], kp_ref[...], kc_ref[...], kn_ref[...]], axis=0)
    vt = jnp.concatenate([vx_ref[...], vp_ref[...], vc_ref[...], vn_ref[...]], axis=1)
    bias = bias_ref[...]
    s = _dot(kk, qt_ref[...])
    s = jnp.concatenate([s[:, g * Q_BLOCK:(g + 1) * Q_BLOCK] + bias for g in range(WIN_GROUP)], axis=1)
    sk = jnp.concatenate([jnp.full((1, Q_BLOCK), sink_ref[hk * WIN_GROUP + g], F32) for g in range(WIN_GROUP)],
                         axis=1)
    m = jnp.maximum(jnp.max(s, axis=0, keepdims=True), sk)
    e = jnp.exp(s - m)
    l = jnp.sum(e, axis=0, keepdims=True) + jnp.exp(sk - m)
    o_ref[...] = (_dot(vt, e.astype(BF16)) / l).astype(o_ref.dtype)


def _window_bias(n_ctx):
    c = np.arange(n_ctx + 3 * Q_BLOCK)[:, None]
    r = np.arange(Q_BLOCK)[None, :]
    band = (c < n_ctx) | (np.abs(c - n_ctx - WINDOW - r) <= WINDOW)
    in_prev = (c >= n_ctx) & (c < n_ctx + Q_BLOCK)
    in_next = c >= n_ctx + 2 * Q_BLOCK
    ok = np.stack([band & ~in_prev, band, band & ~in_next])
    return jnp.asarray(np.where(ok, 0.0, NEG_BIG).astype(np.float32))


def _window_attention(qkv, qkv_ctx, sink):
    b, s, _ = qkv.shape
    n_ctx = qkv_ctx.shape[1]
    hd, hq, hk, grp = WIN_HEAD_DIM, WIN_Q_HEADS, WIN_KV_HEADS, WIN_GROUP
    qw, kw = hq * hd, hk * hd
    nblk = s // Q_BLOCK
    assert nblk >= 2 and WINDOW == Q_BLOCK
    qt = qkv[..., :qw].reshape(b, nblk, Q_BLOCK, hk, grp, hd).transpose(0, 3, 1, 5, 4, 2)
    qt = qt.reshape(b, hk, nblk, hd, grp * Q_BLOCK)
    k = _to_heads(qkv[..., qw:qw + kw], hk)
    kx = _to_heads(qkv_ctx[..., qw:qw + kw], hk)
    vt = qkv[..., qw + kw:].reshape(b, s, hk, hd).transpose(0, 2, 3, 1)
    vxt = qkv_ctx[..., qw + kw:].reshape(b, n_ctx, hk, hd).transpose(0, 2, 3, 1)

    prev = lambda i: jnp.maximum(i - 1, 0)
    nxt = lambda i: jnp.minimum(i + 1, nblk - 1)
    kblk = lambda f: pl.BlockSpec((None, None, Q_BLOCK, hd), lambda bi, h, i: (bi, h, f(i), 0))
    vblk = lambda f: pl.BlockSpec((None, None, hd, Q_BLOCK), lambda bi, h, i: (bi, h, 0, f(i)))
    same = lambda i: i
    edge = lambda bi, h, i: (jnp.where(i == 0, 0, jnp.where(i == nblk - 1, 2, 1)), 0, 0)
    qo_spec = pl.BlockSpec((None, None, None, hd, grp * Q_BLOCK), lambda bi, h, i: (bi, h, i, 0, 0))
    ot = pl.pallas_call(
        _window_kernel,
        out_shape=jax.ShapeDtypeStruct((b, hk, nblk, hd, grp * Q_BLOCK), BF16),
        grid=(b, hk, nblk),
        in_specs=[pl.BlockSpec(memory_space=pltpu.SMEM),
                  pl.BlockSpec((None, n_ctx + 3 * Q_BLOCK, Q_BLOCK), edge),
                  qo_spec,
                  kblk(prev), kblk(same), kblk(nxt),
                  pl.BlockSpec((None, None, n_ctx, hd), lambda bi, h, i: (bi, h, 0, 0)),
                  vblk(prev), vblk(same), vblk(nxt),
                  pl.BlockSpec((None, None, hd, n_ctx), lambda bi, h, i: (bi, h, 0, 0))],
        out_specs=qo_spec,
        compiler_params=_cp("arbitrary", "arbitrary", "arbitrary"),
        name="window_attention",
    )(sink, _window_bias(n_ctx), qt, k, k, k, kx, vt, vt, vt, vxt)
    o = ot.reshape(b, hk, nblk, hd, grp, Q_BLOCK).transpose(0, 2, 5, 1, 4, 3)
    return o.reshape(b, s, qw)


def _shortconv_kernel(u_ref, w_ref, b_ref, o_ref):
    u = u_ref[...]
    rows = u.shape[0]
    t = lax.broadcasted_iota(jnp.int32, u.shape, 0)
    before = jnp.where(t == 0, 0.0, pltpu.roll(u, 1, 0))
    after = jnp.where(t == rows - 1, 0.0, pltpu.roll(u, rows - 1, 0))
    o_ref[...] = before * w_ref[0:1, :] + u * w_ref[1:2, :] + after * w_ref[2:3, :] + b_ref[...]


def _shortconv(u, w, bias, tc=256):
    b, t, n = u.shape
    return pl.pallas_call(
        _shortconv_kernel,
        out_shape=jax.ShapeDtypeStruct((b, t, n), F32),
        grid=(b, n // tc),
        in_specs=[pl.BlockSpec((None, t, tc), lambda bi, j: (bi, 0, j)),
                  pl.BlockSpec((3, tc), lambda bi, j: (0, j)),
                  pl.BlockSpec((1, tc), lambda bi, j: (0, j))],
        out_specs=pl.BlockSpec((None, t, tc), lambda bi, j: (bi, 0, j)),
        compiler_params=_cp("arbitrary", "arbitrary"),
        name="short_conv",
    )(u, w, bias.reshape(1, n))


def _filter_kernel(z_ref, t_ref, w1_ref, b1_ref, w2_ref, b2_ref, w3_ref, b3_ref, fr_ref, w4_ref, dec_ref, o_ref, *,
                   zero_row):
    def dense(hv, w_ref):
        a_hi, a_lo = _split(hv)
        w_hi, w_lo = _split(w_ref[...])
        return _dot3(a_hi, a_lo, w_hi, w_lo)

    freq = fr_ref[...]
    hv = jnp.sin(freq * (dense(z_ref[...], w1_ref) + b1_ref[...]))
    hv = jnp.sin(freq * (dense(hv, w2_ref) + b2_ref[...]))
    hv = jnp.sin(freq * (dense(hv, w3_ref) + b3_ref[...]))
    out = dense(hv, w4_ref) * jnp.exp(-t_ref[...] * jnp.abs(dec_ref[...]))
    tr = out.shape[0]
    row = pl.program_id(1) * tr + lax.broadcasted_iota(jnp.int32, out.shape, 0)
    o_ref[...] = jnp.where(row == zero_row, 0.0, out)


def _hyena_filter(seq, f_w1, f_b1, f_w2, f_b2, f_w3, f_b3, f_freq, f_w4, decay):
    d = f_w4.shape[1] // (2 * HY_ORDER)
    n = 2 * seq
    t = np.linspace(0.0, 1.0, seq, dtype=np.float32)[:, None]
    w = (2.0 * math.pi * np.arange(seq, dtype=np.float32)[:, None] / seq).astype(np.float32)
    f = np.linspace(1e-4, HY_BANDS - 1, HY_BANDS, dtype=np.float32)[None, :]
    z = np.concatenate([t, np.cos(f * w), -np.sin(f * w)], axis=-1).astype(np.float32)
    lag = np.arange(n)
    lag = np.where(lag < seq, lag, n - lag) % seq
    zc = np.zeros((n, LANES), np.float32)
    zc[:, :HY_EMB] = z[lag]
    tc = t[lag]
    w1 = jnp.zeros((LANES, HY_FILTER_W), F32).at[:HY_EMB].set(f_w1)
    tr = min(512, seq)
    fw = HY_FILTER_W
    row = lambda v: v.reshape(1, -1)
    const = lambda o, i: (0, 0)
    col = lambda o, i: (0, 2 * o + (i * tr) // seq)
    return pl.pallas_call(
        functools.partial(_filter_kernel, zero_row=seq),
        out_shape=jax.ShapeDtypeStruct((HY_ORDER, n, d), F32),
        grid=(HY_ORDER, n // tr),
        in_specs=[pl.BlockSpec((tr, LANES), lambda o, i: (i, 0)),
                  pl.BlockSpec((tr, 1), lambda o, i: (i, 0)),
                  pl.BlockSpec((LANES, fw), const), pl.BlockSpec((1, fw), const),
                  pl.BlockSpec((fw, fw), const), pl.BlockSpec((1, fw), const),
                  pl.BlockSpec((fw, fw), const), pl.BlockSpec((1, fw), const),
                  pl.BlockSpec((1, fw), const),
                  pl.BlockSpec((fw, d), col),
                  pl.BlockSpec((1, d), col)],
        out_specs=pl.BlockSpec((None, tr, d), lambda o, i: (o, i, 0)),
        compiler_params=_cp("arbitrary", "arbitrary"),
        name="hyena_filter",
    )(jnp.asarray(zc), jnp.asarray(tc), w1, row(f_b1), f_w2, row(f_b2), f_w3, row(f_b3), row(f_freq), f_w4,
      row(decay))


def _stack_complex(c):
    return np.block([[c.real, -c.imag], [c.imag, c.real]])


def _hi_lo(m):
    m = jnp.asarray(np.asarray(m, np.float32))
    hi = m.astype(BF16)
    lo = (m - hi.astype(F32)).astype(BF16)
    return hi, lo


def _fft_split(n):
    n2 = 128
    return n // n2, n2


@functools.lru_cache(maxsize=None)
def _fft_matrices_np(n):
    n1, n2 = _fft_split(n)
    h = n1 // 2
    k1 = np.arange(n1)
    bb = np.arange(n2)
    tw = np.exp(-2j * np.pi * np.outer(bb, k1) / n)
    f1 = np.exp(-2j * np.pi * np.outer(k1, np.arange(n1)) / n1)
    la = np.stack([_stack_complex(tw[b][:, None] * f1[:, :h]) for b in range(n2)])
    lf = np.stack([np.concatenate([(tw[b][:, None] * f1).real, (tw[b][:, None] * f1).imag], axis=0)
                   for b in range(n2)])
    f2 = np.exp(-2j * np.pi * np.outer(np.arange(n2), np.arange(n2)) / n2)
    lc = _stack_complex(f2)
    lci = _stack_complex(np.conj(f2))
    f1i = np.exp(2j * np.pi * np.outer(np.arange(h), k1) / n1)
    lai = np.stack([_stack_complex(f1i * np.conj(tw[b])[None, :] / n) for b in range(n2)])
    return la, lf, lc, lci, lai


FFT_UNROLL = 8
MID_UNROLL = 4


def _fwd_strided_stage(read_rows, mat_hi_ref, mat_lo_ref, t_ref, n1, n2):
    def body(b, carry):
        hi, lo = _split(read_rows(b))
        out = _dot3(mat_hi_ref[b], mat_lo_ref[b], hi, lo)
        off = pl.multiple_of(b * n1, n1)
        t_ref[0, pl.ds(off, n1), :] = out[:n1]
        t_ref[1, pl.ds(off, n1), :] = out[n1:]
        return carry
    lax.fori_loop(0, n2, body, 0, unroll=FFT_UNROLL)


def _gather_stacked(ref, start, count, stride):
    return jnp.concatenate([ref[0, pl.ds(start, count, stride=stride), :],
                            ref[1, pl.ds(start, count, stride=stride), :]], axis=0)


def _gather_pair(ref, k1, count, stride):
    return jnp.concatenate([_gather_stacked(ref, k1, count, stride),
                            _gather_stacked(ref, k1 + 1, count, stride)], axis=1)


def _load_pair(ref, plane, off, rows):
    blk = ref[plane, pl.ds(off, 2 * rows), :]
    return jnp.concatenate([blk[:rows], blk[rows:]], axis=1)


def _store_pair(ref, plane, off, rows, val):
    td = val.shape[1] // 2
    ref[plane, pl.ds(off, rows), :] = val[:, :td]
    ref[plane, pl.ds(off + rows, rows), :] = val[:, td:]


def _spectrum_kernel(f_ref, lf_hi, lf_lo, lc_hi, lc_lo, h_ref, t_scr, *, n1, n2):
    _fwd_strided_stage(lambda b: f_ref[pl.ds(b, n1, stride=n2), :], lf_hi, lf_lo, t_scr, n1, n2)

    def body(i, carry):
        hi, lo = _split(_gather_pair(t_scr, 2 * i, n2, n1))
        out = _dot3(lc_hi[...], lc_lo[...], hi, lo)
        off = pl.multiple_of(2 * i * n2, 2 * n2)
        _store_pair(h_ref, 0, off, n2, out[:n2])
        _store_pair(h_ref, 1, off, n2, out[n2:])
        return carry
    lax.fori_loop(0, n1 // 2, body, 0, unroll=MID_UNROLL)


def _spectrum(filt, td=LANES):
    order, n, d = filt.shape
    n1, n2 = _fft_split(n)
    _, lf, lc, _, _ = _fft_matrices_np(n)
    lf_hi, lf_lo = _hi_lo(lf)
    lc_hi, lc_lo = _hi_lo(lc)
    c3 = lambda o, j: (0, 0, 0)
    c2 = lambda o, j: (0, 0)
    return pl.pallas_call(
        functools.partial(_spectrum_kernel, n1=n1, n2=n2),
        out_shape=jax.ShapeDtypeStruct((order, 2, n, d), F32),
        grid=(order, d // td),
        in_specs=[pl.BlockSpec((None, n, td), lambda o, j: (o, 0, j)),
                  _single(lf_hi.shape, c3), _single(lf_lo.shape, c3),
                  _single(lc_hi.shape, c2), _single(lc_lo.shape, c2)],
        out_specs=pl.BlockSpec((None, 2, n, td), lambda o, j: (o, 0, 0, j)),
        scratch_shapes=[pltpu.VMEM((2, n, td), F32)],
        compiler_params=_cp("arbitrary", "arbitrary"),
        name="filter_spectrum",
    )(filt, lf_hi, lf_lo, lc_hi, lc_lo)


def _fftconv_kernel(z_ref, g_ref, h_ref, skip_ref, la_hi, la_lo, lc_hi, lc_lo, lci_hi, lci_lo, lai_hi, lai_lo,
                    o_ref, t_scr, u_scr, *, n1, n2):
    half = n1 // 2
    seq = half * n2
    _fwd_strided_stage(lambda b: _gather_stacked(z_ref, b, half, n2), la_hi, la_lo, t_scr, n1, n2)

    def mid(i, carry):
        hi, lo = _split(_gather_pair(t_scr, 2 * i, n2, n1))
        x = _dot3(lc_hi[...], lc_lo[...], hi, lo)
        off = pl.multiple_of(2 * i * n2, 2 * n2)
        xr, xi = x[:n2], x[n2:]
        hr = _load_pair(h_ref, 0, off, n2)
        hi_ = _load_pair(h_ref, 1, off, n2)
        _store_pair(u_scr, 0, off, n2, xr * hr - xi * hi_)
        _store_pair(u_scr, 1, off, n2, xr * hi_ + xi * hr)
        return carry
    lax.fori_loop(0, n1 // 2, mid, 0, unroll=MID_UNROLL)

    def mid_inv(i, carry):
        off = pl.multiple_of(2 * i * n2, 2 * n2)
        y = jnp.concatenate([_load_pair(u_scr, 0, off, n2), _load_pair(u_scr, 1, off, n2)], axis=0)
        y_hi, y_lo = _split(y)
        w = _dot3(lci_hi[...], lci_lo[...], y_hi, y_lo)
        _store_pair(u_scr, 0, off, n2, w[:n2])
        _store_pair(u_scr, 1, off, n2, w[n2:])
        return carry
    lax.fori_loop(0, n1 // 2, mid_inv, 0, unroll=MID_UNROLL)

    def inv(b, carry):
        hi, lo = _split(_gather_stacked(u_scr, b, n1, n2))
        out = _dot3(lai_hi[b], lai_lo[b], hi, lo)
        off = pl.multiple_of(b * half, half)
        t_scr[0, pl.ds(off, half), :] = out[:half]
        t_scr[1, pl.ds(off, half), :] = out[half:]
        return carry
    lax.fori_loop(0, n2, inv, 0, unroll=FFT_UNROLL)

    skip = skip_ref[...]

    def epi(a, carry):
        off = pl.multiple_of(a * n2, n2)
        for plane in range(2):
            y = t_scr[plane, pl.ds(a, n2, stride=half), :]
            zc = z_ref[plane, pl.ds(off, n2), :]
            gc = g_ref[plane, pl.ds(off, n2), :]
            o_ref[plane, pl.ds(off, n2), :] = (gc * (y + zc * skip)).astype(o_ref.dtype)
        return carry
    lax.fori_loop(0, half, epi, 0)
    del seq


def _fftconv(z_arr, z_col, g_arr, g_col, spec, order, skip, out_dtype, td=LANES):
    b, seq, _ = z_arr.shape
    d = spec.shape[3]
    n = 2 * seq
    n1, n2 = _fft_split(n)
    nblk = d // td
    la, _, lc, lci, lai = _fft_matrices_np(n)
    mats = [*_hi_lo(la), *_hi_lo(lc), *_hi_lo(lci), *_hi_lo(lai)]
    mat_specs = [_single(m.shape, (lambda j, p, nd=m.ndim: (0,) * nd)) for m in mats]
    return pl.pallas_call(
        functools.partial(_fftconv_kernel, n1=n1, n2=n2),
        out_shape=jax.ShapeDtypeStruct((b, seq, d), out_dtype),
        grid=(nblk, b // 2),
        in_specs=[_single((2, seq, td), lambda j, p: (p, 0, z_col * nblk + j)),
                  _single((2, seq, td), lambda j, p: (p, 0, g_col * nblk + j)),
                  _single((None, 2, n, td), lambda j, p: (order, 0, 0, j)),
                  pl.BlockSpec((None, 1, td), lambda j, p: (order, 0, j))] + mat_specs,
        out_specs=_single((2, seq, td), lambda j, p: (p, 0, j)),
        scratch_shapes=[pltpu.VMEM((2, n, td), F32), pltpu.VMEM((2, n, td), F32)],
        compiler_params=_cp("arbitrary", "arbitrary"),
        name="fft_conv",
    )(z_arr, g_arr, spec, skip.reshape(HY_ORDER, 1, d), *mats)


@functools.lru_cache(maxsize=None)
def _dense_dft_np(seq):
    n = 2 * seq
    k = np.arange(n)
    f = np.exp(-2j * np.pi * np.outer(k, k) / n)
    fwd = _stack_complex(f[:, :seq])
    flt = np.concatenate([f.real, f.imag], axis=0)
    inv = _stack_complex(np.conj(f)[:seq, :] / n)
    return fwd, flt, inv


def _dense_spectrum_kernel(f_ref, m_hi, m_lo, h_ref):
    hi, lo = _split(f_ref[...])
    out = _dot3(m_hi[...], m_lo[...], hi, lo)
    n = f_ref.shape[0]
    h_ref[0] = out[:n]
    h_ref[1] = out[n:]


def _dense_spectrum(filt, td=256):
    order, n, d = filt.shape
    _, flt, _ = _dense_dft_np(n // 2)
    m_hi, m_lo = _hi_lo(flt)
    c2 = lambda o, j: (0, 0)
    return pl.pallas_call(
        _dense_spectrum_kernel,
        out_shape=jax.ShapeDtypeStruct((order, 2, n, d), F32),
        grid=(order, d // td),
        in_specs=[pl.BlockSpec((None, n, td), lambda o, j: (o, 0, j)),
                  pl.BlockSpec(m_hi.shape, c2), pl.BlockSpec(m_lo.shape, c2)],
        out_specs=pl.BlockSpec((None, 2, n, td), lambda o, j: (o, 0, 0, j)),
        compiler_params=_cp("arbitrary", "arbitrary"),
        name="filter_spectrum_dense",
    )(filt, m_hi, m_lo)


def _dense_conv_kernel(z_ref, g_ref, h_ref, skip_ref, f_hi, f_lo, i_hi, i_lo, o_ref):
    seq = z_ref.shape[1]
    n = 2 * seq
    zz = jnp.concatenate([z_ref[0], z_ref[1]], axis=0)
    hi, lo = _split(zz)
    x = _dot3(f_hi[...], f_lo[...], hi, lo)
    xr, xi = x[:n], x[n:]
    hr, hi_ = h_ref[0], h_ref[1]
    y = jnp.concatenate([xr * hr - xi * hi_, xr * hi_ + xi * hr], axis=0)
    y_hi, y_lo = _split(y)
    w = _dot3(i_hi[...], i_lo[...], y_hi, y_lo)
    skip = skip_ref[...]
    for plane in range(2):
        y_p = w[plane * seq:(plane + 1) * seq]
        o_ref[plane] = (g_ref[plane] * (y_p + z_ref[plane] * skip)).astype(o_ref.dtype)


def _dense_conv(z_arr, z_col, g_arr, g_col, spec, order, skip, out_dtype, td=256):
    b, seq, _ = z_arr.shape
    d = spec.shape[3]
    n = 2 * seq
    nblk = d // td
    fwd, _, inv = _dense_dft_np(seq)
    mats = [*_hi_lo(fwd), *_hi_lo(inv)]
    c2 = lambda j, p: (0, 0)
    return pl.pallas_call(
        _dense_conv_kernel,
        out_shape=jax.ShapeDtypeStruct((b, seq, d), out_dtype),
        grid=(nblk, b // 2),
        in_specs=[pl.BlockSpec((2, seq, td), lambda j, p: (p, 0, z_col * nblk + j)),
                  pl.BlockSpec((2, seq, td), lambda j, p: (p, 0, g_col * nblk + j)),
                  pl.BlockSpec((None, 2, n, td), lambda j, p: (order, 0, 0, j)),
                  pl.BlockSpec((None, 1, td), lambda j, p: (order, 0, j))]
                 + [pl.BlockSpec(m.shape, c2) for m in mats],
        out_specs=pl.BlockSpec((2, seq, td), lambda j, p: (p, 0, j)),
        compiler_params=_cp("arbitrary", "arbitrary"),
        name="dense_conv",
    )(z_arr, g_arr, spec, skip.reshape(HY_ORDER, 1, d), *mats)


DENSE_DFT_MAX_SEQ = 512


def _hyena(x, gain, shift, scale, p):
    b, seq, d = x.shape
    u = _nm_matmul(x, gain, shift, scale, p["w_in"], F32, tn=1024)
    u = _shortconv(u, p["conv_w"], p["conv_b"])
    filt = _hyena_filter(seq, *p["filter"])
    dense = seq <= DENSE_DFT_MAX_SEQ
    spec = _dense_spectrum(filt) if dense else _spectrum(filt)
    conv = _dense_conv if dense else _fftconv
    z1 = conv(u, 0, u, 1, spec, 0, p["skip"], F32)
    return conv(z1, 0, u, 2, spec, 1, p["skip"], BF16)


def _rope_tables(seq):
    t = np.arange(seq)
    pos = np.stack([t // GRID_W, t % GRID_W], axis=1).astype(np.float32)
    n = 16
    inv = jnp.asarray(ROPE_THETA, F32) ** (-jnp.arange(n, dtype=F32) / n)
    lane = np.arange(LANES)
    axis = (lane % 64) // 32
    idx = lane % 16
    sign = np.where(lane % 32 < 16, -1.0, 1.0).astype(np.float32)
    ang = jnp.asarray(pos)[:, axis] * inv[idx][None, :]
    return jnp.cos(ang), jnp.sin(ang) * sign[None, :]


def _identity_tables(seq):
    return jnp.ones((seq, LANES), F32), jnp.zeros((seq, LANES), F32)


def _mla_weights(w_dq, w_uq, w_dkv, w_ukv):
    d = w_dq.shape[0]
    hh = MLA_HEADS
    wd = jnp.concatenate([w_dq, w_dkv, jnp.zeros((d, LANES - MLA_ROPE), F32)], axis=1).astype(BF16)
    uq = w_uq.reshape(MLA_RANK, hh, MLA_NOPE + MLA_ROPE)
    uq = jnp.pad(uq, ((0, 0), (0, 0), (0, MLA_HEAD_PAD - MLA_NOPE - MLA_ROPE)))
    ukv = w_ukv.reshape(MLA_RANK, hh, MLA_NOPE + MLA_V)
    ukv = jnp.concatenate([ukv[:, :, :MLA_NOPE].reshape(MLA_RANK, hh * MLA_NOPE),
                           ukv[:, :, MLA_NOPE:].reshape(MLA_RANK, hh * MLA_V)], axis=1)
    return wd, uq.reshape(MLA_RANK, hh * MLA_HEAD_PAD).astype(BF16), ukv.astype(BF16)


def _ffn_weights(wg, wu, wd):
    pad = (-wg.shape[3]) % FFN_TILE
    col = ((0, 0), (0, 0), (0, 0), (0, pad))
    row = ((0, 0), (0, 0), (0, pad), (0, 0))
    return jnp.pad(wg, col).astype(BF16), jnp.pad(wu, col).astype(BF16), jnp.pad(wd, row).astype(BF16)


def _to_heads(a, heads):
    b, t, _ = a.shape
    return a.reshape(b, t, heads, -1).transpose(0, 2, 1, 3)


def _from_heads(a):
    b, h, t, e = a.shape
    return a.transpose(0, 2, 1, 3).reshape(b, t, h * e)


def kernel(x, c, ctx, c_ctx, mod_w, mod_b, norm_g, final_g, ffn_wg, ffn_wu, ffn_wd, mla_w_dq, mla_g_q, mla_w_uq, mla_w_dkv, mla_g_kv, mla_w_ukv, mla_w_o, hy_w_in, hy_conv_w, hy_conv_b, hy_f_w1, hy_f_b1, hy_f_w2, hy_f_b2, hy_f_w3, hy_f_b3, hy_f_freq, hy_f_w4, hy_decay, hy_skip, hy_w_out, win_w_qkv, win_sink, win_w_o):
    b, s, d = x.shape
    n_ctx = ctx.shape[1]
    depth = mod_w.shape[0]
    assert b % 2 == 0 and s % 512 == 0 and n_ctx % 256 == 0

    cc = jnp.concatenate([c, c_ctx[None, :], jnp.zeros((8 - (b + 1) % 8, d), F32)], axis=0)
    mods = _modulation(cc, mod_w, mod_b).reshape(depth, cc.shape[0], N_MOD, d)

    ffn_w = _ffn_weights(ffn_wg, ffn_wu, ffn_wd)
    cos_l, sin_l = _rope_tables(s)
    cos_c, sin_c = _identity_tables(n_ctx)

    for i in range(depth):
        kind, j = i % N_MIXERS, i // N_MIXERS
        need_ctx = i < depth - 1
        ctx_live = need_ctx or kind != 1
        ml = [mods[i, :b, k][:, None, :] for k in range(N_MOD)]
        mc = [jnp.broadcast_to(mods[i, b, k][None, None, :], (b, 1, d)) for k in range(N_MOD)]

        x = _ffn(x, norm_g[i, 0], ml[0], ml[1], ml[2], *ffn_w, i, 0)
        if ctx_live:
            ctx = _ffn(ctx, norm_g[i, 0], mc[0], mc[1], mc[2], *ffn_w, i, 0)

        g1 = norm_g[i, 1]
        o_c = None
        if kind == 0:
            wd, wuq, wukv = _mla_weights(mla_w_dq[j], mla_w_uq[j], mla_w_dkv[j], mla_w_ukv[j])
            q_l, k_l, v_l = _mla_proj(x, g1, ml[3], ml[4], wd, mla_g_q[j], mla_g_kv[j], wuq, wukv, cos_l, sin_l)
            q_c, k_c, v_c = _mla_proj(ctx, g1, mc[3], mc[4], wd, mla_g_q[j], mla_g_kv[j], wuq, wukv, cos_c, sin_c)
            w_o = mla_w_o[j].astype(BF16)
            o_l = _attention(q_l, [k_c, k_l], [v_c, v_l])
            if need_ctx:
                o_c = _attention(q_c, [k_c], [v_c])
        elif kind == 1:
            hp = dict(w_in=hy_w_in[j].astype(BF16), conv_w=hy_conv_w[j], conv_b=hy_conv_b[j],
                      filter=(hy_f_w1[j], hy_f_b1[j], hy_f_w2[j], hy_f_b2[j], hy_f_w3[j], hy_f_b3[j],
                              hy_f_freq[j], hy_f_w4[j], hy_decay[j]),
                      skip=hy_skip[j])
            w_o = hy_w_out[j].astype(BF16)
            o_l = _hyena(x, g1, ml[3], ml[4], hp)
            if need_ctx:
                o_c = _hyena(ctx, g1, mc[3], mc[4], hp)
        else:
            qw = WIN_Q_HEADS * WIN_HEAD_DIM
            kw = WIN_KV_HEADS * WIN_HEAD_DIM
            w_qkv = win_w_qkv[j].astype(BF16)
            w_o = win_w_o[j].astype(BF16)
            q_chunks = qw // LANES
            rope_l = (cos_l, sin_l, (qw + kw) // LANES, q_chunks, WIN_HEAD_DIM ** -0.5)
            rope_c = (cos_c, sin_c, (qw + kw) // LANES, q_chunks, WIN_HEAD_DIM ** -0.5)
            qkv_l = _nm_matmul(x, g1, ml[3], ml[4], w_qkv, BF16, tn=qw + 2 * kw, rope=rope_l)
            qkv_c = _nm_matmul(ctx, g1, mc[3], mc[4], w_qkv, BF16, tn=qw + 2 * kw, rope=rope_c)
            o_l = _window_attention(qkv_l, qkv_c, win_sink[j])
            if need_ctx:
                q_c = _to_heads(qkv_c[..., :qw], WIN_Q_HEADS)
                k_c = _to_heads(qkv_c[..., qw:qw + kw], WIN_KV_HEADS)
                v_c = _to_heads(qkv_c[..., qw + kw:], WIN_KV_HEADS)
                o_c = _from_heads(_sink_attention(q_c, k_c, v_c, win_sink[j]))

        x = _mm_res(o_l, w_o, x, ml[5])
        x = _ffn(x, norm_g[i, 2], ml[6], ml[7], ml[8], *ffn_w, i, 1)
        if need_ctx:
            ctx = _mm_res(o_c, w_o, ctx, mc[5])
            ctx = _ffn(ctx, norm_g[i, 2], mc[6], mc[7], mc[8], *ffn_w, i, 1)
    return _final_norm(x, final_g)
```

```python
import functools
import math

import numpy as np
import jax
import jax.numpy as jnp
from jax import lax
from jax.experimental import pallas as pl
from jax.experimental.pallas import tpu as pltpu

F32 = jnp.float32
BF16 = jnp.bfloat16

GRID_W = 64
N_MOD = 9
FFN_RES = 0.5
NORM_EPS = 1e-6
ROPE_THETA = 10000.0
N_MIXERS = 3
MLA_HEADS = 16
MLA_NOPE = 128
MLA_ROPE = 64
MLA_V = 128
MLA_RANK = 512
MLA_HEAD_PAD = 256
HY_ORDER = 2
HY_EMB = 33
HY_BANDS = (HY_EMB - 1) // 2
HY_FILTER_W = 64
WIN_HEAD_DIM = 64
WIN_Q_HEADS = 32
WIN_KV_HEADS = 4
WIN_GROUP = WIN_Q_HEADS // WIN_KV_HEADS
WINDOW = 128
Q_BLOCK = 128

LANES = 128
V7X_VMEM_LIMIT = 56 * 1024 * 1024
FFN_TILE = 512
NEG_BIG = -1e30


def _cp(*sem):
    return pltpu.CompilerParams(dimension_semantics=sem, vmem_limit_bytes=V7X_VMEM_LIMIT)


def _single(shape, imap):
    return pl.BlockSpec(shape, imap, pipeline_mode=pl.Buffered(1))


def _dot(a, b):
    return jnp.dot(a, b, preferred_element_type=F32)


def _dot_nt(a, b):
    return lax.dot_general(a, b, (((1,), (1,)), ((), ())), preferred_element_type=F32)


def _split(x):
    hi = x.astype(BF16)
    lo = (x - hi.astype(F32)).astype(BF16)
    return hi, lo


def _dot3(a_hi, a_lo, b_hi, b_lo):
    return _dot(a_hi, b_hi) + (_dot(a_hi, b_lo) + _dot(a_lo, b_hi))


def _modnorm(x, gain, shift, scale):
    r = lax.rsqrt(jnp.mean(x * x, axis=-1, keepdims=True) + NORM_EPS)
    return (x * r * gain) * (1.0 + scale) + shift


def _rms(x, gain):
    return x * lax.rsqrt(jnp.mean(x * x, axis=-1, keepdims=True) + NORM_EPS) * gain


def _rope128(x, cos, sin):
    lane = lax.broadcasted_iota(jnp.int32, x.shape, 1)
    partner = jnp.where(lane % 32 < 16, pltpu.roll(x, LANES - 16, 1), pltpu.roll(x, 16, 1))
    return x * cos + partner * sin


def _mod_kernel(a_ref, w_ref, b_ref, o_ref):
    a = a_ref[...]
    a = a * jax.nn.sigmoid(a)
    a_hi, a_lo = _split(a)
    w_hi, w_lo = _split(w_ref[...])
    o_ref[...] = _dot3(a_hi, a_lo, w_hi, w_lo) + b_ref[...]


def _modulation(cc, mod_w, mod_b):
    depth, d, n = mod_w.shape
    rows = cc.shape[0]
    tn = n // 16
    return pl.pallas_call(
        _mod_kernel,
        out_shape=jax.ShapeDtypeStruct((depth, rows, n), F32),
        grid=(depth, n // tn),
        in_specs=[
            pl.BlockSpec((rows, d), lambda l, j: (0, 0)),
            pl.BlockSpec((None, d, tn), lambda l, j: (l, 0, j)),
            pl.BlockSpec((None, 1, tn), lambda l, j: (l, 0, j)),
        ],
        out_specs=pl.BlockSpec((None, rows, tn), lambda l, j: (l, 0, j)),
        compiler_params=_cp("arbitrary", "arbitrary"),
        name="modulation",
    )(cc, mod_w, mod_b.reshape(depth, 1, n))


def _ffn_kernel(x_ref, g_ref, sh_ref, sc_ref, gt_ref, wg_ref, wu_ref, wd_ref, o_ref, h_scr, acc_scr):
    j = pl.program_id(2)

    @pl.when(j == 0)
    def _():
        h_scr[...] = _modnorm(x_ref[...], g_ref[...], sh_ref[...], sc_ref[...]).astype(BF16)
        acc_scr[...] = jnp.zeros_like(acc_scr)

    h = h_scr[...]
    g = _dot(h, wg_ref[...])
    u = _dot(h, wu_ref[...])
    a = (g * jax.nn.sigmoid(g) * u).astype(BF16)
    acc_scr[...] += _dot(a, wd_ref[...])

    @pl.when(j == pl.num_programs(2) - 1)
    def _():
        o_ref[...] = x_ref[...] + FFN_RES * gt_ref[...] * acc_scr[...]


def _ffn(x, gain, shift, scale, gate, wg, wu, wd, layer, slot, tm=512):
    b, t, d = x.shape
    nj = wg.shape[2]
    tm = min(tm, t)
    vec = pl.BlockSpec((None, 1, d), lambda bi, i, j: (bi, 0, 0))
    return pl.pallas_call(
        _ffn_kernel,
        out_shape=jax.ShapeDtypeStruct((b, t, d), F32),
        grid=(b, t // tm, nj),
        in_specs=[
            pl.BlockSpec((None, tm, d), lambda bi, i, j: (bi, i, 0)),
            pl.BlockSpec((1, d), lambda bi, i, j: (0, 0)),
            vec, vec, vec,
            pl.BlockSpec((None, None, None, d, FFN_TILE), lambda bi, i, j: (layer, slot, j, 0, 0)),
            pl.BlockSpec((None, None, None, d, FFN_TILE), lambda bi, i, j: (layer, slot, j, 0, 0)),
            pl.BlockSpec((None, None, None, FFN_TILE, d), lambda bi, i, j: (layer, slot, j, 0, 0)),
        ],
        out_specs=pl.BlockSpec((None, tm, d), lambda bi, i, j: (bi, i, 0)),
        scratch_shapes=[pltpu.VMEM((tm, d), BF16), pltpu.VMEM((tm, d), F32)],
        compiler_params=_cp("arbitrary", "arbitrary", "arbitrary"),
        name="ffn",
    )(x, gain.reshape(1, d), shift, scale, gate, wg, wu, wd)


def _nm_matmul_kernel(x_ref, g_ref, sh_ref, sc_ref, w_ref, *rest, rope_chunks, scaled_chunks, out_scale):
    if rope_chunks:
        cos_ref, sin_ref, o_ref, h_scr = rest
    else:
        o_ref, h_scr = rest

    @pl.when(pl.program_id(2) == 0)
    def _():
        h_scr[...] = _modnorm(x_ref[...], g_ref[...], sh_ref[...], sc_ref[...]).astype(BF16)

    y = _dot(h_scr[...], w_ref[...])
    if not rope_chunks:
        o_ref[...] = y.astype(o_ref.dtype)
        return
    cos = cos_ref[...]
    sin = sin_ref[...]
    for c in range(y.shape[1] // LANES):
        yc = y[:, c * LANES:(c + 1) * LANES]
        if c < rope_chunks:
            yc = _rope128(yc, cos, sin)
        if c < scaled_chunks:
            yc = yc * out_scale
        o_ref[:, c * LANES:(c + 1) * LANES] = yc.astype(o_ref.dtype)


def _nm_matmul(x, gain, shift, scale, w, out_dtype, tn, tm=512, rope=None):
    b, t, d = x.shape
    n = w.shape[1]
    tm = min(tm, t)
    vec = pl.BlockSpec((None, 1, d), lambda bi, i, j: (bi, 0, 0))
    in_specs = [
        pl.BlockSpec((None, tm, d), lambda bi, i, j: (bi, i, 0)),
        pl.BlockSpec((1, d), lambda bi, i, j: (0, 0)),
        vec, vec,
        pl.BlockSpec((d, tn), lambda bi, i, j: (0, j)),
    ]
    args = [x, gain.reshape(1, d), shift, scale, w]
    rope_chunks = scaled_chunks = 0
    out_scale = 1.0
    if rope is not None:
        cos, sin, rope_chunks, scaled_chunks, out_scale = rope
        in_specs += [pl.BlockSpec((tm, LANES), lambda bi, i, j: (i, 0))] * 2
        args += [cos, sin]
    return pl.pallas_call(
        functools.partial(_nm_matmul_kernel, rope_chunks=rope_chunks, scaled_chunks=scaled_chunks,
                          out_scale=out_scale),
        out_shape=jax.ShapeDtypeStruct((b, t, n), out_dtype),
        grid=(b, t // tm, n // tn),
        in_specs=in_specs,
        out_specs=pl.BlockSpec((None, tm, tn), lambda bi, i, j: (bi, i, j)),
        scratch_shapes=[pltpu.VMEM((tm, d), BF16)],
        compiler_params=_cp("arbitrary", "arbitrary", "arbitrary"),
        name="norm_matmul",
    )(*args)


def _mm_res_kernel(a_ref, w_ref, r_ref, gt_ref, o_ref):
    o_ref[...] = r_ref[...] + gt_ref[...] * _dot(a_ref[...], w_ref[...])


def _mm_res(a, w, res, gate, tm=512):
    b, t, k = a.shape
    n = w.shape[1]
    tm = min(tm, t)
    return pl.pallas_call(
        _mm_res_kernel,
        out_shape=jax.ShapeDtypeStruct((b, t, n), F32),
        grid=(b, t // tm),
        in_specs=[
            pl.BlockSpec((None, tm, k), lambda bi, i: (bi, i, 0)),
            pl.BlockSpec((k, n), lambda bi, i: (0, 0)),
            pl.BlockSpec((None, tm, n), lambda bi, i: (bi, i, 0)),
            pl.BlockSpec((None, 1, n), lambda bi, i: (bi, 0, 0)),
        ],
        out_specs=pl.BlockSpec((None, tm, n), lambda bi, i: (bi, i, 0)),
        compiler_params=_cp("arbitrary", "arbitrary"),
        name="proj_residual",
    )(a, w, res, gate)


def _final_norm_kernel(x_ref, g_ref, o_ref):
    o_ref[...] = _rms(x_ref[...], g_ref[...])


def _final_norm(x, gain, tm=512):
    b, t, d = x.shape
    return pl.pallas_call(
        _final_norm_kernel,
        out_shape=jax.ShapeDtypeStruct((b, t, d), F32),
        grid=(b, t // tm),
        in_specs=[pl.BlockSpec((None, tm, d), lambda bi, i: (bi, i, 0)),
                  pl.BlockSpec((1, d), lambda bi, i: (0, 0))],
        out_specs=pl.BlockSpec((None, tm, d), lambda bi, i: (bi, i, 0)),
        compiler_params=_cp("arbitrary", "arbitrary"),
        name="final_norm",
    )(x, gain.reshape(1, d))


def _mla_proj_kernel(x_ref, g_ref, sh_ref, sc_ref, wd_ref, gq_ref, gkv_ref, wuq_ref, wukv_ref, cos_ref, sin_ref,
                     q_ref, k_ref, v_ref, *, q_scale):
    h = _modnorm(x_ref[...], g_ref[...], sh_ref[...], sc_ref[...]).astype(BF16)
    a = _dot(h, wd_ref[...])
    cq = _rms(a[:, :MLA_RANK], gq_ref[...]).astype(BF16)
    ckv = _rms(a[:, MLA_RANK:2 * MLA_RANK], gkv_ref[...]).astype(BF16)
    cos = cos_ref[...]
    sin = sin_ref[...]
    kr = _rope128(a[:, 2 * MLA_RANK:], cos, sin).astype(BF16)
    q = _dot(cq, wuq_ref[...])
    kv = _dot(ckv, wukv_ref[...])
    for hd in range(MLA_HEADS):
        base = hd * MLA_HEAD_PAD
        q_ref[hd, :, :LANES] = (q[:, base:base + LANES] * q_scale).astype(BF16)
        q_ref[hd, :, LANES:] = (_rope128(q[:, base + LANES:base + 2 * LANES], cos, sin) * q_scale).astype(BF16)
        k_ref[hd, :, :LANES] = kv[:, hd * MLA_NOPE:(hd + 1) * MLA_NOPE].astype(BF16)
        k_ref[hd, :, LANES:] = kr
        voff = MLA_HEADS * MLA_NOPE + hd * MLA_V
        v_ref[hd] = kv[:, voff:voff + MLA_V].T.astype(BF16)


def _mla_proj(x, gain, shift, scale, wd, gq, gkv, wuq, wukv, cos, sin, tm=256):
    b, t, d = x.shape
    tm = min(tm, t)
    vec = pl.BlockSpec((None, 1, d), lambda bi, i: (bi, 0, 0))
    const = lambda bi, i: (0, 0)
    head_out = lambda w: pl.BlockSpec((None, MLA_HEADS, tm, w), lambda bi, i: (bi, 0, i, 0))
    return pl.pallas_call(
        functools.partial(_mla_proj_kernel, q_scale=(MLA_NOPE + MLA_ROPE) ** -0.5),
        out_shape=[jax.ShapeDtypeStruct((b, MLA_HEADS, t, MLA_HEAD_PAD), BF16),
                   jax.ShapeDtypeStruct((b, MLA_HEADS, t, MLA_HEAD_PAD), BF16),
                   jax.ShapeDtypeStruct((b, MLA_HEADS, MLA_V, t), BF16)],
        grid=(b, t // tm),
        in_specs=[
            pl.BlockSpec((None, tm, d), lambda bi, i: (bi, i, 0)),
            pl.BlockSpec((1, d), const),
            vec, vec,
            _single(wd.shape, const),
            pl.BlockSpec((1, MLA_RANK), const),
            pl.BlockSpec((1, MLA_RANK), const),
            _single(wuq.shape, const),
            _single(wukv.shape, const),
            pl.BlockSpec((tm, LANES), lambda bi, i: (i, 0)),
            pl.BlockSpec((tm, LANES), lambda bi, i: (i, 0)),
        ],
        out_specs=[head_out(MLA_HEAD_PAD), head_out(MLA_HEAD_PAD),
                   pl.BlockSpec((None, MLA_HEADS, MLA_V, tm), lambda bi, i: (bi, 0, 0, i))],
        compiler_params=_cp("arbitrary", "arbitrary"),
        name="mla_proj",
    )(x, gain.reshape(1, d), shift, scale, wd, gq.reshape(1, -1), gkv.reshape(1, -1), wuq, wukv, cos, sin)


ATTN_CHUNK = 256
ATTN_SUB = 256


def _attn_kernel(q_ref, *refs):
    o_ref, s_scr = refs[-2:]
    nseg = (len(refs) - 2) // 2
    k_refs, vt_refs = refs[:nseg], refs[nseg:2 * nseg]
    tq = q_ref.shape[0]
    sub = s_scr.shape[2]
    for u in range(tq // sub):
        q = q_ref[u * sub:(u + 1) * sub, :]
        base = 0
        for k_ref in k_refs:
            for c in range(k_ref.shape[0] // ATTN_CHUNK):
                rows = slice(c * ATTN_CHUNK, (c + 1) * ATTN_CHUNK)
                s_scr[u, base + rows.start:base + rows.stop, :] = _dot_nt(k_ref[rows, :], q)
            base += k_ref.shape[0]
    for u in range(tq // sub):
        s = s_scr[u]
        m = jnp.max(s, axis=0, keepdims=True)
        p = jnp.exp(s - m)
        l = jnp.sum(p, axis=0, keepdims=True)
        pb = p.astype(BF16)
        ot = None
        base = 0
        for vt_ref in vt_refs:
            part = _dot(vt_ref[...], pb[base:base + vt_ref.shape[1], :])
            ot = part if ot is None else ot + part
            base += vt_ref.shape[1]
        o_ref[u * sub:(u + 1) * sub, :] = (ot / l).T.astype(o_ref.dtype)


def _attention(q, ks, vts, tq=1024):
    b, h, t, dk = q.shape
    dv = vts[0].shape[2]
    tk = sum(k.shape[2] for k in ks)
    tq = min(tq, t)
    sub = min(ATTN_SUB, tq)
    whole = lambda a: pl.BlockSpec((None, None) + a.shape[2:], lambda bi, hi, i: (bi, hi, 0, 0))
    return pl.pallas_call(
        _attn_kernel,
        out_shape=jax.ShapeDtypeStruct((b, t, h * dv), BF16),
        grid=(b, h, t // tq),
        in_specs=[pl.BlockSpec((None, None, tq, dk), lambda bi, hi, i: (bi, hi, i, 0))]
                 + [whole(k) for k in ks] + [whole(v) for v in vts],
        out_specs=pl.BlockSpec((None, tq, dv), lambda bi, hi, i: (bi, i, hi)),
        scratch_shapes=[pltpu.VMEM((tq // sub, tk, sub), F32)],
        compiler_params=_cp("arbitrary", "arbitrary", "arbitrary"),
        name="attention",
    )(q, *ks, *vts)


def _sink_attn_kernel(sink_ref, q_ref, k_ref, v_ref, o_ref):
    group = q_ref.shape[0]
    k = k_ref[...]
    v = v_ref[...]
    for g in range(group):
        s = _dot_nt(q_ref[g], k)
        sk = sink_ref[pl.program_id(1) * group + g]
        m = jnp.maximum(jnp.max(s, axis=1, keepdims=True), sk)
        p = jnp.exp(s - m)
        l = jnp.sum(p, axis=1, keepdims=True) + jnp.exp(sk - m)
        o_ref[g] = (_dot(p.astype(BF16), v) / l).astype(o_ref.dtype)


def _sink_attention(q, k, v, sink):
    b, hq, t, d = q.shape
    hk = k.shape[1]
    group = hq // hk
    kv_spec = pl.BlockSpec((None, None, t, d), lambda bi, h: (bi, h, 0, 0))
    q_spec = pl.BlockSpec((None, group, t, d), lambda bi, h: (bi, h, 0, 0))
    return pl.pallas_call(
        _sink_attn_kernel,
        out_shape=jax.ShapeDtypeStruct((b, hq, t, d), BF16),
        grid=(b, hk),
        in_specs=[pl.BlockSpec(memory_space=pltpu.SMEM), q_spec, kv_spec, kv_spec],
        out_specs=q_spec,
        compiler_params=_cp("arbitrary", "arbitrary"),
        name="sink_attention",
    )(sink, q, k, v)


def _window_kernel(sink_ref, bias_ref, qt_ref, kp_ref, kc_ref, kn_ref, kx_ref, vp_ref, vc_ref, vn_ref, vx_ref, o_ref):
    hk = pl.program_id(1)
    kk = jnp.concatenate([kx_ref[...], kp_ref[...], kc_ref[...], kn_ref[...]], axis=0)
    vt = jnp.concatenate([vx_ref[...], vp_ref[...], vc_ref[...], vn_ref[...]], axis=1)
    bias = bias_ref[...]
    s = _dot(kk, qt_ref[...])
    s = jnp.concatenate([s[:, g * Q_BLOCK:(g + 1) * Q_BLOCK] + bias for g in range(WIN_GROUP)], axis=1)
    sk = jnp.concatenate([jnp.full((1, Q_BLOCK), sink_ref[hk * WIN_GROUP + g], F32) for g in range(WIN_GROUP)],
                         axis=1)
    m = jnp.maximum(jnp.max(s, axis=0, keepdims=True), sk)
    e = jnp.exp(s - m)
    l = jnp.sum(e, axis=0, keepdims=True) + jnp.exp(sk - m)
    o_ref[...] = (_dot(vt, e.astype(BF16)) / l).astype(o_ref.dtype)


def _window_bias(n_ctx):
    c = np.arange(n_ctx + 3 * Q_BLOCK)[:, None]
    r = np.arange(Q_BLOCK)[None, :]
    band = (c < n_ctx) | (np.abs(c - n_ctx - WINDOW - r) <= WINDOW)
    in_prev = (c >= n_ctx) & (c < n_ctx + Q_BLOCK)
    in_next = c >= n_ctx + 2 * Q_BLOCK
    ok = np.stack([band & ~in_prev, band, band & ~in_next])
    return jnp.asarray(np.where(ok, 0.0, NEG_BIG).astype(np.float32))


def _window_attention(qkv, qkv_ctx, sink):
    b, s, _ = qkv.shape
    n_ctx = qkv_ctx.shape[1]
    hd, hq, hk, grp = WIN_HEAD_DIM, WIN_Q_HEADS, WIN_KV_HEADS, WIN_GROUP
    qw, kw = hq * hd, hk * hd
    nblk = s // Q_BLOCK
    assert nblk >= 2 and WINDOW == Q_BLOCK
    qt = qkv[..., :qw].reshape(b, nblk, Q_BLOCK, hk, grp, hd).transpose(0, 3, 1, 5, 4, 2)
    qt = qt.reshape(b, hk, nblk, hd, grp * Q_BLOCK)
    k = _to_heads(qkv[..., qw:qw + kw], hk)
    kx = _to_heads(qkv_ctx[..., qw:qw + kw], hk)
    vt = qkv[..., qw + kw:].reshape(b, s, hk, hd).transpose(0, 2, 3, 1)
    vxt = qkv_ctx[..., qw + kw:].reshape(b, n_ctx, hk, hd).transpose(0, 2, 3, 1)

    prev = lambda i: jnp.maximum(i - 1, 0)
    nxt = lambda i: jnp.minimum(i + 1, nblk - 1)
    kblk = lambda f: pl.BlockSpec((None, None, Q_BLOCK, hd), lambda bi, h, i: (bi, h, f(i), 0))
    vblk = lambda f: pl.BlockSpec((None, None, hd, Q_BLOCK), lambda bi, h, i: (bi, h, 0, f(i)))
    same = lambda i: i
    edge = lambda bi, h, i: (jnp.where(i == 0, 0, jnp.where(i == nblk - 1, 2, 1)), 0, 0)
    qo_spec = pl.BlockSpec((None, None, None, hd, grp * Q_BLOCK), lambda bi, h, i: (bi, h, i, 0, 0))
    ot = pl.pallas_call(
        _window_kernel,
        out_shape=jax.ShapeDtypeStruct((b, hk, nblk, hd, grp * Q_BLOCK), BF16),
        grid=(b, hk, nblk),
        in_specs=[pl.BlockSpec(memory_space=pltpu.SMEM),
                  pl.BlockSpec((None, n_ctx + 3 * Q_BLOCK, Q_BLOCK), edge),
                  qo_spec,
                  kblk(prev), kblk(same), kblk(nxt),
                  pl.BlockSpec((None, None, n_ctx, hd), lambda bi, h, i: (bi, h, 0, 0)),
                  vblk(prev), vblk(same), vblk(nxt),
                  pl.BlockSpec((None, None, hd, n_ctx), lambda bi, h, i: (bi, h, 0, 0))],
        out_specs=qo_spec,
        compiler_params=_cp("arbitrary", "arbitrary", "arbitrary"),
        name="window_attention",
    )(sink, _window_bias(n_ctx), qt, k, k, k, kx, vt, vt, vt, vxt)
    o = ot.reshape(b, hk, nblk, hd, grp, Q_BLOCK).transpose(0, 2, 5, 1, 4, 3)
    return o.reshape(b, s, qw)


def _shortconv_kernel(u_ref, w_ref, b_ref, o_ref):
    u = u_ref[...]
    rows = u.shape[0]
    t = lax.broadcasted_iota(jnp.int32, u.shape, 0)
    before = jnp.where(t == 0, 0.0, pltpu.roll(u, 1, 0))
    after = jnp.where(t == rows - 1, 0.0, pltpu.roll(u, rows - 1, 0))
    o_ref[...] = before * w_ref[0:1, :] + u * w_ref[1:2, :] + after * w_ref[2:3, :] + b_ref[...]


def _shortconv(u, w, bias, tc=256):
    b, t, n = u.shape
    return pl.pallas_call(
        _shortconv_kernel,
        out_shape=jax.ShapeDtypeStruct((b, t, n), F32),
        grid=(b, n // tc),
        in_specs=[pl.BlockSpec((None, t, tc), lambda bi, j: (bi, 0, j)),
                  pl.BlockSpec((3, tc), lambda bi, j: (0, j)),
                  pl.BlockSpec((1, tc), lambda bi, j: (0, j))],
        out_specs=pl.BlockSpec((None, t, tc), lambda bi, j: (bi, 0, j)),
        compiler_params=_cp("arbitrary", "arbitrary"),
        name="short_conv",
    )(u, w, bias.reshape(1, n))


def _filter_kernel(z_ref, t_ref, w1_ref, b1_ref, w2_ref, b2_ref, w3_ref, b3_ref, fr_ref, w4_ref, dec_ref, o_ref, *,
                   zero_row):
    def dense(hv, w_ref):
        a_hi, a_lo = _split(hv)
        w_hi, w_lo = _split(w_ref[...])
        return _dot3(a_hi, a_lo, w_hi, w_lo)

    freq = fr_ref[...]
    hv = jnp.sin(freq * (dense(z_ref[...], w1_ref) + b1_ref[...]))
    hv = jnp.sin(freq * (dense(hv, w2_ref) + b2_ref[...]))
    hv = jnp.sin(freq * (dense(hv, w3_ref) + b3_ref[...]))
    out = dense(hv, w4_ref) * jnp.exp(-t_ref[...] * jnp.abs(dec_ref[...]))
    tr = out.shape[0]
    row = pl.program_id(1) * tr + lax.broadcasted_iota(jnp.int32, out.shape, 0)
    o_ref[...] = jnp.where(row == zero_row, 0.0, out)


def _hyena_filter(seq, f_w1, f_b1, f_w2, f_b2, f_w3, f_b3, f_freq, f_w4, decay):
    d = f_w4.shape[1] // (2 * HY_ORDER)
    n = 2 * seq
    t = np.linspace(0.0, 1.0, seq, dtype=np.float32)[:, None]
    w = (2.0 * math.pi * np.arange(seq, dtype=np.float32)[:, None] / seq).astype(np.float32)
    f = np.linspace(1e-4, HY_BANDS - 1, HY_BANDS, dtype=np.float32)[None, :]
    z = np.concatenate([t, np.cos(f * w), -np.sin(f * w)], axis=-1).astype(np.float32)
    lag = np.arange(n)
    lag = np.where(lag < seq, lag, n - lag) % seq
    zc = np.zeros((n, LANES), np.float32)
    zc[:, :HY_EMB] = z[lag]
    tc = t[lag]
    w1 = jnp.zeros((LANES, HY_FILTER_W), F32).at[:HY_EMB].set(f_w1)
    tr = min(512, seq)
    fw = HY_FILTER_W
    row = lambda v: v.reshape(1, -1)
    const = lambda o, i: (0, 0)
    col = lambda o, i: (0, 2 * o + (i * tr) // seq)
    return pl.pallas_call(
        functools.partial(_filter_kernel, zero_row=seq),
        out_shape=jax.ShapeDtypeStruct((HY_ORDER, n, d), F32),
        grid=(HY_ORDER, n // tr),
        in_specs=[pl.BlockSpec((tr, LANES), lambda o, i: (i, 0)),
                  pl.BlockSpec((tr, 1), lambda o, i: (i, 0)),
                  pl.BlockSpec((LANES, fw), const), pl.BlockSpec((1, fw), const),
                  pl.BlockSpec((fw, fw), const), pl.BlockSpec((1, fw), const),
                  pl.BlockSpec((fw, fw), const), pl.BlockSpec((1, fw), const),
                  pl.BlockSpec((1, fw), const),
                  pl.BlockSpec((fw, d), col),
                  pl.BlockSpec((1, d), col)],
        out_specs=pl.BlockSpec((None, tr, d), lambda o, i: (o, i, 0)),
        compiler_params=_cp("arbitrary", "arbitrary"),
        name="hyena_filter",
    )(jnp.asarray(zc), jnp.asarray(tc), w1, row(f_b1), f_w2, row(f_b2), f_w3, row(f_b3), row(f_freq), f_w4,
      row(decay))


def _stack_complex(c):
    return np.block([[c.real, -c.imag], [c.imag, c.real]])


def _hi_lo(m):
    m = jnp.asarray(np.asarray(m, np.float32))
    hi = m.astype(BF16)
    lo = (m - hi.astype(F32)).astype(BF16)
    return hi, lo


def _fft_split(n):
    n2 = 128
    return n // n2, n2


@functools.lru_cache(maxsize=None)
def _fft_matrices_np(n):
    n1, n2 = _fft_split(n)
    h = n1 // 2
    k1 = np.arange(n1)
    bb = np.arange(n2)
    tw = np.exp(-2j * np.pi * np.outer(bb, k1) / n)
    f1 = np.exp(-2j * np.pi * np.outer(k1, np.arange(n1)) / n1)
    la = np.stack([_stack_complex(tw[b][:, None] * f1[:, :h]) for b in range(n2)])
    lf = np.stack([np.concatenate([(tw[b][:, None] * f1).real, (tw[b][:, None] * f1).imag], axis=0)
                   for b in range(n2)])
    f2 = np.exp(-2j * np.pi * np.outer(np.arange(n2), np.arange(n2)) / n2)
    lc = _stack_complex(f2)
    lci = _stack_complex(np.conj(f2))
    f1i = np.exp(2j * np.pi * np.outer(np.arange(h), k1) / n1)
    lai = np.stack([_stack_complex(f1i * np.conj(tw[b])[None, :] / n) for b in range(n2)])
    return la, lf, lc, lci, lai


FFT_UNROLL = 8
MID_UNROLL = 4


def _fwd_strided_stage(read_rows, mat_hi_ref, mat_lo_ref, t_ref, n1, n2):
    def body(b, carry):
        hi, lo = _split(read_rows(b))
        out = _dot3(mat_hi_ref[b], mat_lo_ref[b], hi, lo)
        off = pl.multiple_of(b * n1, n1)
        t_ref[0, pl.ds(off, n1), :] = out[:n1]
        t_ref[1, pl.ds(off, n1), :] = out[n1:]
        return carry
    lax.fori_loop(0, n2, body, 0, unroll=FFT_UNROLL)


def _gather_stacked(ref, start, count, stride):
    return jnp.concatenate([ref[0, pl.ds(start, count, stride=stride), :],
                            ref[1, pl.ds(start, count, stride=stride), :]], axis=0)


def _gather_pair(ref, k1, count, stride):
    return jnp.concatenate([_gather_stacked(ref, k1, count, stride),
                            _gather_stacked(ref, k1 + 1, count, stride)], axis=1)


def _load_pair(ref, plane, off, rows):
    blk = ref[plane, pl.ds(off, 2 * rows), :]
    return jnp.concatenate([blk[:rows], blk[rows:]], axis=1)


def _store_pair(ref, plane, off, rows, val):
    td = val.shape[1] // 2
    ref[plane, pl.ds(off, rows), :] = val[:, :td]
    ref[plane, pl.ds(off + rows, rows), :] = val[:, td:]


def _spectrum_kernel(f_ref, lf_hi, lf_lo, lc_hi, lc_lo, h_ref, t_scr, *, n1, n2):
    _fwd_strided_stage(lambda b: f_ref[pl.ds(b, n1, stride=n2), :], lf_hi, lf_lo, t_scr, n1, n2)

    def body(i, carry):
        hi, lo = _split(_gather_pair(t_scr, 2 * i, n2, n1))
        out = _dot3(lc_hi[...], lc_lo[...], hi, lo)
        off = pl.multiple_of(2 * i * n2, 2 * n2)
        _store_pair(h_ref, 0, off, n2, out[:n2])
        _store_pair(h_ref, 1, off, n2, out[n2:])
        return carry
    lax.fori_loop(0, n1 // 2, body, 0, unroll=MID_UNROLL)


def _spectrum(filt, td=LANES):
    order, n, d = filt.shape
    n1, n2 = _fft_split(n)
    _, lf, lc, _, _ = _fft_matrices_np(n)
    lf_hi, lf_lo = _hi_lo(lf)
    lc_hi, lc_lo = _hi_lo(lc)
    c3 = lambda o, j: (0, 0, 0)
    c2 = lambda o, j: (0, 0)
    return pl.pallas_call(
        functools.partial(_spectrum_kernel, n1=n1, n2=n2),
        out_shape=jax.ShapeDtypeStruct((order, 2, n, d), F32),
        grid=(order, d // td),
        in_specs=[pl.BlockSpec((None, n, td), lambda o, j: (o, 0, j)),
                  _single(lf_hi.shape, c3), _single(lf_lo.shape, c3),
                  _single(lc_hi.shape, c2), _single(lc_lo.shape, c2)],
        out_specs=pl.BlockSpec((None, 2, n, td), lambda o, j: (o, 0, 0, j)),
        scratch_shapes=[pltpu.VMEM((2, n, td), F32)],
        compiler_params=_cp("arbitrary", "arbitrary"),
        name="filter_spectrum",
    )(filt, lf_hi, lf_lo, lc_hi, lc_lo)


def _fftconv_kernel(z_ref, g_ref, h_ref, skip_ref, la_hi, la_lo, lc_hi, lc_lo, lci_hi, lci_lo, lai_hi, lai_lo,
                    o_ref, t_scr, u_scr, *, n1, n2):
    half = n1 // 2
    seq = half * n2
    _fwd_strided_stage(lambda b: _gather_stacked(z_ref, b, half, n2), la_hi, la_lo, t_scr, n1, n2)

    def mid(i, carry):
        hi, lo = _split(_gather_pair(t_scr, 2 * i, n2, n1))
        x = _dot3(lc_hi[...], lc_lo[...], hi, lo)
        off = pl.multiple_of(2 * i * n2, 2 * n2)
        xr, xi = x[:n2], x[n2:]
        hr = _load_pair(h_ref, 0, off, n2)
        hi_ = _load_pair(h_ref, 1, off, n2)
        _store_pair(u_scr, 0, off, n2, xr * hr - xi * hi_)
        _store_pair(u_scr, 1, off, n2, xr * hi_ + xi * hr)
        return carry
    lax.fori_loop(0, n1 // 2, mid, 0, unroll=MID_UNROLL)

    def mid_inv(i, carry):
        off = pl.multiple_of(2 * i * n2, 2 * n2)
        y = jnp.concatenate([_load_pair(u_scr, 0, off, n2), _load_pair(u_scr, 1, off, n2)], axis=0)
        y_hi, y_lo = _split(y)
        w = _dot3(lci_hi[...], lci_lo[...], y_hi, y_lo)
        _store_pair(u_scr, 0, off, n2, w[:n2])
        _store_pair(u_scr, 1, off, n2, w[n2:])
        return carry
    lax.fori_loop(0, n1 // 2, mid_inv, 0, unroll=MID_UNROLL)

    def inv(b, carry):
        hi, lo = _split(_gather_stacked(u_scr, b, n1, n2))
        out = _dot3(lai_hi[b], lai_lo[b], hi, lo)
        off = pl.multiple_of(b * half, half)
        t_scr[0, pl.ds(off, half), :] = out[:half]
        t_scr[1, pl.ds(off, half), :] = out[half:]
        return carry
    lax.fori_loop(0, n2, inv, 0, unroll=FFT_UNROLL)

    skip = skip_ref[...]

    def epi(a, carry):
        off = pl.multiple_of(a * n2, n2)
        for plane in range(2):
            y = t_scr[plane, pl.ds(a, n2, stride=half), :]
            zc = z_ref[plane, pl.ds(off, n2), :]
            gc = g_ref[plane, pl.ds(off, n2), :]
            o_ref[plane, pl.ds(off, n2), :] = (gc * (y + zc * skip)).astype(o_ref.dtype)
        return carry
    lax.fori_loop(0, half, epi, 0)
    del seq


def _fftconv(z_arr, z_col, g_arr, g_col, spec, order, skip, out_dtype, td=LANES):
    b, seq, _ = z_arr.shape
    d = spec.shape[3]
    n = 2 * seq
    n1, n2 = _fft_split(n)
    nblk = d // td
    la, _, lc, lci, lai = _fft_matrices_np(n)
    mats = [*_hi_lo(la), *_hi_lo(lc), *_hi_lo(lci), *_hi_lo(lai)]
    mat_specs = [_single(m.shape, (lambda j, p, nd=m.ndim: (0,) * nd)) for m in mats]
    return pl.pallas_call(
        functools.partial(_fftconv_kernel, n1=n1, n2=n2),
        out_shape=jax.ShapeDtypeStruct((b, seq, d), out_dtype),
        grid=(nblk, b // 2),
        in_specs=[_single((2, seq, td), lambda j, p: (p, 0, z_col * nblk + j)),
                  _single((2, seq, td), lambda j, p: (p, 0, g_col * nblk + j)),
                  _single((None, 2, n, td), lambda j, p: (order, 0, 0, j)),
                  pl.BlockSpec((None, 1, td), lambda j, p: (order, 0, j))] + mat_specs,
        out_specs=_single((2, seq, td), lambda j, p: (p, 0, j)),
        scratch_shapes=[pltpu.VMEM((2, n, td), F32), pltpu.VMEM((2, n, td), F32)],
        compiler_params=_cp("arbitrary", "arbitrary"),
        name="fft_conv",
    )(z_arr, g_arr, spec, skip.reshape(HY_ORDER, 1, d), *mats)


@functools.lru_cache(maxsize=None)
def _dense_dft_np(seq):
    n = 2 * seq
    k = np.arange(n)
    f = np.exp(-2j * np.pi * np.outer(k, k) / n)
    fwd = _stack_complex(f[:, :seq])
    flt = np.concatenate([f.real, f.imag], axis=0)
    inv = _stack_complex(np.conj(f)[:seq, :] / n)
    return fwd, flt, inv


def _dense_spectrum_kernel(f_ref, m_hi, m_lo, h_ref):
    hi, lo = _split(f_ref[...])
    out = _dot3(m_hi[...], m_lo[...], hi, lo)
    n = f_ref.shape[0]
    h_ref[0] = out[:n]
    h_ref[1] = out[n:]


def _dense_spectrum(filt, td=256):
    order, n, d = filt.shape
    _, flt, _ = _dense_dft_np(n // 2)
    m_hi, m_lo = _hi_lo(flt)
    c2 = lambda o, j: (0, 0)
    return pl.pallas_call(
        _dense_spectrum_kernel,
        out_shape=jax.ShapeDtypeStruct((order, 2, n, d), F32),
        grid=(order, d // td),
        in_specs=[pl.BlockSpec((None, n, td), lambda o, j: (o, 0, j)),
                  pl.BlockSpec(m_hi.shape, c2), pl.BlockSpec(m_lo.shape, c2)],
        out_specs=pl.BlockSpec((None, 2, n, td), lambda o, j: (o, 0, 0, j)),
        compiler_params=_cp("arbitrary", "arbitrary"),
        name="filter_spectrum_dense",
    )(filt, m_hi, m_lo)


def _dense_conv_kernel(z_ref, g_ref, h_ref, skip_ref, f_hi, f_lo, i_hi, i_lo, o_ref):
    seq = z_ref.shape[1]
    n = 2 * seq
    zz = jnp.concatenate([z_ref[0], z_ref[1]], axis=0)
    hi, lo = _split(zz)
    x = _dot3(f_hi[...], f_lo[...], hi, lo)
    xr, xi = x[:n], x[n:]
    hr, hi_ = h_ref[0], h_ref[1]
    y = jnp.concatenate([xr * hr - xi * hi_, xr * hi_ + xi * hr], axis=0)
    y_hi, y_lo = _split(y)
    w = _dot3(i_hi[...], i_lo[...], y_hi, y_lo)
    skip = skip_ref[...]
    for plane in range(2):
        y_p = w[plane * seq:(plane + 1) * seq]
        o_ref[plane] = (g_ref[plane] * (y_p + z_ref[plane] * skip)).astype(o_ref.dtype)


def _dense_conv(z_arr, z_col, g_arr, g_col, spec, order, skip, out_dtype, td=256):
    b, seq, _ = z_arr.shape
    d = spec.shape[3]
    n = 2 * seq
    nblk = d // td
    fwd, _, inv = _dense_dft_np(seq)
    mats = [*_hi_lo(fwd), *_hi_lo(inv)]
    c2 = lambda j, p: (0, 0)
    return pl.pallas_call(
        _dense_conv_kernel,
        out_shape=jax.ShapeDtypeStruct((b, seq, d), out_dtype),
        grid=(nblk, b // 2),
        in_specs=[pl.BlockSpec((2, seq, td), lambda j, p: (p, 0, z_col * nblk + j)),
                  pl.BlockSpec((2, seq, td), lambda j, p: (p, 0, g_col * nblk + j)),
                  pl.BlockSpec((None, 2, n, td), lambda j, p: (order, 0, 0, j)),
                  pl.BlockSpec((None, 1, td), lambda j, p: (order, 0, j))]
                 + [pl.BlockSpec(m.shape, c2) for m in mats],
        out_specs=pl.BlockSpec((2, seq, td), lambda j, p: (p, 0, j)),
        compiler_params=_cp("arbitrary", "arbitrary"),
        name="dense_conv",
    )(z_arr, g_arr, spec, skip.reshape(HY_ORDER, 1, d), *mats)


DENSE_DFT_MAX_SEQ = 512


def _hyena(x, gain, shift, scale, p):
    b, seq, d = x.shape
    u = _nm_matmul(x, gain, shift, scale, p["w_in"], F32, tn=1024)
    u = _shortconv(u, p["conv_w"], p["conv_b"])
    filt = _hyena_filter(seq, *p["filter"])
    dense = seq <= DENSE_DFT_MAX_SEQ
    spec = _dense_spectrum(filt) if dense else _spectrum(filt)
    conv = _dense_conv if dense else _fftconv
    z1 = conv(u, 0, u, 1, spec, 0, p["skip"], F32)
    return conv(z1, 0, u, 2, spec, 1, p["skip"], BF16)


def _rope_tables(seq):
    t = np.arange(seq)
    pos = np.stack([t // GRID_W, t % GRID_W], axis=1).astype(np.float32)
    n = 16
    inv = jnp.asarray(ROPE_THETA, F32) ** (-jnp.arange(n, dtype=F32) / n)
    lane = np.arange(LANES)
    axis = (lane % 64) // 32
    idx = lane % 16
    sign = np.where(lane % 32 < 16, -1.0, 1.0).astype(np.float32)
    ang = jnp.asarray(pos)[:, axis] * inv[idx][None, :]
    return jnp.cos(ang), jnp.sin(ang) * sign[None, :]


def _identity_tables(seq):
    return jnp.ones((seq, LANES), F32), jnp.zeros((seq, LANES), F32)


def _mla_weights(w_dq, w_uq, w_dkv, w_ukv):
    d = w_dq.shape[0]
    hh = MLA_HEADS
    wd = jnp.concatenate([w_dq, w_dkv, jnp.zeros((d, LANES - MLA_ROPE), F32)], axis=1).astype(BF16)
    uq = w_uq.reshape(MLA_RANK, hh, MLA_NOPE + MLA_ROPE)
    uq = jnp.pad(uq, ((0, 0), (0, 0), (0, MLA_HEAD_PAD - MLA_NOPE - MLA_ROPE)))
    ukv = w_ukv.reshape(MLA_RANK, hh, MLA_NOPE + MLA_V)
    ukv = jnp.concatenate([ukv[:, :, :MLA_NOPE].reshape(MLA_RANK, hh * MLA_NOPE),
                           ukv[:, :, MLA_NOPE:].reshape(MLA_RANK, hh * MLA_V)], axis=1)
    return wd, uq.reshape(MLA_RANK, hh * MLA_HEAD_PAD).astype(BF16), ukv.astype(BF16)


def _cast_tile_kernel(w_ref, o_ref, *, valid, axis):
    w = w_ref[...]
    idx = pl.program_id(2) * FFN_TILE + lax.broadcasted_iota(jnp.int32, w.shape, axis)
    o_ref[...] = jnp.where(idx < valid, w, 0.0).astype(o_ref.dtype)


def _cast_tile(w, axis):
    depth, two, r, c = w.shape
    valid = w.shape[axis]
    nj = -(-valid // FFN_TILE)
    if axis == 3:
        blk, in_map = (r, FFN_TILE), (lambda l, k, j: (l, k, 0, j))
    else:
        blk, in_map = (FFN_TILE, c), (lambda l, k, j: (l, k, j, 0))
    return pl.pallas_call(
        functools.partial(_cast_tile_kernel, valid=valid, axis=axis - 2),
        out_shape=jax.ShapeDtypeStruct((depth, two, nj) + blk, BF16),
        grid=(depth, two, nj),
        in_specs=[pl.BlockSpec((None, None) + blk, in_map)],
        out_specs=pl.BlockSpec((None, None, None) + blk, lambda l, k, j: (l, k, j, 0, 0)),
        compiler_params=_cp("arbitrary", "arbitrary", "arbitrary"),
        name="cast_tile",
    )(w)


def _to_heads(a, heads):
    b, t, _ = a.shape
    return a.reshape(b, t, heads, -1).transpose(0, 2, 1, 3)


def _from_heads(a):
    b, h, t, e = a.shape
    return a.transpose(0, 2, 1, 3).reshape(b, t, h * e)


def kernel(x, c, ctx, c_ctx, mod_w, mod_b, norm_g, final_g, ffn_wg, ffn_wu, ffn_wd, mla_w_dq, mla_g_q, mla_w_uq, mla_w_dkv, mla_g_kv, mla_w_ukv, mla_w_o, hy_w_in, hy_conv_w, hy_conv_b, hy_f_w1, hy_f_b1, hy_f_w2, hy_f_b2, hy_f_w3, hy_f_b3, hy_f_freq, hy_f_w4, hy_decay, hy_skip, hy_w_out, win_w_qkv, win_sink, win_w_o):
    b, s, d = x.shape
    n_ctx = ctx.shape[1]
    depth = mod_w.shape[0]
    assert b % 2 == 0 and s % 512 == 0 and n_ctx % 256 == 0

    cc = jnp.concatenate([c, c_ctx[None, :], jnp.zeros((8 - (b + 1) % 8, d), F32)], axis=0)
    mods = _modulation(cc, mod_w, mod_b).reshape(depth, cc.shape[0], N_MOD, d)

    ffn_w = (_cast_tile(ffn_wg, 3), _cast_tile(ffn_wu, 3), _cast_tile(ffn_wd, 2))
    cos_l, sin_l = _rope_tables(s)
    cos_c, sin_c = _identity_tables(n_ctx)

    for i in range(depth):
        kind, j = i % N_MIXERS, i // N_MIXERS
        need_ctx = i < depth - 1
        ctx_live = need_ctx or kind != 1
        ml = [mods[i, :b, k][:, None, :] for k in range(N_MOD)]
        mc = [jnp.broadcast_to(mods[i, b, k][None, None, :], (b, 1, d)) for k in range(N_MOD)]

        x = _ffn(x, norm_g[i, 0], ml[0], ml[1], ml[2], *ffn_w, i, 0)
        if ctx_live:
            ctx = _ffn(ctx, norm_g[i, 0], mc[0], mc[1], mc[2], *ffn_w, i, 0)

        g1 = norm_g[i, 1]
        o_c = None
        if kind == 0:
            wd, wuq, wukv = _mla_weights(mla_w_dq[j], mla_w_uq[j], mla_w_dkv[j], mla_w_ukv[j])
            q_l, k_l, v_l = _mla_proj(x, g1, ml[3], ml[4], wd, mla_g_q[j], mla_g_kv[j], wuq, wukv, cos_l, sin_l)
            q_c, k_c, v_c = _mla_proj(ctx, g1, mc[3], mc[4], wd, mla_g_q[j], mla_g_kv[j], wuq, wukv, cos_c, sin_c)
            w_o = mla_w_o[j].astype(BF16)
            o_l = _attention(q_l, [k_c, k_l], [v_c, v_l])
            if need_ctx:
                o_c = _attention(q_c, [k_c], [v_c])
        elif kind == 1:
            hp = dict(w_in=hy_w_in[j].astype(BF16), conv_w=hy_conv_w[j], conv_b=hy_conv_b[j],
                      filter=(hy_f_w1[j], hy_f_b1[j], hy_f_w2[j], hy_f_b2[j], hy_f_w3[j], hy_f_b3[j],
                              hy_f_freq[j], hy_f_w4[j], hy_decay[j]),
                      skip=hy_skip[j])
            w_o = hy_w_out[j].astype(BF16)
            o_l = _hyena(x, g1, ml[3], ml[4], hp)
            if need_ctx:
                o_c = _hyena(ctx, g1, mc[3], mc[4], hp)
        else:
            qw = WIN_Q_HEADS * WIN_HEAD_DIM
            kw = WIN_KV_HEADS * WIN_HEAD_DIM
            w_qkv = win_w_qkv[j].astype(BF16)
            w_o = win_w_o[j].astype(BF16)
            q_chunks = qw // LANES
            rope_l = (cos_l, sin_l, (qw + kw) // LANES, q_chunks, WIN_HEAD_DIM ** -0.5)
            rope_c = (cos_c, sin_c, (qw + kw) // LANES, q_chunks, WIN_HEAD_DIM ** -0.5)
            qkv_l = _nm_matmul(x, g1, ml[3], ml[4], w_qkv, BF16, tn=qw + 2 * kw, rope=rope_l)
            qkv_c = _nm_matmul(ctx, g1, mc[3], mc[4], w_qkv, BF16, tn=qw + 2 * kw, rope=rope_c)
            o_l = _window_attention(qkv_l, qkv_c, win_sink[j])
            if need_ctx:
                q_c = _to_heads(qkv_c[..., :qw], WIN_Q_HEADS)
                k_c = _to_heads(qkv_c[..., qw:qw + kw], WIN_KV_HEADS)
                v_c = _to_heads(qkv_c[..., qw + kw:], WIN_KV_HEADS)
                o_c = _from_heads(_sink_attention(q_c, k_c, v_c, win_sink[j]))

        x = _mm_res(o_l, w_o, x, ml[5])
        x = _ffn(x, norm_g[i, 2], ml[6], ml[7], ml[8], *ffn_w, i, 1)
        if need_ctx:
            ctx = _mm_res(o_c, w_o, ctx, mc[5])
            ctx = _ffn(ctx, norm_g[i, 2], mc[6], mc[7], mc[8], *ffn_w, i, 1)
    return _final_norm(x, final_g)
```

```python
import functools
import math

import numpy as np
import jax
import jax.numpy as jnp
from jax import lax
from jax.experimental import pallas as pl
from jax.experimental.pallas import tpu as pltpu

F32 = jnp.float32
BF16 = jnp.bfloat16

GRID_W = 64
N_MOD = 9
FFN_RES = 0.5
NORM_EPS = 1e-6
ROPE_THETA = 10000.0
N_MIXERS = 3
MLA_HEADS = 16
MLA_NOPE = 128
MLA_ROPE = 64
MLA_V = 128
MLA_RANK = 512
MLA_HEAD_PAD = 256
HY_ORDER = 2
HY_EMB = 33
HY_BANDS = (HY_EMB - 1) // 2
HY_FILTER_W = 64
WIN_HEAD_DIM = 64
WIN_Q_HEADS = 32
WIN_KV_HEADS = 4
WIN_GROUP = WIN_Q_HEADS // WIN_KV_HEADS
WINDOW = 128
Q_BLOCK = 128

LANES = 128
V7X_VMEM_LIMIT = 56 * 1024 * 1024
FFN_TILE = 512
NEG_BIG = -1e30


def _cp(*sem):
    return pltpu.CompilerParams(dimension_semantics=sem, vmem_limit_bytes=V7X_VMEM_LIMIT)


def _single(shape, imap):
    return pl.BlockSpec(shape, imap, pipeline_mode=pl.Buffered(1))


def _dot(a, b):
    return jnp.dot(a, b, preferred_element_type=F32)


def _dot_nt(a, b):
    return lax.dot_general(a, b, (((1,), (1,)), ((), ())), preferred_element_type=F32)


def _split(x):
    hi = x.astype(BF16)
    lo = (x - hi.astype(F32)).astype(BF16)
    return hi, lo


def _dot3(a_hi, a_lo, b_hi, b_lo):
    return _dot(a_hi, b_hi) + (_dot(a_hi, b_lo) + _dot(a_lo, b_hi))


def _modnorm(x, gain, shift, scale):
    r = lax.rsqrt(jnp.mean(x * x, axis=-1, keepdims=True) + NORM_EPS)
    return (x * r * gain) * (1.0 + scale) + shift


def _rms(x, gain):
    return x * lax.rsqrt(jnp.mean(x * x, axis=-1, keepdims=True) + NORM_EPS) * gain


def _rope128(x, cos, sin):
    lane = lax.broadcasted_iota(jnp.int32, x.shape, 1)
    partner = jnp.where(lane % 32 < 16, pltpu.roll(x, LANES - 16, 1), pltpu.roll(x, 16, 1))
    return x * cos + partner * sin


def _mod_kernel(a_ref, w_ref, b_ref, o_ref):
    a = a_ref[...]
    a = a * jax.nn.sigmoid(a)
    a_hi, a_lo = _split(a)
    w_hi, w_lo = _split(w_ref[...])
    o_ref[...] = _dot3(a_hi, a_lo, w_hi, w_lo) + b_ref[...]


def _modulation(cc, mod_w, mod_b):
    depth, d, n = mod_w.shape
    rows = cc.shape[0]
    tn = n // 16
    return pl.pallas_call(
        _mod_kernel,
        out_shape=jax.ShapeDtypeStruct((depth, rows, n), F32),
        grid=(depth, n // tn),
        in_specs=[
            pl.BlockSpec((rows, d), lambda l, j: (0, 0)),
            pl.BlockSpec((None, d, tn), lambda l, j: (l, 0, j)),
            pl.BlockSpec((None, 1, tn), lambda l, j: (l, 0, j)),
        ],
        out_specs=pl.BlockSpec((None, rows, tn), lambda l, j: (l, 0, j)),
        compiler_params=_cp("arbitrary", "arbitrary"),
        name="modulation",
    )(cc, mod_w, mod_b.reshape(depth, 1, n))


def _ffn_kernel(x_ref, g_ref, sh_ref, sc_ref, gt_ref, wg_ref, wu_ref, wd_ref, o_ref, h_scr, acc_scr):
    j = pl.program_id(2)

    @pl.when(j == 0)
    def _():
        h_scr[...] = _modnorm(x_ref[...], g_ref[...], sh_ref[...], sc_ref[...]).astype(BF16)
        acc_scr[...] = jnp.zeros_like(acc_scr)

    h = h_scr[...]
    g = _dot(h, wg_ref[...])
    u = _dot(h, wu_ref[...])
    a = (g * jax.nn.sigmoid(g) * u).astype(BF16)
    acc_scr[...] += _dot(a, wd_ref[...])

    @pl.when(j == pl.num_programs(2) - 1)
    def _():
        o_ref[...] = x_ref[...] + FFN_RES * gt_ref[...] * acc_scr[...]


def _ffn(x, gain, shift, scale, gate, wg, wu, wd, layer, slot, tm=512):
    b, t, d = x.shape
    nj = wg.shape[2]
    tm = min(tm, t)
    vec = pl.BlockSpec((None, 1, d), lambda bi, i, j: (bi, 0, 0))
    return pl.pallas_call(
        _ffn_kernel,
        out_shape=jax.ShapeDtypeStruct((b, t, d), F32),
        grid=(b, t // tm, nj),
        in_specs=[
            pl.BlockSpec((None, tm, d), lambda bi, i, j: (bi, i, 0)),
            pl.BlockSpec((1, d), lambda bi, i, j: (0, 0)),
            vec, vec, vec,
            pl.BlockSpec((None, None, None, d, FFN_TILE), lambda bi, i, j: (layer, slot, j, 0, 0)),
            pl.BlockSpec((None, None, None, d, FFN_TILE), lambda bi, i, j: (layer, slot, j, 0, 0)),
            pl.BlockSpec((None, None, None, FFN_TILE, d), lambda bi, i, j: (layer, slot, j, 0, 0)),
        ],
        out_specs=pl.BlockSpec((None, tm, d), lambda bi, i, j: (bi, i, 0)),
        scratch_shapes=[pltpu.VMEM((tm, d), BF16), pltpu.VMEM((tm, d), F32)],
        compiler_params=_cp("arbitrary", "arbitrary", "arbitrary"),
        name="ffn",
    )(x, gain.reshape(1, d), shift, scale, gate, wg, wu, wd)


def _nm_matmul_kernel(x_ref, g_ref, sh_ref, sc_ref, w_ref, *rest, rope_chunks, scaled_chunks, out_scale):
    if rope_chunks:
        cos_ref, sin_ref, o_ref, h_scr = rest
    else:
        o_ref, h_scr = rest

    @pl.when(pl.program_id(2) == 0)
    def _():
        h_scr[...] = _modnorm(x_ref[...], g_ref[...], sh_ref[...], sc_ref[...]).astype(BF16)

    y = _dot(h_scr[...], w_ref[...])
    if not rope_chunks:
        o_ref[...] = y.astype(o_ref.dtype)
        return
    cos = cos_ref[...]
    sin = sin_ref[...]
    for c in range(y.shape[1] // LANES):
        yc = y[:, c * LANES:(c + 1) * LANES]
        if c < rope_chunks:
            yc = _rope128(yc, cos, sin)
        if c < scaled_chunks:
            yc = yc * out_scale
        o_ref[:, c * LANES:(c + 1) * LANES] = yc.astype(o_ref.dtype)


def _nm_matmul(x, gain, shift, scale, w, out_dtype, tn, tm=512, rope=None):
    b, t, d = x.shape
    n = w.shape[1]
    tm = min(tm, t)
    vec = pl.BlockSpec((None, 1, d), lambda bi, i, j: (bi, 0, 0))
    in_specs = [
        pl.BlockSpec((None, tm, d), lambda bi, i, j: (bi, i, 0)),
        pl.BlockSpec((1, d), lambda bi, i, j: (0, 0)),
        vec, vec,
        pl.BlockSpec((d, tn), lambda bi, i, j: (0, j)),
    ]
    args = [x, gain.reshape(1, d), shift, scale, w]
    rope_chunks = scaled_chunks = 0
    out_scale = 1.0
    if rope is not None:
        cos, sin, rope_chunks, scaled_chunks, out_scale = rope
        in_specs += [pl.BlockSpec((tm, LANES), lambda bi, i, j: (i, 0))] * 2
        args += [cos, sin]
    return pl.pallas_call(
        functools.partial(_nm_matmul_kernel, rope_chunks=rope_chunks, scaled_chunks=scaled_chunks,
                          out_scale=out_scale),
        out_shape=jax.ShapeDtypeStruct((b, t, n), out_dtype),
        grid=(b, t // tm, n // tn),
        in_specs=in_specs,
        out_specs=pl.BlockSpec((None, tm, tn), lambda bi, i, j: (bi, i, j)),
        scratch_shapes=[pltpu.VMEM((tm, d), BF16)],
        compiler_params=_cp("arbitrary", "arbitrary", "arbitrary"),
        name="norm_matmul",
    )(*args)


def _mm_res_kernel(a_ref, w_ref, r_ref, gt_ref, o_ref):
    o_ref[...] = r_ref[...] + gt_ref[...] * _dot(a_ref[...], w_ref[...])


def _mm_res(a, w, res, gate, tm=512, bmajor_rows=None):
    b, t, n = res.shape
    k = w.shape[0]
    tm = bmajor_rows or min(tm, t)
    if bmajor_rows:
        a_spec = pl.BlockSpec((None, tm, k), lambda bi, i: (bi, 0, i))
    else:
        a_spec = pl.BlockSpec((None, tm, k), lambda bi, i: (bi, i, 0))
    return pl.pallas_call(
        _mm_res_kernel,
        out_shape=jax.ShapeDtypeStruct((b, t, n), F32),
        grid=(b, t // tm),
        in_specs=[
            a_spec,
            pl.BlockSpec((k, n), lambda bi, i: (0, 0)),
            pl.BlockSpec((None, tm, n), lambda bi, i: (bi, i, 0)),
            pl.BlockSpec((None, 1, n), lambda bi, i: (bi, 0, 0)),
        ],
        out_specs=pl.BlockSpec((None, tm, n), lambda bi, i: (bi, i, 0)),
        compiler_params=_cp("arbitrary", "arbitrary"),
        name="proj_residual",
    )(a, w, res, gate)


def _final_norm_kernel(x_ref, g_ref, o_ref):
    o_ref[...] = _rms(x_ref[...], g_ref[...])


def _final_norm(x, gain, tm=512):
    b, t, d = x.shape
    return pl.pallas_call(
        _final_norm_kernel,
        out_shape=jax.ShapeDtypeStruct((b, t, d), F32),
        grid=(b, t // tm),
        in_specs=[pl.BlockSpec((None, tm, d), lambda bi, i: (bi, i, 0)),
                  pl.BlockSpec((1, d), lambda bi, i: (0, 0))],
        out_specs=pl.BlockSpec((None, tm, d), lambda bi, i: (bi, i, 0)),
        compiler_params=_cp("arbitrary", "arbitrary"),
        name="final_norm",
    )(x, gain.reshape(1, d))


def _mla_proj_kernel(x_ref, g_ref, sh_ref, sc_ref, wd_ref, gq_ref, gkv_ref, wuq_ref, wukv_ref, cos_ref, sin_ref,
                     q_ref, k_ref, v_ref, *, q_scale):
    h = _modnorm(x_ref[...], g_ref[...], sh_ref[...], sc_ref[...]).astype(BF16)
    a = _dot(h, wd_ref[...])
    cq = _rms(a[:, :MLA_RANK], gq_ref[...]).astype(BF16)
    ckv = _rms(a[:, MLA_RANK:2 * MLA_RANK], gkv_ref[...]).astype(BF16)
    cos = cos_ref[...]
    sin = sin_ref[...]
    kr = _rope128(a[:, 2 * MLA_RANK:], cos, sin).astype(BF16)
    q = _dot(cq, wuq_ref[...])
    kv = _dot(ckv, wukv_ref[...])
    for hd in range(MLA_HEADS):
        base = hd * MLA_HEAD_PAD
        q_ref[hd, :, :LANES] = (q[:, base:base + LANES] * q_scale).astype(BF16)
        q_ref[hd, :, LANES:] = (_rope128(q[:, base + LANES:base + 2 * LANES], cos, sin) * q_scale).astype(BF16)
        k_ref[hd, :, :LANES] = kv[:, hd * MLA_NOPE:(hd + 1) * MLA_NOPE].astype(BF16)
        k_ref[hd, :, LANES:] = kr
        voff = MLA_HEADS * MLA_NOPE + hd * MLA_V
        v_ref[hd] = kv[:, voff:voff + MLA_V].T.astype(BF16)


def _mla_proj(x, gain, shift, scale, wd, gq, gkv, wuq, wukv, cos, sin, tm=256):
    b, t, d = x.shape
    tm = min(tm, t)
    vec = pl.BlockSpec((None, 1, d), lambda bi, i: (bi, 0, 0))
    const = lambda bi, i: (0, 0)
    head_out = lambda w: pl.BlockSpec((None, MLA_HEADS, tm, w), lambda bi, i: (bi, 0, i, 0))
    return pl.pallas_call(
        functools.partial(_mla_proj_kernel, q_scale=(MLA_NOPE + MLA_ROPE) ** -0.5),
        out_shape=[jax.ShapeDtypeStruct((b, MLA_HEADS, t, MLA_HEAD_PAD), BF16),
                   jax.ShapeDtypeStruct((b, MLA_HEADS, t, MLA_HEAD_PAD), BF16),
                   jax.ShapeDtypeStruct((b, MLA_HEADS, MLA_V, t), BF16)],
        grid=(b, t // tm),
        in_specs=[
            pl.BlockSpec((None, tm, d), lambda bi, i: (bi, i, 0)),
            pl.BlockSpec((1, d), const),
            vec, vec,
            _single(wd.shape, const),
            pl.BlockSpec((1, MLA_RANK), const),
            pl.BlockSpec((1, MLA_RANK), const),
            _single(wuq.shape, const),
            _single(wukv.shape, const),
            pl.BlockSpec((tm, LANES), lambda bi, i: (i, 0)),
            pl.BlockSpec((tm, LANES), lambda bi, i: (i, 0)),
        ],
        out_specs=[head_out(MLA_HEAD_PAD), head_out(MLA_HEAD_PAD),
                   pl.BlockSpec((None, MLA_HEADS, MLA_V, tm), lambda bi, i: (bi, 0, 0, i))],
        compiler_params=_cp("arbitrary", "arbitrary"),
        name="mla_proj",
    )(x, gain.reshape(1, d), shift, scale, wd, gq.reshape(1, -1), gkv.reshape(1, -1), wuq, wukv, cos, sin)


ATTN_CHUNK = 256
ATTN_SUB = 256


def _attn_kernel(q_ref, *refs):
    o_ref, s_scr = refs[-2:]
    nseg = (len(refs) - 2) // 2
    k_refs, vt_refs = refs[:nseg], refs[nseg:2 * nseg]
    tq = q_ref.shape[0]
    sub = s_scr.shape[2]
    for u in range(tq // sub):
        q = q_ref[u * sub:(u + 1) * sub, :]
        base = 0
        for k_ref in k_refs:
            for c in range(k_ref.shape[0] // ATTN_CHUNK):
                rows = slice(c * ATTN_CHUNK, (c + 1) * ATTN_CHUNK)
                s_scr[u, base + rows.start:base + rows.stop, :] = _dot_nt(k_ref[rows, :], q)
            base += k_ref.shape[0]
    for u in range(tq // sub):
        s = s_scr[u]
        m = jnp.max(s, axis=0, keepdims=True)
        p = jnp.exp(s - m)
        l = jnp.sum(p, axis=0, keepdims=True)
        pb = p.astype(BF16)
        ot = None
        base = 0
        for vt_ref in vt_refs:
            part = _dot(vt_ref[...], pb[base:base + vt_ref.shape[1], :])
            ot = part if ot is None else ot + part
            base += vt_ref.shape[1]
        o_ref[u * sub:(u + 1) * sub, :] = (ot / l).T.astype(o_ref.dtype)


def _attention(q, ks, vts, tq=1024):
    b, h, t, dk = q.shape
    dv = vts[0].shape[2]
    tk = sum(k.shape[2] for k in ks)
    tq = min(tq, t)
    sub = min(ATTN_SUB, tq)
    whole = lambda a: pl.BlockSpec((None, None) + a.shape[2:], lambda bi, hi, i: (bi, hi, 0, 0))
    return pl.pallas_call(
        _attn_kernel,
        out_shape=jax.ShapeDtypeStruct((b, t, h * dv), BF16),
        grid=(b, h, t // tq),
        in_specs=[pl.BlockSpec((None, None, tq, dk), lambda bi, hi, i: (bi, hi, i, 0))]
                 + [whole(k) for k in ks] + [whole(v) for v in vts],
        out_specs=pl.BlockSpec((None, tq, dv), lambda bi, hi, i: (bi, i, hi)),
        scratch_shapes=[pltpu.VMEM((tq // sub, tk, sub), F32)],
        compiler_params=_cp("arbitrary", "arbitrary", "arbitrary"),
        name="attention",
    )(q, *ks, *vts)


def _sink_attn_kernel(sink_ref, q_ref, k_ref, v_ref, o_ref):
    group = q_ref.shape[0]
    k = k_ref[...]
    v = v_ref[...]
    for g in range(group):
        s = _dot_nt(q_ref[g], k)
        sk = sink_ref[pl.program_id(1) * group + g]
        m = jnp.maximum(jnp.max(s, axis=1, keepdims=True), sk)
        p = jnp.exp(s - m)
        l = jnp.sum(p, axis=1, keepdims=True) + jnp.exp(sk - m)
        o_ref[g] = (_dot(p.astype(BF16), v) / l).astype(o_ref.dtype)


def _sink_attention(q, k, v, sink):
    b, hq, t, d = q.shape
    hk = k.shape[1]
    group = hq // hk
    kv_spec = pl.BlockSpec((None, None, t, d), lambda bi, h: (bi, h, 0, 0))
    q_spec = pl.BlockSpec((None, group, t, d), lambda bi, h: (bi, h, 0, 0))
    return pl.pallas_call(
        _sink_attn_kernel,
        out_shape=jax.ShapeDtypeStruct((b, hq, t, d), BF16),
        grid=(b, hk),
        in_specs=[pl.BlockSpec(memory_space=pltpu.SMEM), q_spec, kv_spec, kv_spec],
        out_specs=q_spec,
        compiler_params=_cp("arbitrary", "arbitrary"),
        name="sink_attention",
    )(sink, q, k, v)


def _window_kernel(sink_ref, bias_ref, qt_ref, kp_ref, kc_ref, kn_ref, kx_ref, vp_ref, vc_ref, vn_ref, vx_ref, o_ref):
    hk = pl.program_id(1)
    kk = jnp.concatenate([kx_ref[...], kp_ref[...], kc_ref[...], kn_ref[...]], axis=0)
    vt = jnp.concatenate([vx_ref[...], vp_ref[...], vc_ref[...], vn_ref[...]], axis=1)
    bias = bias_ref[...]
    s = _dot(kk, qt_ref[...])
    s = jnp.concatenate([s[:, g * Q_BLOCK:(g + 1) * Q_BLOCK] + bias for g in range(WIN_GROUP)], axis=1)
    sk = jnp.concatenate([jnp.full((1, Q_BLOCK), sink_ref[hk * WIN_GROUP + g], F32) for g in range(WIN_GROUP)],
                         axis=1)
    m = jnp.maximum(jnp.max(s, axis=0, keepdims=True), sk)
    e = jnp.exp(s - m)
    l = jnp.sum(e, axis=0, keepdims=True) + jnp.exp(sk - m)
    o_ref[...] = (_dot(vt, e.astype(BF16)) / l).astype(o_ref.dtype)


def _window_bias(n_ctx):
    c = np.arange(n_ctx + 3 * Q_BLOCK)[:, None]
    r = np.arange(Q_BLOCK)[None, :]
    band = (c < n_ctx) | (np.abs(c - n_ctx - WINDOW - r) <= WINDOW)
    in_prev = (c >= n_ctx) & (c < n_ctx + Q_BLOCK)
    in_next = c >= n_ctx + 2 * Q_BLOCK
    ok = np.stack([band & ~in_prev, band, band & ~in_next])
    return jnp.asarray(np.where(ok, 0.0, NEG_BIG).astype(np.float32))


def _window_attention(qkv, qkv_ctx, sink):
    b, s, _ = qkv.shape
    n_ctx = qkv_ctx.shape[1]
    hd, hq, hk, grp = WIN_HEAD_DIM, WIN_Q_HEADS, WIN_KV_HEADS, WIN_GROUP
    qw, kw = hq * hd, hk * hd
    nblk = s // Q_BLOCK
    assert nblk >= 2 and WINDOW == Q_BLOCK
    qt = qkv[..., :qw].reshape(b, nblk, Q_BLOCK, hk, grp, hd).transpose(0, 3, 1, 5, 4, 2)
    qt = qt.reshape(b, hk, nblk, hd, grp * Q_BLOCK)
    k = _to_heads(qkv[..., qw:qw + kw], hk)
    kx = _to_heads(qkv_ctx[..., qw:qw + kw], hk)
    vt = qkv[..., qw + kw:].reshape(b, s, hk, hd).transpose(0, 2, 3, 1)
    vxt = qkv_ctx[..., qw + kw:].reshape(b, n_ctx, hk, hd).transpose(0, 2, 3, 1)

    prev = lambda i: jnp.maximum(i - 1, 0)
    nxt = lambda i: jnp.minimum(i + 1, nblk - 1)
    kblk = lambda f: pl.BlockSpec((None, None, Q_BLOCK, hd), lambda bi, h, i: (bi, h, f(i), 0))
    vblk = lambda f: pl.BlockSpec((None, None, hd, Q_BLOCK), lambda bi, h, i: (bi, h, 0, f(i)))
    same = lambda i: i
    edge = lambda bi, h, i: (jnp.where(i == 0, 0, jnp.where(i == nblk - 1, 2, 1)), 0, 0)
    qo_spec = pl.BlockSpec((None, None, None, hd, grp * Q_BLOCK), lambda bi, h, i: (bi, h, i, 0, 0))
    ot = pl.pallas_call(
        _window_kernel,
        out_shape=jax.ShapeDtypeStruct((b, hk, nblk, hd, grp * Q_BLOCK), BF16),
        grid=(b, hk, nblk),
        in_specs=[pl.BlockSpec(memory_space=pltpu.SMEM),
                  pl.BlockSpec((None, n_ctx + 3 * Q_BLOCK, Q_BLOCK), edge),
                  qo_spec,
                  kblk(prev), kblk(same), kblk(nxt),
                  pl.BlockSpec((None, None, n_ctx, hd), lambda bi, h, i: (bi, h, 0, 0)),
                  vblk(prev), vblk(same), vblk(nxt),
                  pl.BlockSpec((None, None, hd, n_ctx), lambda bi, h, i: (bi, h, 0, 0))],
        out_specs=qo_spec,
        compiler_params=_cp("arbitrary", "arbitrary", "arbitrary"),
        name="window_attention",
    )(sink, _window_bias(n_ctx), qt, k, k, k, kx, vt, vt, vt, vxt)
    o = ot.reshape(b, hk, nblk, hd, grp, Q_BLOCK).transpose(0, 2, 5, 1, 4, 3)
    return o.reshape(b, s, qw)


def _shortconv_kernel(u_ref, w_ref, b_ref, o_ref):
    u = u_ref[...]
    rows = u.shape[0]
    t = lax.broadcasted_iota(jnp.int32, u.shape, 0)
    before = jnp.where(t == 0, 0.0, pltpu.roll(u, 1, 0))
    after = jnp.where(t == rows - 1, 0.0, pltpu.roll(u, rows - 1, 0))
    o_ref[...] = before * w_ref[0:1, :] + u * w_ref[1:2, :] + after * w_ref[2:3, :] + b_ref[...]


def _shortconv(u, w, bias, tc=256):
    b, t, n = u.shape
    return pl.pallas_call(
        _shortconv_kernel,
        out_shape=jax.ShapeDtypeStruct((b, t, n), F32),
        grid=(b, n // tc),
        in_specs=[pl.BlockSpec((None, t, tc), lambda bi, j: (bi, 0, j)),
                  pl.BlockSpec((3, tc), lambda bi, j: (0, j)),
                  pl.BlockSpec((1, tc), lambda bi, j: (0, j))],
        out_specs=pl.BlockSpec((None, t, tc), lambda bi, j: (bi, 0, j)),
        compiler_params=_cp("arbitrary", "arbitrary"),
        name="short_conv",
    )(u, w, bias.reshape(1, n))


def _filter_kernel(z_ref, t_ref, w1_ref, b1_ref, w2_ref, b2_ref, w3_ref, b3_ref, fr_ref, w4_ref, dec_ref, o_ref, *,
                   zero_row):
    def dense(hv, w_ref):
        a_hi, a_lo = _split(hv)
        w_hi, w_lo = _split(w_ref[...])
        return _dot3(a_hi, a_lo, w_hi, w_lo)

    freq = fr_ref[...]
    hv = jnp.sin(freq * (dense(z_ref[...], w1_ref) + b1_ref[...]))
    hv = jnp.sin(freq * (dense(hv, w2_ref) + b2_ref[...]))
    hv = jnp.sin(freq * (dense(hv, w3_ref) + b3_ref[...]))
    out = dense(hv, w4_ref) * jnp.exp(-t_ref[...] * jnp.abs(dec_ref[...]))
    tr = out.shape[0]
    row = pl.program_id(1) * tr + lax.broadcasted_iota(jnp.int32, out.shape, 0)
    o_ref[...] = jnp.where(row == zero_row, 0.0, out)


def _hyena_filter(seq, f_w1, f_b1, f_w2, f_b2, f_w3, f_b3, f_freq, f_w4, decay, bmajor_rows=None):
    d = f_w4.shape[1] // (2 * HY_ORDER)
    n = 2 * seq
    t = np.linspace(0.0, 1.0, seq, dtype=np.float32)[:, None]
    w = (2.0 * math.pi * np.arange(seq, dtype=np.float32)[:, None] / seq).astype(np.float32)
    f = np.linspace(1e-4, HY_BANDS - 1, HY_BANDS, dtype=np.float32)[None, :]
    z = np.concatenate([t, np.cos(f * w), -np.sin(f * w)], axis=-1).astype(np.float32)
    lag = np.arange(n)
    lag = np.where(lag < seq, lag, n - lag) % seq
    zc = np.zeros((n, LANES), np.float32)
    zc[:, :HY_EMB] = z[lag]
    tc = t[lag]
    w1 = jnp.zeros((LANES, HY_FILTER_W), F32).at[:HY_EMB].set(f_w1)
    tr = bmajor_rows or min(512, seq)
    fw = HY_FILTER_W
    if bmajor_rows:
        out_shape = jax.ShapeDtypeStruct((HY_ORDER, tr, (n // tr) * d), F32)
        out_spec = pl.BlockSpec((None, tr, d), lambda o, i: (o, 0, i))
    else:
        out_shape = jax.ShapeDtypeStruct((HY_ORDER, n, d), F32)
        out_spec = pl.BlockSpec((None, tr, d), lambda o, i: (o, i, 0))
    row = lambda v: v.reshape(1, -1)
    const = lambda o, i: (0, 0)
    col = lambda o, i: (0, 2 * o + (i * tr) // seq)
    return pl.pallas_call(
        functools.partial(_filter_kernel, zero_row=seq),
        out_shape=out_shape,
        grid=(HY_ORDER, n // tr),
        in_specs=[pl.BlockSpec((tr, LANES), lambda o, i: (i, 0)),
                  pl.BlockSpec((tr, 1), lambda o, i: (i, 0)),
                  pl.BlockSpec((LANES, fw), const), pl.BlockSpec((1, fw), const),
                  pl.BlockSpec((fw, fw), const), pl.BlockSpec((1, fw), const),
                  pl.BlockSpec((fw, fw), const), pl.BlockSpec((1, fw), const),
                  pl.BlockSpec((1, fw), const),
                  pl.BlockSpec((fw, d), col),
                  pl.BlockSpec((1, d), col)],
        out_specs=out_spec,
        compiler_params=_cp("arbitrary", "arbitrary"),
        name="hyena_filter",
    )(jnp.asarray(zc), jnp.asarray(tc), w1, row(f_b1), f_w2, row(f_b2), f_w3, row(f_b3), row(f_freq), f_w4,
      row(decay))


def _stack_complex(c):
    return np.block([[c.real, -c.imag], [c.imag, c.real]])


def _hi_lo(m):
    m = jnp.asarray(np.asarray(m, np.float32))
    hi = m.astype(BF16)
    lo = (m - hi.astype(F32)).astype(BF16)
    return hi, lo


def _fft_split(n):
    n2 = 128
    return n // n2, n2


@functools.lru_cache(maxsize=None)
def _fft_matrices_np(n):
    n1, n2 = _fft_split(n)
    h = n1 // 2
    k1 = np.arange(n1)
    bb = np.arange(n2)
    tw = np.exp(-2j * np.pi * np.outer(bb, k1) / n)
    f1 = np.exp(-2j * np.pi * np.outer(k1, np.arange(n1)) / n1)
    la = np.stack([_stack_complex(tw[b][:, None] * f1[:, :h]) for b in range(n2)])
    lf = np.stack([np.concatenate([(tw[b][:, None] * f1).real, (tw[b][:, None] * f1).imag], axis=0)
                   for b in range(n2)])
    f2 = np.exp(-2j * np.pi * np.outer(np.arange(n2), np.arange(n2)) / n2)
    lc = _stack_complex(f2)
    lci = _stack_complex(np.conj(f2))
    f1i = np.exp(2j * np.pi * np.outer(np.arange(h), k1) / n1)
    lai = np.stack([_stack_complex(f1i * np.conj(tw[b])[None, :] / n) for b in range(n2)])
    return la, lf, lc, lci, lai


FFT_UNROLL = 8
FFT_LANES = 256
FFT_MID_CHUNK = 512
FFT_INV_COLS = 4


def _shortconv_bmajor_kernel(u_ref, first_ref, last_ref, w_ref, b_ref, o_ref):
    a = pl.program_id(1)
    na = pl.num_programs(1)
    u = u_ref[...]
    rows = u.shape[0]
    t = lax.broadcasted_iota(jnp.int32, u.shape, 0)
    prev_row = jnp.where(a > 0, last_ref[pl.ds(jnp.maximum(a - 1, 0), 1), :], 0.0)
    next_row = jnp.where(a < na - 1, first_ref[pl.ds(jnp.minimum(a + 1, na - 1), 1), :], 0.0)
    before = jnp.where(t == 0, prev_row, pltpu.roll(u, 1, 0))
    after = jnp.where(t == rows - 1, next_row, pltpu.roll(u, rows - 1, 0))
    o_ref[...] = before * w_ref[0:1, :] + u * w_ref[1:2, :] + after * w_ref[2:3, :] + b_ref[...]


def _shortconv_bmajor(u, w, bias, n2, tc=2048):
    b, t, n = u.shape
    half = t // n2
    nj = n // tc
    edge = pl.BlockSpec((None, half, tc), lambda bi, a, j: (bi, 0, j))
    return pl.pallas_call(
        _shortconv_bmajor_kernel,
        out_shape=jax.ShapeDtypeStruct((b, n2, half * n), F32),
        grid=(b, half, nj),
        in_specs=[pl.BlockSpec((None, n2, tc), lambda bi, a, j: (bi, a, j)), edge, edge,
                  pl.BlockSpec((3, tc), lambda bi, a, j: (0, j)),
                  pl.BlockSpec((1, tc), lambda bi, a, j: (0, j))],
        out_specs=pl.BlockSpec((None, n2, tc), lambda bi, a, j: (bi, 0, a * nj + j)),
        compiler_params=_cp("arbitrary", "arbitrary", "arbitrary"),
        name="short_conv_bmajor",
    )(u, u[:, 0::n2, :], u[:, n2 - 1::n2, :], w, bias.reshape(1, n))


def _fft_a_kernel(z_ref, m_hi, m_lo, t_ref, *, packed):
    n2, n1 = t_ref.shape[1], t_ref.shape[2]

    def body(b, carry):
        rows = jnp.concatenate([z_ref[0, b], z_ref[1, b]], axis=0) if packed else z_ref[b]
        hi, lo = _split(rows)
        out = _dot3(m_hi[b], m_lo[b], hi, lo)
        t_ref[0, b] = out[:n1]
        t_ref[1, b] = out[n1:]
        return carry
    lax.fori_loop(0, n2, body, 0, unroll=FFT_UNROLL)


def _fft_a(z, col, width, mats, packed):
    m_hi, m_lo = mats
    n2, rows2, k = m_hi.shape
    n1 = rows2 // 2
    td = FFT_LANES
    nblk = width // td
    lead = z.shape[0]
    if packed:
        z4 = z.reshape(lead, n2, k // 2, -1)
        in_spec = pl.BlockSpec((2, n2, k // 2, td), lambda p, j: (p, 0, 0, col * nblk + j))
        groups = lead // 2
    else:
        z4 = z.reshape(lead, n2, k, -1)
        in_spec = pl.BlockSpec((None, n2, k, td), lambda p, j: (p, 0, 0, j))
        groups = lead
    c3 = lambda p, j: (0, 0, 0)
    return pl.pallas_call(
        functools.partial(_fft_a_kernel, packed=packed),
        out_shape=jax.ShapeDtypeStruct((groups, 2, n2, n1, width), F32),
        grid=(groups, nblk),
        in_specs=[in_spec, _single(m_hi.shape, c3), _single(m_lo.shape, c3)],
        out_specs=_single((None, 2, n2, n1, td), lambda p, j: (p, 0, 0, 0, j)),
        compiler_params=_cp("arbitrary", "arbitrary"),
        name="fft_strided_stage",
    )(z4, m_hi, m_lo)


def _fft_mid_kernel(t_ref, *refs, with_filter):
    if with_filter:
        h_ref, lc_hi, lc_lo, lci_hi, lci_lo, o_ref = refs
    else:
        lc_hi, lc_lo, o_ref = refs
    n2, width = t_ref.shape[1], t_ref.shape[2]
    for c in range(width // FFT_MID_CHUNK):
        lanes = slice(c * FFT_MID_CHUNK, (c + 1) * FFT_MID_CHUNK)
        hi, lo = _split(jnp.concatenate([t_ref[0, :, lanes], t_ref[1, :, lanes]], axis=0))
        x = _dot3(lc_hi[...], lc_lo[...], hi, lo)
        if with_filter:
            xr, xi = x[:n2], x[n2:]
            hr, hi_ = h_ref[0, :, lanes], h_ref[1, :, lanes]
            y_hi, y_lo = _split(jnp.concatenate([xr * hr - xi * hi_, xr * hi_ + xi * hr], axis=0))
            x = _dot3(lci_hi[...], lci_lo[...], y_hi, y_lo)
        o_ref[0, :, lanes] = x[:n2]
        o_ref[1, :, lanes] = x[n2:]


def _fft_mid(t, mats, spec=None, order=0):
    g, _, n2, n1, width = t.shape
    tv = t.reshape(g, 2, n2, n1 * width)
    c2 = lambda k1, p: (0, 0)
    in_specs = [pl.BlockSpec((None, 2, n2, width), lambda k1, p: (p, 0, 0, k1))]
    args = [tv]
    if spec is not None:
        in_specs.append(pl.BlockSpec((None, 2, None, n2, width), lambda k1, p: (order, 0, k1, 0, 0)))
        args.append(spec)
    in_specs += [pl.BlockSpec(m.shape, c2) for m in mats]
    return pl.pallas_call(
        functools.partial(_fft_mid_kernel, with_filter=spec is not None),
        out_shape=jax.ShapeDtypeStruct((g, 2, n1, n2, width), F32),
        grid=(n1, g),
        in_specs=in_specs,
        out_specs=pl.BlockSpec((None, 2, None, n2, width), lambda k1, p: (p, 0, k1, 0, 0)),
        compiler_params=_cp("arbitrary", "arbitrary"),
        name="fft_contiguous_stage",
    )(*args, *mats)


def _fft_inv_kernel(u_ref, z_ref, g_ref, skip_ref, m_hi, m_lo, o_ref):
    cols, half, width = o_ref.shape[1], o_ref.shape[2], o_ref.shape[3]
    b0 = pl.program_id(1) * cols
    skip = skip_ref[...]
    for c in range(cols):
        lanes = slice(c * width, (c + 1) * width)
        hi, lo = _split(jnp.concatenate([u_ref[0, :, lanes], u_ref[1, :, lanes]], axis=0))
        out = _dot3(m_hi[b0 + c], m_lo[b0 + c], hi, lo)
        for plane in range(2):
            y = out[plane * half:(plane + 1) * half]
            o_ref[plane, c] = (g_ref[plane, c] * (y + z_ref[plane, c] * skip)).astype(o_ref.dtype)


def _fft_inv(u, z, z_col, g, g_col, skip, order, mats, out_dtype):
    m_hi, m_lo = mats
    pairs, _, n1, n2, width = u.shape
    half = n1 // 2
    cols = FFT_INV_COLS
    uv = u.reshape(pairs, 2, n1, n2 * width)
    sig = lambda arr, col: (arr.reshape(2 * pairs, n2, half, -1),
                            pl.BlockSpec((2, cols, half, width), lambda p, i: (p, i, 0, col)))
    z4, z_spec = sig(z, z_col)
    g4, g_spec = sig(g, g_col)
    c3 = lambda p, i: (0, 0, 0)
    out = pl.pallas_call(
        _fft_inv_kernel,
        out_shape=jax.ShapeDtypeStruct((2 * pairs, n2, half, width), out_dtype),
        grid=(pairs, n2 // cols),
        in_specs=[pl.BlockSpec((None, 2, n1, cols * width), lambda p, i: (p, 0, 0, i)),
                  z_spec, g_spec,
                  pl.BlockSpec((None, 1, width), lambda p, i: (order, 0, 0)),
                  _single(m_hi.shape, c3), _single(m_lo.shape, c3)],
        out_specs=pl.BlockSpec((2, cols, half, width), lambda p, i: (p, i, 0, 0)),
        compiler_params=_cp("arbitrary", "arbitrary"),
        name="fft_inverse_strided_stage",
    )(uv, z4, g4, skip.reshape(HY_ORDER, 1, width), m_hi, m_lo)
    return out.reshape(2 * pairs, n2, half * width)


def _fft_long_conv(usc, filt_p, skip):
    order, n2, nd = filt_p.shape
    d = skip.shape[1]
    n = n2 * (nd // d)
    la, lf, lc, lci, lai = _fft_matrices_np(n)
    fwd = (*_hi_lo(lc),)
    both = (*_hi_lo(lc), *_hi_lo(lci))
    spec = _fft_mid(_fft_a(filt_p, 0, d, _hi_lo(lf), packed=False), fwd)
    la_m, lai_m = _hi_lo(la), _hi_lo(lai)
    z = usc
    for o in range(order):
        t = _fft_a(z, 0, d, la_m, packed=True)
        u = _fft_mid(t, both, spec, o)
        z = _fft_inv(u, z, 0, usc, o + 1, skip, o, lai_m, F32 if o + 1 < order else BF16)
    return z


@functools.lru_cache(maxsize=None)
def _dense_dft_np(seq):
    n = 2 * seq
    k = np.arange(n)
    f = np.exp(-2j * np.pi * np.outer(k, k) / n)
    fwd = _stack_complex(f[:, :seq])
    flt = np.concatenate([f.real, f.imag], axis=0)
    inv = _stack_complex(np.conj(f)[:seq, :] / n)
    return fwd, flt, inv


def _dense_spectrum_kernel(f_ref, m_hi, m_lo, h_ref):
    hi, lo = _split(f_ref[...])
    out = _dot3(m_hi[...], m_lo[...], hi, lo)
    n = f_ref.shape[0]
    h_ref[0] = out[:n]
    h_ref[1] = out[n:]


def _dense_spectrum(filt, td=256):
    order, n, d = filt.shape
    _, flt, _ = _dense_dft_np(n // 2)
    m_hi, m_lo = _hi_lo(flt)
    c2 = lambda o, j: (0, 0)
    return pl.pallas_call(
        _dense_spectrum_kernel,
        out_shape=jax.ShapeDtypeStruct((order, 2, n, d), F32),
        grid=(order, d // td),
        in_specs=[pl.BlockSpec((None, n, td), lambda o, j: (o, 0, j)),
                  pl.BlockSpec(m_hi.shape, c2), pl.BlockSpec(m_lo.shape, c2)],
        out_specs=pl.BlockSpec((None, 2, n, td), lambda o, j: (o, 0, 0, j)),
        compiler_params=_cp("arbitrary", "arbitrary"),
        name="filter_spectrum_dense",
    )(filt, m_hi, m_lo)


def _dense_conv_kernel(z_ref, g_ref, h_ref, skip_ref, f_hi, f_lo, i_hi, i_lo, o_ref):
    seq = z_ref.shape[1]
    n = 2 * seq
    zz = jnp.concatenate([z_ref[0], z_ref[1]], axis=0)
    hi, lo = _split(zz)
    x = _dot3(f_hi[...], f_lo[...], hi, lo)
    xr, xi = x[:n], x[n:]
    hr, hi_ = h_ref[0], h_ref[1]
    y = jnp.concatenate([xr * hr - xi * hi_, xr * hi_ + xi * hr], axis=0)
    y_hi, y_lo = _split(y)
    w = _dot3(i_hi[...], i_lo[...], y_hi, y_lo)
    skip = skip_ref[...]
    for plane in range(2):
        y_p = w[plane * seq:(plane + 1) * seq]
        o_ref[plane] = (g_ref[plane] * (y_p + z_ref[plane] * skip)).astype(o_ref.dtype)


def _dense_conv(z_arr, z_col, g_arr, g_col, spec, order, skip, out_dtype, td=256):
    b, seq, _ = z_arr.shape
    d = spec.shape[3]
    n = 2 * seq
    nblk = d // td
    fwd, _, inv = _dense_dft_np(seq)
    mats = [*_hi_lo(fwd), *_hi_lo(inv)]
    c2 = lambda j, p: (0, 0)
    return pl.pallas_call(
        _dense_conv_kernel,
        out_shape=jax.ShapeDtypeStruct((b, seq, d), out_dtype),
        grid=(nblk, b // 2),
        in_specs=[pl.BlockSpec((2, seq, td), lambda j, p: (p, 0, z_col * nblk + j)),
                  pl.BlockSpec((2, seq, td), lambda j, p: (p, 0, g_col * nblk + j)),
                  pl.BlockSpec((None, 2, n, td), lambda j, p: (order, 0, 0, j)),
                  pl.BlockSpec((None, 1, td), lambda j, p: (order, 0, j))]
                 + [pl.BlockSpec(m.shape, c2) for m in mats],
        out_specs=pl.BlockSpec((2, seq, td), lambda j, p: (p, 0, j)),
        compiler_params=_cp("arbitrary", "arbitrary"),
        name="dense_conv",
    )(z_arr, g_arr, spec, skip.reshape(HY_ORDER, 1, d), *mats)


DENSE_DFT_MAX_SEQ = 512


def _hyena(x, gain, shift, scale, p):
    b, seq, d = x.shape
    u = _nm_matmul(x, gain, shift, scale, p["w_in"], F32, tn=1024)
    if seq <= DENSE_DFT_MAX_SEQ:
        u = _shortconv(u, p["conv_w"], p["conv_b"])
        spec = _dense_spectrum(_hyena_filter(seq, *p["filter"]))
        z1 = _dense_conv(u, 0, u, 1, spec, 0, p["skip"], F32)
        return _dense_conv(z1, 0, u, 2, spec, 1, p["skip"], BF16), False
    _, n2 = _fft_split(2 * seq)
    usc = _shortconv_bmajor(u, p["conv_w"], p["conv_b"], n2)
    filt_p = _hyena_filter(seq, *p["filter"], bmajor_rows=n2)
    return _fft_long_conv(usc, filt_p, p["skip"]), True


def _rope_tables(seq):
    t = np.arange(seq)
    pos = np.stack([t // GRID_W, t % GRID_W], axis=1).astype(np.float32)
    n = 16
    inv = jnp.asarray(ROPE_THETA, F32) ** (-jnp.arange(n, dtype=F32) / n)
    lane = np.arange(LANES)
    axis = (lane % 64) // 32
    idx = lane % 16
    sign = np.where(lane % 32 < 16, -1.0, 1.0).astype(np.float32)
    ang = jnp.asarray(pos)[:, axis] * inv[idx][None, :]
    return jnp.cos(ang), jnp.sin(ang) * sign[None, :]


def _identity_tables(seq):
    return jnp.ones((seq, LANES), F32), jnp.zeros((seq, LANES), F32)


def _mla_weights(w_dq, w_uq, w_dkv, w_ukv):
    d = w_dq.shape[0]
    hh = MLA_HEADS
    wd = jnp.concatenate([w_dq, w_dkv, jnp.zeros((d, LANES - MLA_ROPE), F32)], axis=1).astype(BF16)
    uq = w_uq.reshape(MLA_RANK, hh, MLA_NOPE + MLA_ROPE)
    uq = jnp.pad(uq, ((0, 0), (0, 0), (0, MLA_HEAD_PAD - MLA_NOPE - MLA_ROPE)))
    ukv = w_ukv.reshape(MLA_RANK, hh, MLA_NOPE + MLA_V)
    ukv = jnp.concatenate([ukv[:, :, :MLA_NOPE].reshape(MLA_RANK, hh * MLA_NOPE),
                           ukv[:, :, MLA_NOPE:].reshape(MLA_RANK, hh * MLA_V)], axis=1)
    return wd, uq.reshape(MLA_RANK, hh * MLA_HEAD_PAD).astype(BF16), ukv.astype(BF16)


def _cast_tile_kernel(w_ref, o_ref, *, valid, axis):
    w = w_ref[...]
    idx = pl.program_id(2) * FFN_TILE + lax.broadcasted_iota(jnp.int32, w.shape, axis)
    o_ref[...] = jnp.where(idx < valid, w, 0.0).astype(o_ref.dtype)


def _cast_tile(w, axis):
    depth, two, r, c = w.shape
    valid = w.shape[axis]
    nj = -(-valid // FFN_TILE)
    if axis == 3:
        blk, in_map = (r, FFN_TILE), (lambda l, k, j: (l, k, 0, j))
    else:
        blk, in_map = (FFN_TILE, c), (lambda l, k, j: (l, k, j, 0))
    return pl.pallas_call(
        functools.partial(_cast_tile_kernel, valid=valid, axis=axis - 2),
        out_shape=jax.ShapeDtypeStruct((depth, two, nj) + blk, BF16),
        grid=(depth, two, nj),
        in_specs=[pl.BlockSpec((None, None) + blk, in_map)],
        out_specs=pl.BlockSpec((None, None, None) + blk, lambda l, k, j: (l, k, j, 0, 0)),
        compiler_params=_cp("arbitrary", "arbitrary", "arbitrary"),
        name="cast_tile",
    )(w)


def _to_heads(a, heads):
    b, t, _ = a.shape
    return a.reshape(b, t, heads, -1).transpose(0, 2, 1, 3)


def _from_heads(a):
    b, h, t, e = a.shape
    return a.transpose(0, 2, 1, 3).reshape(b, t, h * e)


def kernel(x, c, ctx, c_ctx, mod_w, mod_b, norm_g, final_g, ffn_wg, ffn_wu, ffn_wd, mla_w_dq, mla_g_q, mla_w_uq, mla_w_dkv, mla_g_kv, mla_w_ukv, mla_w_o, hy_w_in, hy_conv_w, hy_conv_b, hy_f_w1, hy_f_b1, hy_f_w2, hy_f_b2, hy_f_w3, hy_f_b3, hy_f_freq, hy_f_w4, hy_decay, hy_skip, hy_w_out, win_w_qkv, win_sink, win_w_o):
    b, s, d = x.shape
    n_ctx = ctx.shape[1]
    depth = mod_w.shape[0]
    assert b % 2 == 0 and s % 512 == 0 and n_ctx % 256 == 0

    cc = jnp.concatenate([c, c_ctx[None, :], jnp.zeros((8 - (b + 1) % 8, d), F32)], axis=0)
    mods = _modulation(cc, mod_w, mod_b).reshape(depth, cc.shape[0], N_MOD, d)

    ffn_w = (_cast_tile(ffn_wg, 3), _cast_tile(ffn_wu, 3), _cast_tile(ffn_wd, 2))
    cos_l, sin_l = _rope_tables(s)
    cos_c, sin_c = _identity_tables(n_ctx)

    for i in range(depth):
        kind, j = i % N_MIXERS, i // N_MIXERS
        need_ctx = i < depth - 1
        ctx_live = need_ctx or kind != 1
        ml = [mods[i, :b, k][:, None, :] for k in range(N_MOD)]
        mc = [jnp.broadcast_to(mods[i, b, k][None, None, :], (b, 1, d)) for k in range(N_MOD)]

        x = _ffn(x, norm_g[i, 0], ml[0], ml[1], ml[2], *ffn_w, i, 0)
        if ctx_live:
            ctx = _ffn(ctx, norm_g[i, 0], mc[0], mc[1], mc[2], *ffn_w, i, 0)

        g1 = norm_g[i, 1]
        o_c = None
        if kind == 0:
            wd, wuq, wukv = _mla_weights(mla_w_dq[j], mla_w_uq[j], mla_w_dkv[j], mla_w_ukv[j])
            q_l, k_l, v_l = _mla_proj(x, g1, ml[3], ml[4], wd, mla_g_q[j], mla_g_kv[j], wuq, wukv, cos_l, sin_l)
            q_c, k_c, v_c = _mla_proj(ctx, g1, mc[3], mc[4], wd, mla_g_q[j], mla_g_kv[j], wuq, wukv, cos_c, sin_c)
            w_o = mla_w_o[j].astype(BF16)
            o_l = _attention(q_l, [k_c, k_l], [v_c, v_l])
            if need_ctx:
                o_c = _attention(q_c, [k_c], [v_c])
        elif kind == 1:
            hp = dict(w_in=hy_w_in[j].astype(BF16), conv_w=hy_conv_w[j], conv_b=hy_conv_b[j],
                      filter=(hy_f_w1[j], hy_f_b1[j], hy_f_w2[j], hy_f_b2[j], hy_f_w3[j], hy_f_b3[j],
                              hy_f_freq[j], hy_f_w4[j], hy_decay[j]),
                      skip=hy_skip[j])
            w_o = hy_w_out[j].astype(BF16)
            o_l, o_l_bmajor = _hyena(x, g1, ml[3], ml[4], hp)
            if need_ctx:
                o_c, _ = _hyena(ctx, g1, mc[3], mc[4], hp)
        else:
            qw = WIN_Q_HEADS * WIN_HEAD_DIM
            kw = WIN_KV_HEADS * WIN_HEAD_DIM
            w_qkv = win_w_qkv[j].astype(BF16)
            w_o = win_w_o[j].astype(BF16)
            q_chunks = qw // LANES
            rope_l = (cos_l, sin_l, (qw + kw) // LANES, q_chunks, WIN_HEAD_DIM ** -0.5)
            rope_c = (cos_c, sin_c, (qw + kw) // LANES, q_chunks, WIN_HEAD_DIM ** -0.5)
            qkv_l = _nm_matmul(x, g1, ml[3], ml[4], w_qkv, BF16, tn=qw + 2 * kw, rope=rope_l)
            qkv_c = _nm_matmul(ctx, g1, mc[3], mc[4], w_qkv, BF16, tn=qw + 2 * kw, rope=rope_c)
            o_l = _window_attention(qkv_l, qkv_c, win_sink[j])
            if need_ctx:
                q_c = _to_heads(qkv_c[..., :qw], WIN_Q_HEADS)
                k_c = _to_heads(qkv_c[..., qw:qw + kw], WIN_KV_HEADS)
                v_c = _to_heads(qkv_c[..., qw + kw:], WIN_KV_HEADS)
                o_c = _from_heads(_sink_attention(q_c, k_c, v_c, win_sink[j]))

        n2_rows = _fft_split(2 * s)[1] if (kind == 1 and o_l_bmajor) else None
        x = _mm_res(o_l, w_o, x, ml[5], bmajor_rows=n2_rows)
        x = _ffn(x, norm_g[i, 2], ml[6], ml[7], ml[8], *ffn_w, i, 1)
        if need_ctx:
            ctx = _mm_res(o_c, w_o, ctx, mc[5])
            ctx = _ffn(ctx, norm_g[i, 2], mc[6], mc[7], mc[8], *ffn_w, i, 1)
    return _final_norm(x, final_g)
```

```python
import functools
import math

import numpy as np
import jax
import jax.numpy as jnp
from jax import lax
from jax.experimental import pallas as pl
from jax.experimental.pallas import tpu as pltpu

F32 = jnp.float32
BF16 = jnp.bfloat16

GRID_W = 64
N_MOD = 9
FFN_RES = 0.5
NORM_EPS = 1e-6
ROPE_THETA = 10000.0
N_MIXERS = 3
MLA_HEADS = 16
MLA_NOPE = 128
MLA_ROPE = 64
MLA_V = 128
MLA_RANK = 512
MLA_HEAD_PAD = 256
HY_ORDER = 2
HY_EMB = 33
HY_BANDS = (HY_EMB - 1) // 2
HY_FILTER_W = 64
WIN_HEAD_DIM = 64
WIN_Q_HEADS = 32
WIN_KV_HEADS = 4
WIN_GROUP = WIN_Q_HEADS // WIN_KV_HEADS
WINDOW = 128
Q_BLOCK = 128

LANES = 128
V7X_VMEM_LIMIT = 56 * 1024 * 1024
FFN_TILE = 512
NEG_BIG = -1e30


def _cp(*sem):
    return pltpu.CompilerParams(dimension_semantics=sem, vmem_limit_bytes=V7X_VMEM_LIMIT)


def _single(shape, imap):
    return pl.BlockSpec(shape, imap, pipeline_mode=pl.Buffered(1))


def _dot(a, b):
    return jnp.dot(a, b, preferred_element_type=F32)


def _dot_nt(a, b):
    return lax.dot_general(a, b, (((1,), (1,)), ((), ())), preferred_element_type=F32)


def _split(x):
    hi = x.astype(BF16)
    lo = (x - hi.astype(F32)).astype(BF16)
    return hi, lo


def _dot3(a_hi, a_lo, b_hi, b_lo):
    return _dot(a_hi, b_hi) + (_dot(a_hi, b_lo) + _dot(a_lo, b_hi))


def _modnorm(x, gain, shift, scale):
    r = lax.rsqrt(jnp.mean(x * x, axis=-1, keepdims=True) + NORM_EPS)
    return (x * r * gain) * (1.0 + scale) + shift


def _rms(x, gain):
    return x * lax.rsqrt(jnp.mean(x * x, axis=-1, keepdims=True) + NORM_EPS) * gain


def _rope128(x, cos, sin):
    lane = lax.broadcasted_iota(jnp.int32, x.shape, 1)
    partner = jnp.where(lane % 32 < 16, pltpu.roll(x, LANES - 16, 1), pltpu.roll(x, 16, 1))
    return x * cos + partner * sin


def _mod_kernel(a_ref, w_ref, b_ref, o_ref):
    a = a_ref[...]
    a = a * jax.nn.sigmoid(a)
    a_hi, a_lo = _split(a)
    w_hi, w_lo = _split(w_ref[...])
    o_ref[...] = _dot3(a_hi, a_lo, w_hi, w_lo) + b_ref[...]


def _modulation(cc, mod_w, mod_b):
    depth, d, n = mod_w.shape
    rows = cc.shape[0]
    tn = n // 16
    return pl.pallas_call(
        _mod_kernel,
        out_shape=jax.ShapeDtypeStruct((depth, rows, n), F32),
        grid=(depth, n // tn),
        in_specs=[
            pl.BlockSpec((rows, d), lambda l, j: (0, 0)),
            pl.BlockSpec((None, d, tn), lambda l, j: (l, 0, j)),
            pl.BlockSpec((None, 1, tn), lambda l, j: (l, 0, j)),
        ],
        out_specs=pl.BlockSpec((None, rows, tn), lambda l, j: (l, 0, j)),
        compiler_params=_cp("arbitrary", "arbitrary"),
        name="modulation",
    )(cc, mod_w, mod_b.reshape(depth, 1, n))


def _ffn_kernel(x_ref, g_ref, sh_ref, sc_ref, gt_ref, wg_ref, wu_ref, wd_ref, o_ref, h_scr, acc_scr):
    j = pl.program_id(2)

    @pl.when(j == 0)
    def _():
        h_scr[...] = _modnorm(x_ref[...], g_ref[...], sh_ref[...], sc_ref[...]).astype(BF16)
        acc_scr[...] = jnp.zeros_like(acc_scr)

    h = h_scr[...]
    g = _dot(h, wg_ref[...])
    u = _dot(h, wu_ref[...])
    a = (g * jax.nn.sigmoid(g) * u).astype(BF16)
    acc_scr[...] += _dot(a, wd_ref[...])

    @pl.when(j == pl.num_programs(2) - 1)
    def _():
        o_ref[...] = x_ref[...] + FFN_RES * gt_ref[...] * acc_scr[...]


def _ffn(x, gain, shift, scale, gate, wg, wu, wd, layer, slot, tm=512):
    b, t, d = x.shape
    nj = wg.shape[2]
    tm = min(tm, t)
    vec = pl.BlockSpec((None, 1, d), lambda bi, i, j: (bi, 0, 0))
    return pl.pallas_call(
        _ffn_kernel,
        out_shape=jax.ShapeDtypeStruct((b, t, d), F32),
        grid=(b, t // tm, nj),
        in_specs=[
            pl.BlockSpec((None, tm, d), lambda bi, i, j: (bi, i, 0)),
            pl.BlockSpec((1, d), lambda bi, i, j: (0, 0)),
            vec, vec, vec,
            pl.BlockSpec((None, None, None, d, FFN_TILE), lambda bi, i, j: (layer, slot, j, 0, 0)),
            pl.BlockSpec((None, None, None, d, FFN_TILE), lambda bi, i, j: (layer, slot, j, 0, 0)),
            pl.BlockSpec((None, None, None, FFN_TILE, d), lambda bi, i, j: (layer, slot, j, 0, 0)),
        ],
        out_specs=pl.BlockSpec((None, tm, d), lambda bi, i, j: (bi, i, 0)),
        scratch_shapes=[pltpu.VMEM((tm, d), BF16), pltpu.VMEM((tm, d), F32)],
        compiler_params=_cp("arbitrary", "arbitrary", "arbitrary"),
        name="ffn",
    )(x, gain.reshape(1, d), shift, scale, gate, wg, wu, wd)


def _nm_matmul_kernel(x_ref, g_ref, sh_ref, sc_ref, w_ref, *rest, rope_chunks, scaled_chunks, out_scale):
    if rope_chunks:
        cos_ref, sin_ref, o_ref, h_scr = rest
    else:
        o_ref, h_scr = rest

    @pl.when(pl.program_id(2) == 0)
    def _():
        h_scr[...] = _modnorm(x_ref[...], g_ref[...], sh_ref[...], sc_ref[...]).astype(BF16)

    y = _dot(h_scr[...], w_ref[...])
    if not rope_chunks:
        o_ref[...] = y.astype(o_ref.dtype)
        return
    cos = cos_ref[...]
    sin = sin_ref[...]
    for c in range(y.shape[1] // LANES):
        yc = y[:, c * LANES:(c + 1) * LANES]
        if c < rope_chunks:
            yc = _rope128(yc, cos, sin)
        if c < scaled_chunks:
            yc = yc * out_scale
        o_ref[:, c * LANES:(c + 1) * LANES] = yc.astype(o_ref.dtype)


def _nm_matmul(x, gain, shift, scale, w, out_dtype, tn, tm=512, rope=None):
    b, t, d = x.shape
    n = w.shape[1]
    tm = min(tm, t)
    vec = pl.BlockSpec((None, 1, d), lambda bi, i, j: (bi, 0, 0))
    in_specs = [
        pl.BlockSpec((None, tm, d), lambda bi, i, j: (bi, i, 0)),
        pl.BlockSpec((1, d), lambda bi, i, j: (0, 0)),
        vec, vec,
        pl.BlockSpec((d, tn), lambda bi, i, j: (0, j)),
    ]
    args = [x, gain.reshape(1, d), shift, scale, w]
    rope_chunks = scaled_chunks = 0
    out_scale = 1.0
    if rope is not None:
        cos, sin, rope_chunks, scaled_chunks, out_scale = rope
        in_specs += [pl.BlockSpec((tm, LANES), lambda bi, i, j: (i, 0))] * 2
        args += [cos, sin]
    return pl.pallas_call(
        functools.partial(_nm_matmul_kernel, rope_chunks=rope_chunks, scaled_chunks=scaled_chunks,
                          out_scale=out_scale),
        out_shape=jax.ShapeDtypeStruct((b, t, n), out_dtype),
        grid=(b, t // tm, n // tn),
        in_specs=in_specs,
        out_specs=pl.BlockSpec((None, tm, tn), lambda bi, i, j: (bi, i, j)),
        scratch_shapes=[pltpu.VMEM((tm, d), BF16)],
        compiler_params=_cp("arbitrary", "arbitrary", "arbitrary"),
        name="norm_matmul",
    )(*args)


def _mm_res_kernel(a_ref, w_ref, r_ref, gt_ref, o_ref):
    o_ref[...] = r_ref[...] + gt_ref[...] * _dot(a_ref[...], w_ref[...])


def _mm_res(a, w, res, gate, tm=512):
    b, t, k = a.shape
    n = w.shape[1]
    tm = min(tm, t)
    return pl.pallas_call(
        _mm_res_kernel,
        out_shape=jax.ShapeDtypeStruct((b, t, n), F32),
        grid=(b, t // tm),
        in_specs=[
            pl.BlockSpec((None, tm, k), lambda bi, i: (bi, i, 0)),
            pl.BlockSpec((k, n), lambda bi, i: (0, 0)),
            pl.BlockSpec((None, tm, n), lambda bi, i: (bi, i, 0)),
            pl.BlockSpec((None, 1, n), lambda bi, i: (bi, 0, 0)),
        ],
        out_specs=pl.BlockSpec((None, tm, n), lambda bi, i: (bi, i, 0)),
        compiler_params=_cp("arbitrary", "arbitrary"),
        name="proj_residual",
    )(a, w, res, gate)


def _final_norm_kernel(x_ref, g_ref, o_ref):
    o_ref[...] = _rms(x_ref[...], g_ref[...])


def _final_norm(x, gain, tm=512):
    b, t, d = x.shape
    return pl.pallas_call(
        _final_norm_kernel,
        out_shape=jax.ShapeDtypeStruct((b, t, d), F32),
        grid=(b, t // tm),
        in_specs=[pl.BlockSpec((None, tm, d), lambda bi, i: (bi, i, 0)),
                  pl.BlockSpec((1, d), lambda bi, i: (0, 0))],
        out_specs=pl.BlockSpec((None, tm, d), lambda bi, i: (bi, i, 0)),
        compiler_params=_cp("arbitrary", "arbitrary"),
        name="final_norm",
    )(x, gain.reshape(1, d))


def _mla_proj_kernel(x_ref, g_ref, sh_ref, sc_ref, wd_ref, gq_ref, gkv_ref, wuq_ref, wukv_ref, cos_ref, sin_ref,
                     q_ref, k_ref, v_ref, *, q_scale):
    h = _modnorm(x_ref[...], g_ref[...], sh_ref[...], sc_ref[...]).astype(BF16)
    a = _dot(h, wd_ref[...])
    cq = _rms(a[:, :MLA_RANK], gq_ref[...]).astype(BF16)
    ckv = _rms(a[:, MLA_RANK:2 * MLA_RANK], gkv_ref[...]).astype(BF16)
    cos = cos_ref[...]
    sin = sin_ref[...]
    kr = _rope128(a[:, 2 * MLA_RANK:], cos, sin).astype(BF16)
    q = _dot(cq, wuq_ref[...])
    kv = _dot(ckv, wukv_ref[...])
    for hd in range(MLA_HEADS):
        base = hd * MLA_HEAD_PAD
        q_ref[hd, :, :LANES] = (q[:, base:base + LANES] * q_scale).astype(BF16)
        q_ref[hd, :, LANES:] = (_rope128(q[:, base + LANES:base + 2 * LANES], cos, sin) * q_scale).astype(BF16)
        k_ref[hd, :, :LANES] = kv[:, hd * MLA_NOPE:(hd + 1) * MLA_NOPE].astype(BF16)
        k_ref[hd, :, LANES:] = kr
        voff = MLA_HEADS * MLA_NOPE + hd * MLA_V
        v_ref[hd] = kv[:, voff:voff + MLA_V].T.astype(BF16)


def _mla_proj(x, gain, shift, scale, wd, gq, gkv, wuq, wukv, cos, sin, tm=256):
    b, t, d = x.shape
    tm = min(tm, t)
    vec = pl.BlockSpec((None, 1, d), lambda bi, i: (bi, 0, 0))
    const = lambda bi, i: (0, 0)
    head_out = lambda w: pl.BlockSpec((None, MLA_HEADS, tm, w), lambda bi, i: (bi, 0, i, 0))
    return pl.pallas_call(
        functools.partial(_mla_proj_kernel, q_scale=(MLA_NOPE + MLA_ROPE) ** -0.5),
        out_shape=[jax.ShapeDtypeStruct((b, MLA_HEADS, t, MLA_HEAD_PAD), BF16),
                   jax.ShapeDtypeStruct((b, MLA_HEADS, t, MLA_HEAD_PAD), BF16),
                   jax.ShapeDtypeStruct((b, MLA_HEADS, MLA_V, t), BF16)],
        grid=(b, t // tm),
        in_specs=[
            pl.BlockSpec((None, tm, d), lambda bi, i: (bi, i, 0)),
            pl.BlockSpec((1, d), const),
            vec, vec,
            _single(wd.shape, const),
            pl.BlockSpec((1, MLA_RANK), const),
            pl.BlockSpec((1, MLA_RANK), const),
            _single(wuq.shape, const),
            _single(wukv.shape, const),
            pl.BlockSpec((tm, LANES), lambda bi, i: (i, 0)),
            pl.BlockSpec((tm, LANES), lambda bi, i: (i, 0)),
        ],
        out_specs=[head_out(MLA_HEAD_PAD), head_out(MLA_HEAD_PAD),
                   pl.BlockSpec((None, MLA_HEADS, MLA_V, tm), lambda bi, i: (bi, 0, 0, i))],
        compiler_params=_cp("arbitrary", "arbitrary"),
        name="mla_proj",
    )(x, gain.reshape(1, d), shift, scale, wd, gq.reshape(1, -1), gkv.reshape(1, -1), wuq, wukv, cos, sin)


ATTN_CHUNK = 256
ATTN_SUB = 256


def _attn_kernel(q_ref, *refs):
    o_ref, s_scr = refs[-2:]
    nseg = (len(refs) - 2) // 2
    k_refs, vt_refs = refs[:nseg], refs[nseg:2 * nseg]
    tq = q_ref.shape[0]
    sub = s_scr.shape[2]
    for u in range(tq // sub):
        q = q_ref[u * sub:(u + 1) * sub, :]
        base = 0
        for k_ref in k_refs:
            for c in range(k_ref.shape[0] // ATTN_CHUNK):
                rows = slice(c * ATTN_CHUNK, (c + 1) * ATTN_CHUNK)
                s_scr[u, base + rows.start:base + rows.stop, :] = _dot_nt(k_ref[rows, :], q)
            base += k_ref.shape[0]
    for u in range(tq // sub):
        s = s_scr[u]
        m = jnp.max(s, axis=0, keepdims=True)
        p = jnp.exp(s - m)
        l = jnp.sum(p, axis=0, keepdims=True)
        pb = p.astype(BF16)
        ot = None
        base = 0
        for vt_ref in vt_refs:
            part = _dot(vt_ref[...], pb[base:base + vt_ref.shape[1], :])
            ot = part if ot is None else ot + part
            base += vt_ref.shape[1]
        o_ref[u * sub:(u + 1) * sub, :] = (ot / l).T.astype(o_ref.dtype)


def _attention(q, ks, vts, tq=1024):
    b, h, t, dk = q.shape
    dv = vts[0].shape[2]
    tk = sum(k.shape[2] for k in ks)
    tq = min(tq, t)
    sub = min(ATTN_SUB, tq)
    whole = lambda a: pl.BlockSpec((None, None) + a.shape[2:], lambda bi, hi, i: (bi, hi, 0, 0))
    return pl.pallas_call(
        _attn_kernel,
        out_shape=jax.ShapeDtypeStruct((b, t, h * dv), BF16),
        grid=(b, h, t // tq),
        in_specs=[pl.BlockSpec((None, None, tq, dk), lambda bi, hi, i: (bi, hi, i, 0))]
                 + [whole(k) for k in ks] + [whole(v) for v in vts],
        out_specs=pl.BlockSpec((None, tq, dv), lambda bi, hi, i: (bi, i, hi)),
        scratch_shapes=[pltpu.VMEM((tq // sub, tk, sub), F32)],
        compiler_params=_cp("arbitrary", "arbitrary", "arbitrary"),
        name="attention",
    )(q, *ks, *vts)


def _sink_attn_kernel(sink_ref, q_ref, k_ref, v_ref, o_ref):
    group = q_ref.shape[0]
    k = k_ref[...]
    v = v_ref[...]
    for g in range(group):
        s = _dot_nt(q_ref[g], k)
        sk = sink_ref[pl.program_id(1) * group + g]
        m = jnp.maximum(jnp.max(s, axis=1, keepdims=True), sk)
        p = jnp.exp(s - m)
        l = jnp.sum(p, axis=1, keepdims=True) + jnp.exp(sk - m)
        o_ref[g] = (_dot(p.astype(BF16), v) / l).astype(o_ref.dtype)


def _sink_attention(q, k, v, sink):
    b, hq, t, d = q.shape
    hk = k.shape[1]
    group = hq // hk
    kv_spec = pl.BlockSpec((None, None, t, d), lambda bi, h: (bi, h, 0, 0))
    q_spec = pl.BlockSpec((None, group, t, d), lambda bi, h: (bi, h, 0, 0))
    return pl.pallas_call(
        _sink_attn_kernel,
        out_shape=jax.ShapeDtypeStruct((b, hq, t, d), BF16),
        grid=(b, hk),
        in_specs=[pl.BlockSpec(memory_space=pltpu.SMEM), q_spec, kv_spec, kv_spec],
        out_specs=q_spec,
        compiler_params=_cp("arbitrary", "arbitrary"),
        name="sink_attention",
    )(sink, q, k, v)


def _window_kernel(sink_ref, bias_ref, qt_ref, kp_ref, kc_ref, kn_ref, kx_ref, vp_ref, vc_ref, vn_ref, vx_ref, o_ref):
    hk = pl.program_id(1)
    kk = jnp.concatenate([kx_ref[...], kp_ref[...], kc_ref[...], kn_ref[...]], axis=0)
    vt = jnp.concatenate([vx_ref[...], vp_ref[...], vc_ref[...], vn_ref[...]], axis=1)
    bias = bias_ref[...]
    s = _dot(kk, qt_ref[...])
    s = jnp.concatenate([s[:, g * Q_BLOCK:(g + 1) * Q_BLOCK] + bias for g in range(WIN_GROUP)], axis=1)
    sk = jnp.concatenate([jnp.full((1, Q_BLOCK), sink_ref[hk * WIN_GROUP + g], F32) for g in range(WIN_GROUP)],
                         axis=1)
    m = jnp.maximum(jnp.max(s, axis=0, keepdims=True), sk)
    e = jnp.exp(s - m)
    l = jnp.sum(e, axis=0, keepdims=True) + jnp.exp(sk - m)
    o_ref[...] = (_dot(vt, e.astype(BF16)) / l).astype(o_ref.dtype)


def _window_bias(n_ctx):
    c = np.arange(n_ctx + 3 * Q_BLOCK)[:, None]
    r = np.arange(Q_BLOCK)[None, :]
    band = (c < n_ctx) | (np.abs(c - n_ctx - WINDOW - r) <= WINDOW)
    in_prev = (c >= n_ctx) & (c < n_ctx + Q_BLOCK)
    in_next = c >= n_ctx + 2 * Q_BLOCK
    ok = np.stack([band & ~in_prev, band, band & ~in_next])
    return jnp.asarray(np.where(ok, 0.0, NEG_BIG).astype(np.float32))


def _window_attention(qkv, qkv_ctx, sink):
    b, s, _ = qkv.shape
    n_ctx = qkv_ctx.shape[1]
    hd, hq, hk, grp = WIN_HEAD_DIM, WIN_Q_HEADS, WIN_KV_HEADS, WIN_GROUP
    qw, kw = hq * hd, hk * hd
    nblk = s // Q_BLOCK
    assert nblk >= 2 and WINDOW == Q_BLOCK
    qt = qkv[..., :qw].reshape(b, nblk, Q_BLOCK, hk, grp, hd).transpose(0, 3, 1, 5, 4, 2)
    qt = qt.reshape(b, hk, nblk, hd, grp * Q_BLOCK)
    k = _to_heads(qkv[..., qw:qw + kw], hk)
    kx = _to_heads(qkv_ctx[..., qw:qw + kw], hk)
    vt = qkv[..., qw + kw:].reshape(b, s, hk, hd).transpose(0, 2, 3, 1)
    vxt = qkv_ctx[..., qw + kw:].reshape(b, n_ctx, hk, hd).transpose(0, 2, 3, 1)

    prev = lambda i: jnp.maximum(i - 1, 0)
    nxt = lambda i: jnp.minimum(i + 1, nblk - 1)
    kblk = lambda f: pl.BlockSpec((None, None, Q_BLOCK, hd), lambda bi, h, i: (bi, h, f(i), 0))
    vblk = lambda f: pl.BlockSpec((None, None, hd, Q_BLOCK), lambda bi, h, i: (bi, h, 0, f(i)))
    same = lambda i: i
    edge = lambda bi, h, i: (jnp.where(i == 0, 0, jnp.where(i == nblk - 1, 2, 1)), 0, 0)
    qo_spec = pl.BlockSpec((None, None, None, hd, grp * Q_BLOCK), lambda bi, h, i: (bi, h, i, 0, 0))
    ot = pl.pallas_call(
        _window_kernel,
        out_shape=jax.ShapeDtypeStruct((b, hk, nblk, hd, grp * Q_BLOCK), BF16),
        grid=(b, hk, nblk),
        in_specs=[pl.BlockSpec(memory_space=pltpu.SMEM),
                  pl.BlockSpec((None, n_ctx + 3 * Q_BLOCK, Q_BLOCK), edge),
                  qo_spec,
                  kblk(prev), kblk(same), kblk(nxt),
                  pl.BlockSpec((None, None, n_ctx, hd), lambda bi, h, i: (bi, h, 0, 0)),
                  vblk(prev), vblk(same), vblk(nxt),
                  pl.BlockSpec((None, None, hd, n_ctx), lambda bi, h, i: (bi, h, 0, 0))],
        out_specs=qo_spec,
        compiler_params=_cp("arbitrary", "arbitrary", "arbitrary"),
        name="window_attention",
    )(sink, _window_bias(n_ctx), qt, k, k, k, kx, vt, vt, vt, vxt)
    o = ot.reshape(b, hk, nblk, hd, grp, Q_BLOCK).transpose(0, 2, 5, 1, 4, 3)
    return o.reshape(b, s, qw)


def _shortconv_kernel(u_ref, w_ref, b_ref, o_ref):
    u = u_ref[...]
    rows = u.shape[0]
    t = lax.broadcasted_iota(jnp.int32, u.shape, 0)
    before = jnp.where(t == 0, 0.0, pltpu.roll(u, 1, 0))
    after = jnp.where(t == rows - 1, 0.0, pltpu.roll(u, rows - 1, 0))
    o_ref[...] = before * w_ref[0:1, :] + u * w_ref[1:2, :] + after * w_ref[2:3, :] + b_ref[...]


def _shortconv(u, w, bias, tc=256):
    b, t, n = u.shape
    return pl.pallas_call(
        _shortconv_kernel,
        out_shape=jax.ShapeDtypeStruct((b, t, n), F32),
        grid=(b, n // tc),
        in_specs=[pl.BlockSpec((None, t, tc), lambda bi, j: (bi, 0, j)),
                  pl.BlockSpec((3, tc), lambda bi, j: (0, j)),
                  pl.BlockSpec((1, tc), lambda bi, j: (0, j))],
        out_specs=pl.BlockSpec((None, t, tc), lambda bi, j: (bi, 0, j)),
        compiler_params=_cp("arbitrary", "arbitrary"),
        name="short_conv",
    )(u, w, bias.reshape(1, n))


def _filter_kernel(z_ref, t_ref, w1_ref, b1_ref, w2_ref, b2_ref, w3_ref, b3_ref, fr_ref, w4_ref, dec_ref, o_ref, *,
                   zero_row):
    def dense(hv, w_ref):
        a_hi, a_lo = _split(hv)
        w_hi, w_lo = _split(w_ref[...])
        return _dot3(a_hi, a_lo, w_hi, w_lo)

    freq = fr_ref[...]
    hv = jnp.sin(freq * (dense(z_ref[...], w1_ref) + b1_ref[...]))
    hv = jnp.sin(freq * (dense(hv, w2_ref) + b2_ref[...]))
    hv = jnp.sin(freq * (dense(hv, w3_ref) + b3_ref[...]))
    out = dense(hv, w4_ref) * jnp.exp(-t_ref[...] * jnp.abs(dec_ref[...]))
    tr = out.shape[0]
    row = pl.program_id(1) * tr + lax.broadcasted_iota(jnp.int32, out.shape, 0)
    out = jnp.where(row == zero_row, 0.0, out)
    if len(o_ref.shape) == 2:
        o_ref[...] = out
    else:
        n2 = o_ref.shape[0]
        for al in range(o_ref.shape[1]):
            o_ref[:, al, :] = out[al * n2:(al + 1) * n2, :]


def _hyena_filter(seq, f_w1, f_b1, f_w2, f_b2, f_w3, f_b3, f_freq, f_w4, decay, bmajor_rows=None):
    d = f_w4.shape[1] // (2 * HY_ORDER)
    n = 2 * seq
    t = np.linspace(0.0, 1.0, seq, dtype=np.float32)[:, None]
    w = (2.0 * math.pi * np.arange(seq, dtype=np.float32)[:, None] / seq).astype(np.float32)
    f = np.linspace(1e-4, HY_BANDS - 1, HY_BANDS, dtype=np.float32)[None, :]
    z = np.concatenate([t, np.cos(f * w), -np.sin(f * w)], axis=-1).astype(np.float32)
    lag = np.arange(n)
    lag = np.where(lag < seq, lag, n - lag) % seq
    zc = np.zeros((n, LANES), np.float32)
    zc[:, :HY_EMB] = z[lag]
    tc = t[lag]
    w1 = jnp.zeros((LANES, HY_FILTER_W), F32).at[:HY_EMB].set(f_w1)
    tr = SUBLANES * bmajor_rows if bmajor_rows else min(512, seq)
    fw = HY_FILTER_W
    if bmajor_rows:
        out_shape = jax.ShapeDtypeStruct((HY_ORDER, bmajor_rows, n // bmajor_rows, d), F32)
        out_spec = pl.BlockSpec((None, bmajor_rows, SUBLANES, d), lambda o, i: (o, 0, i, 0))
    else:
        out_shape = jax.ShapeDtypeStruct((HY_ORDER, n, d), F32)
        out_spec = pl.BlockSpec((None, tr, d), lambda o, i: (o, i, 0))
    row = lambda v: v.reshape(1, -1)
    const = lambda o, i: (0, 0)
    col = lambda o, i: (0, 2 * o + (i * tr) // seq)
    return pl.pallas_call(
        functools.partial(_filter_kernel, zero_row=seq),
        out_shape=out_shape,
        grid=(HY_ORDER, n // tr),
        in_specs=[pl.BlockSpec((tr, LANES), lambda o, i: (i, 0)),
                  pl.BlockSpec((tr, 1), lambda o, i: (i, 0)),
                  pl.BlockSpec((LANES, fw), const), pl.BlockSpec((1, fw), const),
                  pl.BlockSpec((fw, fw), const), pl.BlockSpec((1, fw), const),
                  pl.BlockSpec((fw, fw), const), pl.BlockSpec((1, fw), const),
                  pl.BlockSpec((1, fw), const),
                  pl.BlockSpec((fw, d), col),
                  pl.BlockSpec((1, d), col)],
        out_specs=out_spec,
        compiler_params=_cp("arbitrary", "arbitrary"),
        name="hyena_filter",
    )(jnp.asarray(zc), jnp.asarray(tc), w1, row(f_b1), f_w2, row(f_b2), f_w3, row(f_b3), row(f_freq), f_w4,
      row(decay))


def _stack_complex(c):
    return np.block([[c.real, -c.imag], [c.imag, c.real]])


def _hi_lo(m):
    m = jnp.asarray(np.asarray(m, np.float32))
    hi = m.astype(BF16)
    lo = (m - hi.astype(F32)).astype(BF16)
    return hi, lo


def _fft_split(n):
    n2 = 128
    return n // n2, n2


@functools.lru_cache(maxsize=None)
def _fft_matrices_np(n):
    n1, n2 = _fft_split(n)
    h = n1 // 2
    k1 = np.arange(n1)
    bb = np.arange(n2)
    tw = np.exp(-2j * np.pi * np.outer(bb, k1) / n)
    f1 = np.exp(-2j * np.pi * np.outer(k1, np.arange(n1)) / n1)
    la = np.stack([_stack_complex(tw[b][:, None] * f1[:, :h]) for b in range(n2)])
    lf = np.stack([np.concatenate([(tw[b][:, None] * f1).real, (tw[b][:, None] * f1).imag], axis=0)
                   for b in range(n2)])
    f2 = np.exp(-2j * np.pi * np.outer(np.arange(n2), np.arange(n2)) / n2)
    lc = _stack_complex(f2)
    lci = _stack_complex(np.conj(f2))
    f1i = np.exp(2j * np.pi * np.outer(np.arange(h), k1) / n1)
    lai = np.stack([_stack_complex(f1i * np.conj(tw[b])[None, :] / n) for b in range(n2)])
    return la, lf, lc, lci, lai


FFT_UNROLL = 8
FFT_LANES = 256
FFT_MID_CHUNK = 512
FFT_INV_COLS = 4


SUBLANES = 8


def _shortconv_bmajor_kernel(u_ref, first_ref, last_ref, w_ref, b_ref, o_ref):
    g = pl.program_id(1)
    ng = pl.num_programs(1)
    u = u_ref[...]
    rows = u.shape[0]
    n2 = o_ref.shape[0]
    t = lax.broadcasted_iota(jnp.int32, u.shape, 0)
    prev_row = jnp.where(g > 0, last_ref[pl.ds(jnp.maximum(g - 1, 0), 1), :], 0.0)
    next_row = jnp.where(g < ng - 1, first_ref[pl.ds(jnp.minimum(g + 1, ng - 1), 1), :], 0.0)
    before = jnp.where(t == 0, prev_row, pltpu.roll(u, 1, 0))
    after = jnp.where(t == rows - 1, next_row, pltpu.roll(u, rows - 1, 0))
    y = before * w_ref[0:1, :] + u * w_ref[1:2, :] + after * w_ref[2:3, :] + b_ref[...]
    for al in range(rows // n2):
        o_ref[:, al, :] = y[al * n2:(al + 1) * n2, :]


def _shortconv_bmajor(u, w, bias, n2, tc=1024):
    b, t, n = u.shape
    half = t // n2
    rows = SUBLANES * n2
    ng = t // rows
    nj = n // tc
    edge = pl.BlockSpec((None, ng, tc), lambda bi, g, j: (bi, 0, j))
    return pl.pallas_call(
        _shortconv_bmajor_kernel,
        out_shape=jax.ShapeDtypeStruct((b, n2, half, n), F32),
        grid=(b, ng, nj),
        in_specs=[pl.BlockSpec((None, rows, tc), lambda bi, g, j: (bi, g, j)), edge, edge,
                  pl.BlockSpec((3, tc), lambda bi, g, j: (0, j)),
                  pl.BlockSpec((1, tc), lambda bi, g, j: (0, j))],
        out_specs=pl.BlockSpec((None, n2, SUBLANES, tc), lambda bi, g, j: (bi, 0, g, j)),
        compiler_params=_cp("arbitrary", "arbitrary", "arbitrary"),
        name="short_conv_bmajor",
    )(u, u[:, 0::rows, :], u[:, rows - 1::rows, :], w, bias.reshape(1, n))


def _fft_a_kernel(z_ref, m_hi, m_lo, t_ref, *, packed):
    n2, n1 = t_ref.shape[1], t_ref.shape[2]

    def body(b, carry):
        rows = jnp.concatenate([z_ref[0, b], z_ref[1, b]], axis=0) if packed else z_ref[b]
        hi, lo = _split(rows)
        out = _dot3(m_hi[b], m_lo[b], hi, lo)
        t_ref[0, b] = out[:n1]
        t_ref[1, b] = out[n1:]
        return carry
    lax.fori_loop(0, n2, body, 0, unroll=FFT_UNROLL)


def _fft_a(z, col, width, mats, packed):
    m_hi, m_lo = mats
    n2, rows2, k = m_hi.shape
    n1 = rows2 // 2
    td = FFT_LANES
    nblk = width // td
    lead = z.shape[0]
    if packed:
        in_spec = pl.BlockSpec((2, n2, k // 2, td), lambda p, j: (p, 0, 0, col * nblk + j))
        groups = lead // 2
    else:
        in_spec = pl.BlockSpec((None, n2, k, td), lambda p, j: (p, 0, 0, j))
        groups = lead
    c3 = lambda p, j: (0, 0, 0)
    return pl.pallas_call(
        functools.partial(_fft_a_kernel, packed=packed),
        out_shape=jax.ShapeDtypeStruct((groups, 2, n2, n1, width), F32),
        grid=(groups, nblk),
        in_specs=[in_spec, _single(m_hi.shape, c3), _single(m_lo.shape, c3)],
        out_specs=_single((None, 2, n2, n1, td), lambda p, j: (p, 0, 0, 0, j)),
        compiler_params=_cp("arbitrary", "arbitrary"),
        name="fft_strided_stage",
    )(z, m_hi, m_lo)


def _fft_mid_kernel(t_ref, *refs, with_filter):
    if with_filter:
        h_ref, lc_hi, lc_lo, lci_hi, lci_lo, o_ref = refs
    else:
        lc_hi, lc_lo, o_ref = refs
    n2 = t_ref.shape[1]
    for kk in range(t_ref.shape[2]):
        hi, lo = _split(jnp.concatenate([t_ref[0, :, kk, :], t_ref[1, :, kk, :]], axis=0))
        x = _dot3(lc_hi[...], lc_lo[...], hi, lo)
        if with_filter:
            xr, xi = x[:n2], x[n2:]
            hr, hi_ = h_ref[0, kk], h_ref[1, kk]
            y_hi, y_lo = _split(jnp.concatenate([xr * hr - xi * hi_, xr * hi_ + xi * hr], axis=0))
            x = _dot3(lci_hi[...], lci_lo[...], y_hi, y_lo)
        o_ref[0, kk] = x[:n2]
        o_ref[1, kk] = x[n2:]


def _fft_mid(t, mats, spec=None, order=0):
    g, _, n2, n1, width = t.shape
    tw = FFT_MID_CHUNK
    c2 = lambda kg, j, p: (0, 0)
    in_specs = [pl.BlockSpec((None, 2, n2, SUBLANES, tw), lambda kg, j, p: (p, 0, 0, kg, j))]
    args = [t]
    if spec is not None:
        in_specs.append(pl.BlockSpec((None, 2, SUBLANES, n2, tw), lambda kg, j, p: (order, 0, kg, 0, j)))
        args.append(spec)
    in_specs += [pl.BlockSpec(m.shape, c2) for m in mats]
    return pl.pallas_call(
        functools.partial(_fft_mid_kernel, with_filter=spec is not None),
        out_shape=jax.ShapeDtypeStruct((g, 2, n1, n2, width), F32),
        grid=(n1 // SUBLANES, width // tw, g),
        in_specs=in_specs,
        out_specs=pl.BlockSpec((None, 2, SUBLANES, n2, tw), lambda kg, j, p: (p, 0, kg, 0, j)),
        compiler_params=_cp("arbitrary", "arbitrary", "arbitrary"),
        name="fft_contiguous_stage",
    )(*args, *mats)


def _fft_inv_kernel(u_ref, z_ref, g_ref, skip_ref, m_hi, m_lo, o_ref, *, natural_out):
    cols = u_ref.shape[2]
    half = z_ref.shape[2]
    b0 = pl.program_id(1) * cols
    skip = skip_ref[...]
    for c in range(cols):
        hi, lo = _split(jnp.concatenate([u_ref[0, :, c, :], u_ref[1, :, c, :]], axis=0))
        out = _dot3(m_hi[b0 + c], m_lo[b0 + c], hi, lo)
        for plane in range(2):
            y = out[plane * half:(plane + 1) * half]
            val = (g_ref[plane, c] * (y + z_ref[plane, c] * skip)).astype(o_ref.dtype)
            if natural_out:
                o_ref[plane, :, c, :] = val
            else:
                o_ref[plane, c] = val


def _fft_inv(u, z, z_col, g, g_col, skip, order, mats, natural_out):
    m_hi, m_lo = mats
    pairs, _, n1, n2, width = u.shape
    half = n1 // 2
    cols = SUBLANES
    sig = lambda col: pl.BlockSpec((2, cols, half, width), lambda p, i: (p, i, 0, col))
    c3 = lambda p, i: (0, 0, 0)
    if natural_out:
        out_shape = jax.ShapeDtypeStruct((2 * pairs, half, n2, width), BF16)
        out_spec = pl.BlockSpec((2, half, cols, width), lambda p, i: (p, 0, i, 0))
    else:
        out_shape = jax.ShapeDtypeStruct((2 * pairs, n2, half, width), F32)
        out_spec = pl.BlockSpec((2, cols, half, width), lambda p, i: (p, i, 0, 0))
    out = pl.pallas_call(
        functools.partial(_fft_inv_kernel, natural_out=natural_out),
        out_shape=out_shape,
        grid=(pairs, n2 // cols),
        in_specs=[pl.BlockSpec((None, 2, n1, cols, width), lambda p, i: (p, 0, 0, i, 0)),
                  sig(z_col), sig(g_col),
                  pl.BlockSpec((None, 1, width), lambda p, i: (order, 0, 0)),
                  _single(m_hi.shape, c3), _single(m_lo.shape, c3)],
        out_specs=out_spec,
        compiler_params=_cp("arbitrary", "arbitrary"),
        name="fft_inverse_strided_stage",
    )(u, z, g, skip.reshape(HY_ORDER, 1, width), m_hi, m_lo)
    return out.reshape(2 * pairs, half * n2, width) if natural_out else out


def _fft_long_conv(usc, filt_p, skip):
    order, n2, n1, d = filt_p.shape
    la, lf, lc, lci, lai = _fft_matrices_np(n1 * n2)
    fwd = (*_hi_lo(lc),)
    both = (*_hi_lo(lc), *_hi_lo(lci))
    spec = _fft_mid(_fft_a(filt_p, 0, d, _hi_lo(lf), packed=False), fwd)
    la_m, lai_m = _hi_lo(la), _hi_lo(lai)
    z = usc
    for o in range(order):
        t = _fft_a(z, 0, d, la_m, packed=True)
        u = _fft_mid(t, both, spec, o)
        z = _fft_inv(u, z, 0, usc, o + 1, skip, o, lai_m, natural_out=o + 1 == order)
    return z


@functools.lru_cache(maxsize=None)
def _dense_dft_np(seq):
    n = 2 * seq
    k = np.arange(n)
    f = np.exp(-2j * np.pi * np.outer(k, k) / n)
    fwd = _stack_complex(f[:, :seq])
    flt = np.concatenate([f.real, f.imag], axis=0)
    inv = _stack_complex(np.conj(f)[:seq, :] / n)
    return fwd, flt, inv


def _dense_spectrum_kernel(f_ref, m_hi, m_lo, h_ref):
    hi, lo = _split(f_ref[...])
    out = _dot3(m_hi[...], m_lo[...], hi, lo)
    n = f_ref.shape[0]
    h_ref[0] = out[:n]
    h_ref[1] = out[n:]


def _dense_spectrum(filt, td=256):
    order, n, d = filt.shape
    _, flt, _ = _dense_dft_np(n // 2)
    m_hi, m_lo = _hi_lo(flt)
    c2 = lambda o, j: (0, 0)
    return pl.pallas_call(
        _dense_spectrum_kernel,
        out_shape=jax.ShapeDtypeStruct((order, 2, n, d), F32),
        grid=(order, d // td),
        in_specs=[pl.BlockSpec((None, n, td), lambda o, j: (o, 0, j)),
                  pl.BlockSpec(m_hi.shape, c2), pl.BlockSpec(m_lo.shape, c2)],
        out_specs=pl.BlockSpec((None, 2, n, td), lambda o, j: (o, 0, 0, j)),
        compiler_params=_cp("arbitrary", "arbitrary"),
        name="filter_spectrum_dense",
    )(filt, m_hi, m_lo)


def _dense_conv_kernel(z_ref, g_ref, h_ref, skip_ref, f_hi, f_lo, i_hi, i_lo, o_ref):
    seq = z_ref.shape[1]
    n = 2 * seq
    zz = jnp.concatenate([z_ref[0], z_ref[1]], axis=0)
    hi, lo = _split(zz)
    x = _dot3(f_hi[...], f_lo[...], hi, lo)
    xr, xi = x[:n], x[n:]
    hr, hi_ = h_ref[0], h_ref[1]
    y = jnp.concatenate([xr * hr - xi * hi_, xr * hi_ + xi * hr], axis=0)
    y_hi, y_lo = _split(y)
    w = _dot3(i_hi[...], i_lo[...], y_hi, y_lo)
    skip = skip_ref[...]
    for plane in range(2):
        y_p = w[plane * seq:(plane + 1) * seq]
        o_ref[plane] = (g_ref[plane] * (y_p + z_ref[plane] * skip)).astype(o_ref.dtype)


def _dense_conv(z_arr, z_col, g_arr, g_col, spec, order, skip, out_dtype, td=256):
    b, seq, _ = z_arr.shape
    d = spec.shape[3]
    n = 2 * seq
    nblk = d // td
    fwd, _, inv = _dense_dft_np(seq)
    mats = [*_hi_lo(fwd), *_hi_lo(inv)]
    c2 = lambda j, p: (0, 0)
    return pl.pallas_call(
        _dense_conv_kernel,
        out_shape=jax.ShapeDtypeStruct((b, seq, d), out_dtype),
        grid=(nblk, b // 2),
        in_specs=[pl.BlockSpec((2, seq, td), lambda j, p: (p, 0, z_col * nblk + j)),
                  pl.BlockSpec((2, seq, td), lambda j, p: (p, 0, g_col * nblk + j)),
                  pl.BlockSpec((None, 2, n, td), lambda j, p: (order, 0, 0, j)),
                  pl.BlockSpec((None, 1, td), lambda j, p: (order, 0, j))]
                 + [pl.BlockSpec(m.shape, c2) for m in mats],
        out_specs=pl.BlockSpec((2, seq, td), lambda j, p: (p, 0, j)),
        compiler_params=_cp("arbitrary", "arbitrary"),
        name="dense_conv",
    )(z_arr, g_arr, spec, skip.reshape(HY_ORDER, 1, d), *mats)


DENSE_DFT_MAX_SEQ = 512


def _hyena(x, gain, shift, scale, p):
    b, seq, d = x.shape
    u = _nm_matmul(x, gain, shift, scale, p["w_in"], F32, tn=1024)
    if seq <= DENSE_DFT_MAX_SEQ:
        u = _shortconv(u, p["conv_w"], p["conv_b"])
        spec = _dense_spectrum(_hyena_filter(seq, *p["filter"]))
        z1 = _dense_conv(u, 0, u, 1, spec, 0, p["skip"], F32)
        return _dense_conv(z1, 0, u, 2, spec, 1, p["skip"], BF16)
    _, n2 = _fft_split(2 * seq)
    usc = _shortconv_bmajor(u, p["conv_w"], p["conv_b"], n2)
    filt_p = _hyena_filter(seq, *p["filter"], bmajor_rows=n2)
    return _fft_long_conv(usc, filt_p, p["skip"])


def _rope_tables(seq):
    t = np.arange(seq)
    pos = np.stack([t // GRID_W, t % GRID_W], axis=1).astype(np.float32)
    n = 16
    inv = jnp.asarray(ROPE_THETA, F32) ** (-jnp.arange(n, dtype=F32) / n)
    lane = np.arange(LANES)
    axis = (lane % 64) // 32
    idx = lane % 16
    sign = np.where(lane % 32 < 16, -1.0, 1.0).astype(np.float32)
    ang = jnp.asarray(pos)[:, axis] * inv[idx][None, :]
    return jnp.cos(ang), jnp.sin(ang) * sign[None, :]


def _identity_tables(seq):
    return jnp.ones((seq, LANES), F32), jnp.zeros((seq, LANES), F32)


def _mla_weights(w_dq, w_uq, w_dkv, w_ukv):
    d = w_dq.shape[0]
    hh = MLA_HEADS
    wd = jnp.concatenate([w_dq, w_dkv, jnp.zeros((d, LANES - MLA_ROPE), F32)], axis=1).astype(BF16)
    uq = w_uq.reshape(MLA_RANK, hh, MLA_NOPE + MLA_ROPE)
    uq = jnp.pad(uq, ((0, 0), (0, 0), (0, MLA_HEAD_PAD - MLA_NOPE - MLA_ROPE)))
    ukv = w_ukv.reshape(MLA_RANK, hh, MLA_NOPE + MLA_V)
    ukv = jnp.concatenate([ukv[:, :, :MLA_NOPE].reshape(MLA_RANK, hh * MLA_NOPE),
                           ukv[:, :, MLA_NOPE:].reshape(MLA_RANK, hh * MLA_V)], axis=1)
    return wd, uq.reshape(MLA_RANK, hh * MLA_HEAD_PAD).astype(BF16), ukv.astype(BF16)


def _cast_tile_kernel(w_ref, o_ref, *, valid, axis):
    w = w_ref[...]
    idx = pl.program_id(2) * FFN_TILE + lax.broadcasted_iota(jnp.int32, w.shape, axis)
    o_ref[...] = jnp.where(idx < valid, w, 0.0).astype(o_ref.dtype)


def _cast_tile(w, axis):
    depth, two, r, c = w.shape
    valid = w.shape[axis]
    nj = -(-valid // FFN_TILE)
    if axis == 3:
        blk, in_map = (r, FFN_TILE), (lambda l, k, j: (l, k, 0, j))
    else:
        blk, in_map = (FFN_TILE, c), (lambda l, k, j: (l, k, j, 0))
    return pl.pallas_call(
        functools.partial(_cast_tile_kernel, valid=valid, axis=axis - 2),
        out_shape=jax.ShapeDtypeStruct((depth, two, nj) + blk, BF16),
        grid=(depth, two, nj),
        in_specs=[pl.BlockSpec((None, None) + blk, in_map)],
        out_specs=pl.BlockSpec((None, None, None) + blk, lambda l, k, j: (l, k, j, 0, 0)),
        compiler_params=_cp("arbitrary", "arbitrary", "arbitrary"),
        name="cast_tile",
    )(w)


def _to_heads(a, heads):
    b, t, _ = a.shape
    return a.reshape(b, t, heads, -1).transpose(0, 2, 1, 3)


def _from_heads(a):
    b, h, t, e = a.shape
    return a.transpose(0, 2, 1, 3).reshape(b, t, h * e)


def kernel(x, c, ctx, c_ctx, mod_w, mod_b, norm_g, final_g, ffn_wg, ffn_wu, ffn_wd, mla_w_dq, mla_g_q, mla_w_uq, mla_w_dkv, mla_g_kv, mla_w_ukv, mla_w_o, hy_w_in, hy_conv_w, hy_conv_b, hy_f_w1, hy_f_b1, hy_f_w2, hy_f_b2, hy_f_w3, hy_f_b3, hy_f_freq, hy_f_w4, hy_decay, hy_skip, hy_w_out, win_w_qkv, win_sink, win_w_o):
    b, s, d = x.shape
    n_ctx = ctx.shape[1]
    depth = mod_w.shape[0]
    assert b % 2 == 0 and s % 512 == 0 and n_ctx % 256 == 0

    cc = jnp.concatenate([c, c_ctx[None, :], jnp.zeros((8 - (b + 1) % 8, d), F32)], axis=0)
    mods = _modulation(cc, mod_w, mod_b).reshape(depth, cc.shape[0], N_MOD, d)

    ffn_w = (_cast_tile(ffn_wg, 3), _cast_tile(ffn_wu, 3), _cast_tile(ffn_wd, 2))
    cos_l, sin_l = _rope_tables(s)
    cos_c, sin_c = _identity_tables(n_ctx)

    for i in range(depth):
        kind, j = i % N_MIXERS, i // N_MIXERS
        need_ctx = i < depth - 1
        ctx_live = need_ctx or kind != 1
        ml = [mods[i, :b, k][:, None, :] for k in range(N_MOD)]
        mc = [jnp.broadcast_to(mods[i, b, k][None, None, :], (b, 1, d)) for k in range(N_MOD)]

        x = _ffn(x, norm_g[i, 0], ml[0], ml[1], ml[2], *ffn_w, i, 0)
        if ctx_live:
            ctx = _ffn(ctx, norm_g[i, 0], mc[0], mc[1], mc[2], *ffn_w, i, 0)

        g1 = norm_g[i, 1]
        o_c = None
        if kind == 0:
            wd, wuq, wukv = _mla_weights(mla_w_dq[j], mla_w_uq[j], mla_w_dkv[j], mla_w_ukv[j])
            q_l, k_l, v_l = _mla_proj(x, g1, ml[3], ml[4], wd, mla_g_q[j], mla_g_kv[j], wuq, wukv, cos_l, sin_l)
            q_c, k_c, v_c = _mla_proj(ctx, g1, mc[3], mc[4], wd, mla_g_q[j], mla_g_kv[j], wuq, wukv, cos_c, sin_c)
            w_o = mla_w_o[j].astype(BF16)
            o_l = _attention(q_l, [k_c, k_l], [v_c, v_l])
            if need_ctx:
                o_c = _attention(q_c, [k_c], [v_c])
        elif kind == 1:
            hp = dict(w_in=hy_w_in[j].astype(BF16), conv_w=hy_conv_w[j], conv_b=hy_conv_b[j],
                      filter=(hy_f_w1[j], hy_f_b1[j], hy_f_w2[j], hy_f_b2[j], hy_f_w3[j], hy_f_b3[j],
                              hy_f_freq[j], hy_f_w4[j], hy_decay[j]),
                      skip=hy_skip[j])
            w_o = hy_w_out[j].astype(BF16)
            o_l = _hyena(x, g1, ml[3], ml[4], hp)
            if need_ctx:
                o_c = _hyena(ctx, g1, mc[3], mc[4], hp)
        else:
            qw = WIN_Q_HEADS * WIN_HEAD_DIM
            kw = WIN_KV_HEADS * WIN_HEAD_DIM
            w_qkv = win_w_qkv[j].astype(BF16)
            w_o = win_w_o[j].astype(BF16)
            q_chunks = qw // LANES
            rope_l = (cos_l, sin_l, (qw + kw) // LANES, q_chunks, WIN_HEAD_DIM ** -0.5)
            rope_c = (cos_c, sin_c, (qw + kw) // LANES, q_chunks, WIN_HEAD_DIM ** -0.5)
            qkv_l = _nm_matmul(x, g1, ml[3], ml[4], w_qkv, BF16, tn=qw + 2 * kw, rope=rope_l)
            qkv_c = _nm_matmul(ctx, g1, mc[3], mc[4], w_qkv, BF16, tn=qw + 2 * kw, rope=rope_c)
            o_l = _window_attention(qkv_l, qkv_c, win_sink[j])
            if need_ctx:
                q_c = _to_heads(qkv_c[..., :qw], WIN_Q_HEADS)
                k_c = _to_heads(qkv_c[..., qw:qw + kw], WIN_KV_HEADS)
                v_c = _to_heads(qkv_c[..., qw + kw:], WIN_KV_HEADS)
                o_c = _from_heads(_sink_attention(q_c, k_c, v_c, win_sink[j]))

        x = _mm_res(o_l, w_o, x, ml[5])
        x = _ffn(x, norm_g[i, 2], ml[6], ml[7], ml[8], *ffn_w, i, 1)
        if need_ctx:
            ctx = _mm_res(o_c, w_o, ctx, mc[5])
            ctx = _ffn(ctx, norm_g[i, 2], mc[6], mc[7], mc[8], *ffn_w, i, 1)
    return _final_norm(x, final_g)
```

```python
import functools
import math

import numpy as np
import jax
import jax.numpy as jnp
from jax import lax
from jax.experimental import pallas as pl
from jax.experimental.pallas import tpu as pltpu

F32 = jnp.float32
BF16 = jnp.bfloat16

GRID_W = 64
N_MOD = 9
FFN_RES = 0.5
NORM_EPS = 1e-6
ROPE_THETA = 10000.0
N_MIXERS = 3
MLA_HEADS = 16
MLA_NOPE = 128
MLA_ROPE = 64
MLA_V = 128
MLA_RANK = 512
MLA_HEAD_PAD = 256
HY_ORDER = 2
HY_EMB = 33
HY_BANDS = (HY_EMB - 1) // 2
HY_FILTER_W = 64
WIN_HEAD_DIM = 64
WIN_Q_HEADS = 32
WIN_KV_HEADS = 4
WIN_GROUP = WIN_Q_HEADS // WIN_KV_HEADS
WINDOW = 128
Q_BLOCK = 128

LANES = 128
V7X_VMEM_LIMIT = 56 * 1024 * 1024
FFN_TILE = 512
NEG_BIG = -1e30


def _cp(*sem):
    return pltpu.CompilerParams(dimension_semantics=sem, vmem_limit_bytes=V7X_VMEM_LIMIT)


def _single(shape, imap):
    return pl.BlockSpec(shape, imap, pipeline_mode=pl.Buffered(1))


def _dot(a, b):
    return jnp.dot(a, b, preferred_element_type=F32)


def _dot_nt(a, b):
    return lax.dot_general(a, b, (((1,), (1,)), ((), ())), preferred_element_type=F32)


def _split(x):
    hi = x.astype(BF16)
    lo = (x - hi.astype(F32)).astype(BF16)
    return hi, lo


def _dot3(a_hi, a_lo, b_hi, b_lo):
    return _dot(a_hi, b_hi) + (_dot(a_hi, b_lo) + _dot(a_lo, b_hi))


def _modnorm(x, gain, shift, scale):
    r = lax.rsqrt(jnp.mean(x * x, axis=-1, keepdims=True) + NORM_EPS)
    return (x * r * gain) * (1.0 + scale) + shift


def _rms(x, gain):
    return x * lax.rsqrt(jnp.mean(x * x, axis=-1, keepdims=True) + NORM_EPS) * gain


def _rope128(x, cos, sin):
    lane = lax.broadcasted_iota(jnp.int32, x.shape, 1)
    partner = jnp.where(lane % 32 < 16, pltpu.roll(x, LANES - 16, 1), pltpu.roll(x, 16, 1))
    return x * cos + partner * sin


def _mod_kernel(a_ref, w_ref, b_ref, o_ref):
    a = a_ref[...]
    a = a * jax.nn.sigmoid(a)
    a_hi, a_lo = _split(a)
    w_hi, w_lo = _split(w_ref[...])
    o_ref[...] = _dot3(a_hi, a_lo, w_hi, w_lo) + b_ref[...]


def _modulation(cc, mod_w, mod_b):
    depth, d, n = mod_w.shape
    rows = cc.shape[0]
    tn = n // 16
    return pl.pallas_call(
        _mod_kernel,
        out_shape=jax.ShapeDtypeStruct((depth, rows, n), F32),
        grid=(depth, n // tn),
        in_specs=[
            pl.BlockSpec((rows, d), lambda l, j: (0, 0)),
            pl.BlockSpec((None, d, tn), lambda l, j: (l, 0, j)),
            pl.BlockSpec((None, 1, tn), lambda l, j: (l, 0, j)),
        ],
        out_specs=pl.BlockSpec((None, rows, tn), lambda l, j: (l, 0, j)),
        compiler_params=_cp("arbitrary", "arbitrary"),
        name="modulation",
    )(cc, mod_w, mod_b.reshape(depth, 1, n))


def _ffn_kernel(x_ref, g_ref, sh_ref, sc_ref, gt_ref, wg_ref, wu_ref, wd_ref, o_ref, h_scr, acc_scr):
    j = pl.program_id(2)

    @pl.when(j == 0)
    def _():
        h_scr[...] = _modnorm(x_ref[...], g_ref[...], sh_ref[...], sc_ref[...]).astype(BF16)
        acc_scr[...] = jnp.zeros_like(acc_scr)

    h = h_scr[...]
    g = _dot(h, wg_ref[...])
    u = _dot(h, wu_ref[...])
    a = (g * jax.nn.sigmoid(g) * u).astype(BF16)
    acc_scr[...] += _dot(a, wd_ref[...])

    @pl.when(j == pl.num_programs(2) - 1)
    def _():
        o_ref[...] = x_ref[...] + FFN_RES * gt_ref[...] * acc_scr[...]


def _ffn(x, gain, shift, scale, gate, wg, wu, wd, layer, slot, tm=512):
    b, t, d = x.shape
    nj = wg.shape[2]
    tm = min(tm, t)
    vec = pl.BlockSpec((None, 1, d), lambda bi, i, j: (bi, 0, 0))
    return pl.pallas_call(
        _ffn_kernel,
        out_shape=jax.ShapeDtypeStruct((b, t, d), F32),
        grid=(b, t // tm, nj),
        in_specs=[
            pl.BlockSpec((None, tm, d), lambda bi, i, j: (bi, i, 0)),
            pl.BlockSpec((1, d), lambda bi, i, j: (0, 0)),
            vec, vec, vec,
            pl.BlockSpec((None, None, None, d, FFN_TILE), lambda bi, i, j: (layer, slot, j, 0, 0)),
            pl.BlockSpec((None, None, None, d, FFN_TILE), lambda bi, i, j: (layer, slot, j, 0, 0)),
            pl.BlockSpec((None, None, None, FFN_TILE, d), lambda bi, i, j: (layer, slot, j, 0, 0)),
        ],
        out_specs=pl.BlockSpec((None, tm, d), lambda bi, i, j: (bi, i, 0)),
        scratch_shapes=[pltpu.VMEM((tm, d), BF16), pltpu.VMEM((tm, d), F32)],
        compiler_params=_cp("arbitrary", "arbitrary", "arbitrary"),
        name="ffn",
    )(x, gain.reshape(1, d), shift, scale, gate, wg, wu, wd)


def _nm_matmul_kernel(x_ref, g_ref, sh_ref, sc_ref, w_ref, *rest, rope_chunks, scaled_chunks, out_scale):
    if rope_chunks:
        cos_ref, sin_ref, o_ref, h_scr = rest
    else:
        o_ref, h_scr = rest

    @pl.when(pl.program_id(2) == 0)
    def _():
        h_scr[...] = _modnorm(x_ref[...], g_ref[...], sh_ref[...], sc_ref[...]).astype(BF16)

    y = _dot(h_scr[...], w_ref[...])
    if not rope_chunks:
        o_ref[...] = y.astype(o_ref.dtype)
        return
    cos = cos_ref[...]
    sin = sin_ref[...]
    for c in range(y.shape[1] // LANES):
        yc = y[:, c * LANES:(c + 1) * LANES]
        if c < rope_chunks:
            yc = _rope128(yc, cos, sin)
        if c < scaled_chunks:
            yc = yc * out_scale
        o_ref[:, c * LANES:(c + 1) * LANES] = yc.astype(o_ref.dtype)


def _nm_matmul(x, gain, shift, scale, w, out_dtype, tn, tm=512, rope=None):
    b, t, d = x.shape
    n = w.shape[1]
    tm = min(tm, t)
    vec = pl.BlockSpec((None, 1, d), lambda bi, i, j: (bi, 0, 0))
    in_specs = [
        pl.BlockSpec((None, tm, d), lambda bi, i, j: (bi, i, 0)),
        pl.BlockSpec((1, d), lambda bi, i, j: (0, 0)),
        vec, vec,
        pl.BlockSpec((d, tn), lambda bi, i, j: (0, j)),
    ]
    args = [x, gain.reshape(1, d), shift, scale, w]
    rope_chunks = scaled_chunks = 0
    out_scale = 1.0
    if rope is not None:
        cos, sin, rope_chunks, scaled_chunks, out_scale = rope
        in_specs += [pl.BlockSpec((tm, LANES), lambda bi, i, j: (i, 0))] * 2
        args += [cos, sin]
    return pl.pallas_call(
        functools.partial(_nm_matmul_kernel, rope_chunks=rope_chunks, scaled_chunks=scaled_chunks,
                          out_scale=out_scale),
        out_shape=jax.ShapeDtypeStruct((b, t, n), out_dtype),
        grid=(b, t // tm, n // tn),
        in_specs=in_specs,
        out_specs=pl.BlockSpec((None, tm, tn), lambda bi, i, j: (bi, i, j)),
        scratch_shapes=[pltpu.VMEM((tm, d), BF16)],
        compiler_params=_cp("arbitrary", "arbitrary", "arbitrary"),
        name="norm_matmul",
    )(*args)


def _mm_res_kernel(a_ref, w_ref, r_ref, gt_ref, o_ref):
    o_ref[...] = r_ref[...] + gt_ref[...] * _dot(a_ref[...], w_ref[...])


def _mm_res(a, w, res, gate, tm=512):
    b, t, k = a.shape
    n = w.shape[1]
    tm = min(tm, t)
    return pl.pallas_call(
        _mm_res_kernel,
        out_shape=jax.ShapeDtypeStruct((b, t, n), F32),
        grid=(b, t // tm),
        in_specs=[
            pl.BlockSpec((None, tm, k), lambda bi, i: (bi, i, 0)),
            pl.BlockSpec((k, n), lambda bi, i: (0, 0)),
            pl.BlockSpec((None, tm, n), lambda bi, i: (bi, i, 0)),
            pl.BlockSpec((None, 1, n), lambda bi, i: (bi, 0, 0)),
        ],
        out_specs=pl.BlockSpec((None, tm, n), lambda bi, i: (bi, i, 0)),
        compiler_params=_cp("arbitrary", "arbitrary"),
        name="proj_residual",
    )(a, w, res, gate)


def _final_norm_kernel(x_ref, g_ref, o_ref):
    o_ref[...] = _rms(x_ref[...], g_ref[...])


def _final_norm(x, gain, tm=512):
    b, t, d = x.shape
    return pl.pallas_call(
        _final_norm_kernel,
        out_shape=jax.ShapeDtypeStruct((b, t, d), F32),
        grid=(b, t // tm),
        in_specs=[pl.BlockSpec((None, tm, d), lambda bi, i: (bi, i, 0)),
                  pl.BlockSpec((1, d), lambda bi, i: (0, 0))],
        out_specs=pl.BlockSpec((None, tm, d), lambda bi, i: (bi, i, 0)),
        compiler_params=_cp("arbitrary", "arbitrary"),
        name="final_norm",
    )(x, gain.reshape(1, d))


def _mla_proj_kernel(x_ref, g_ref, sh_ref, sc_ref, wd_ref, gq_ref, gkv_ref, wuq_ref, wukv_ref, cos_ref, sin_ref,
                     q_ref, k_ref, v_ref, *, q_scale):
    h = _modnorm(x_ref[...], g_ref[...], sh_ref[...], sc_ref[...]).astype(BF16)
    a = _dot(h, wd_ref[...])
    cq = _rms(a[:, :MLA_RANK], gq_ref[...]).astype(BF16)
    ckv = _rms(a[:, MLA_RANK:2 * MLA_RANK], gkv_ref[...]).astype(BF16)
    cos = cos_ref[...]
    sin = sin_ref[...]
    kr = _rope128(a[:, 2 * MLA_RANK:], cos, sin).astype(BF16)
    q = _dot(cq, wuq_ref[...])
    kv = _dot(ckv, wukv_ref[...])
    for hd in range(MLA_HEADS):
        base = hd * MLA_HEAD_PAD
        q_ref[hd, :, :LANES] = (q[:, base:base + LANES] * q_scale).astype(BF16)
        q_ref[hd, :, LANES:] = (_rope128(q[:, base + LANES:base + 2 * LANES], cos, sin) * q_scale).astype(BF16)
        k_ref[hd, :, :LANES] = kv[:, hd * MLA_NOPE:(hd + 1) * MLA_NOPE].astype(BF16)
        k_ref[hd, :, LANES:] = kr
        voff = MLA_HEADS * MLA_NOPE + hd * MLA_V
        v_ref[hd] = kv[:, voff:voff + MLA_V].T.astype(BF16)


def _mla_proj(x, gain, shift, scale, wd, gq, gkv, wuq, wukv, cos, sin, tm=256):
    b, t, d = x.shape
    tm = min(tm, t)
    vec = pl.BlockSpec((None, 1, d), lambda bi, i: (bi, 0, 0))
    const = lambda bi, i: (0, 0)
    head_out = lambda w: pl.BlockSpec((None, MLA_HEADS, tm, w), lambda bi, i: (bi, 0, i, 0))
    return pl.pallas_call(
        functools.partial(_mla_proj_kernel, q_scale=(MLA_NOPE + MLA_ROPE) ** -0.5 * math.log2(math.e)),
        out_shape=[jax.ShapeDtypeStruct((b, MLA_HEADS, t, MLA_HEAD_PAD), BF16),
                   jax.ShapeDtypeStruct((b, MLA_HEADS, t, MLA_HEAD_PAD), BF16),
                   jax.ShapeDtypeStruct((b, MLA_HEADS, MLA_V, t), BF16)],
        grid=(b, t // tm),
        in_specs=[
            pl.BlockSpec((None, tm, d), lambda bi, i: (bi, i, 0)),
            pl.BlockSpec((1, d), const),
            vec, vec,
            _single(wd.shape, const),
            pl.BlockSpec((1, MLA_RANK), const),
            pl.BlockSpec((1, MLA_RANK), const),
            _single(wuq.shape, const),
            _single(wukv.shape, const),
            pl.BlockSpec((tm, LANES), lambda bi, i: (i, 0)),
            pl.BlockSpec((tm, LANES), lambda bi, i: (i, 0)),
        ],
        out_specs=[head_out(MLA_HEAD_PAD), head_out(MLA_HEAD_PAD),
                   pl.BlockSpec((None, MLA_HEADS, MLA_V, tm), lambda bi, i: (bi, 0, 0, i))],
        compiler_params=_cp("arbitrary", "arbitrary"),
        name="mla_proj",
    )(x, gain.reshape(1, d), shift, scale, wd, gq.reshape(1, -1), gkv.reshape(1, -1), wuq, wukv, cos, sin)


ATTN_CHUNK = 256
ATTN_SUB = 256


def _attn_kernel(q_ref, *refs):
    o_ref, s_scr = refs[-2:]
    nseg = (len(refs) - 2) // 2
    k_refs, vt_refs = refs[:nseg], refs[nseg:2 * nseg]
    tq = q_ref.shape[0]
    sub = s_scr.shape[2]
    for u in range(tq // sub):
        q = q_ref[u * sub:(u + 1) * sub, :]
        base = 0
        for k_ref in k_refs:
            for c in range(k_ref.shape[0] // ATTN_CHUNK):
                rows = slice(c * ATTN_CHUNK, (c + 1) * ATTN_CHUNK)
                s_scr[u, base + rows.start:base + rows.stop, :] = _dot_nt(k_ref[rows, :], q)
            base += k_ref.shape[0]
    for u in range(tq // sub):
        s = s_scr[u]
        m = jnp.max(s, axis=0, keepdims=True)
        p = jnp.exp2(s - m)
        l = jnp.sum(p, axis=0, keepdims=True)
        pb = p.astype(BF16)
        ot = None
        base = 0
        for vt_ref in vt_refs:
            part = _dot(vt_ref[...], pb[base:base + vt_ref.shape[1], :])
            ot = part if ot is None else ot + part
            base += vt_ref.shape[1]
        o_ref[u * sub:(u + 1) * sub, :] = (ot / l).T.astype(o_ref.dtype)


def _attention(q, ks, vts, tq=1024):
    b, h, t, dk = q.shape
    dv = vts[0].shape[2]
    tk = sum(k.shape[2] for k in ks)
    tq = min(tq, t)
    sub = min(ATTN_SUB, tq)
    whole = lambda a: pl.BlockSpec((None, None) + a.shape[2:], lambda bi, hi, i: (bi, hi, 0, 0))
    return pl.pallas_call(
        _attn_kernel,
        out_shape=jax.ShapeDtypeStruct((b, t, h * dv), BF16),
        grid=(b, h, t // tq),
        in_specs=[pl.BlockSpec((None, None, tq, dk), lambda bi, hi, i: (bi, hi, i, 0))]
                 + [whole(k) for k in ks] + [whole(v) for v in vts],
        out_specs=pl.BlockSpec((None, tq, dv), lambda bi, hi, i: (bi, i, hi)),
        scratch_shapes=[pltpu.VMEM((tq // sub, tk, sub), F32)],
        compiler_params=_cp("arbitrary", "arbitrary", "arbitrary"),
        name="attention",
    )(q, *ks, *vts)


def _sink_attn_kernel(sink_ref, q_ref, k_ref, v_ref, o_ref):
    group = q_ref.shape[0]
    k = k_ref[...]
    v = v_ref[...]
    for g in range(group):
        s = _dot_nt(q_ref[g], k)
        sk = sink_ref[pl.program_id(1) * group + g]
        m = jnp.maximum(jnp.max(s, axis=1, keepdims=True), sk)
        p = jnp.exp(s - m)
        l = jnp.sum(p, axis=1, keepdims=True) + jnp.exp(sk - m)
        o_ref[g] = (_dot(p.astype(BF16), v) / l).astype(o_ref.dtype)


def _sink_attention(q, k, v, sink):
    b, hq, t, d = q.shape
    hk = k.shape[1]
    group = hq // hk
    kv_spec = pl.BlockSpec((None, None, t, d), lambda bi, h: (bi, h, 0, 0))
    q_spec = pl.BlockSpec((None, group, t, d), lambda bi, h: (bi, h, 0, 0))
    return pl.pallas_call(
        _sink_attn_kernel,
        out_shape=jax.ShapeDtypeStruct((b, hq, t, d), BF16),
        grid=(b, hk),
        in_specs=[pl.BlockSpec(memory_space=pltpu.SMEM), q_spec, kv_spec, kv_spec],
        out_specs=q_spec,
        compiler_params=_cp("arbitrary", "arbitrary"),
        name="sink_attention",
    )(sink, q, k, v)


def _window_kernel(sink_ref, bias_ref, qt_ref, kp_ref, kc_ref, kn_ref, kx_ref, vp_ref, vc_ref, vn_ref, vx_ref, o_ref):
    hk = pl.program_id(1)
    kk = jnp.concatenate([kx_ref[...], kp_ref[...], kc_ref[...], kn_ref[...]], axis=0)
    vt = jnp.concatenate([vx_ref[...], vp_ref[...], vc_ref[...], vn_ref[...]], axis=1)
    bias = bias_ref[...]
    s = _dot(kk, qt_ref[...])
    s = jnp.concatenate([s[:, g * Q_BLOCK:(g + 1) * Q_BLOCK] + bias for g in range(WIN_GROUP)], axis=1)
    sk = jnp.concatenate([jnp.full((1, Q_BLOCK), sink_ref[hk * WIN_GROUP + g], F32) for g in range(WIN_GROUP)],
                         axis=1)
    m = jnp.maximum(jnp.max(s, axis=0, keepdims=True), sk)
    e = jnp.exp(s - m)
    l = jnp.sum(e, axis=0, keepdims=True) + jnp.exp(sk - m)
    o_ref[...] = (_dot(vt, e.astype(BF16)) / l).astype(o_ref.dtype)


def _window_bias(n_ctx):
    c = np.arange(n_ctx + 3 * Q_BLOCK)[:, None]
    r = np.arange(Q_BLOCK)[None, :]
    band = (c < n_ctx) | (np.abs(c - n_ctx - WINDOW - r) <= WINDOW)
    in_prev = (c >= n_ctx) & (c < n_ctx + Q_BLOCK)
    in_next = c >= n_ctx + 2 * Q_BLOCK
    ok = np.stack([band & ~in_prev, band, band & ~in_next])
    return jnp.asarray(np.where(ok, 0.0, NEG_BIG).astype(np.float32))


def _window_attention(qkv, qkv_ctx, sink):
    b, s, _ = qkv.shape
    n_ctx = qkv_ctx.shape[1]
    hd, hq, hk, grp = WIN_HEAD_DIM, WIN_Q_HEADS, WIN_KV_HEADS, WIN_GROUP
    qw, kw = hq * hd, hk * hd
    nblk = s // Q_BLOCK
    assert nblk >= 2 and WINDOW == Q_BLOCK
    qt = qkv[..., :qw].reshape(b, nblk, Q_BLOCK, hk, grp, hd).transpose(0, 3, 1, 5, 4, 2)
    qt = qt.reshape(b, hk, nblk, hd, grp * Q_BLOCK)
    k = _to_heads(qkv[..., qw:qw + kw], hk)
    kx = _to_heads(qkv_ctx[..., qw:qw + kw], hk)
    vt = qkv[..., qw + kw:].reshape(b, s, hk, hd).transpose(0, 2, 3, 1)
    vxt = qkv_ctx[..., qw + kw:].reshape(b, n_ctx, hk, hd).transpose(0, 2, 3, 1)

    prev = lambda i: jnp.maximum(i - 1, 0)
    nxt = lambda i: jnp.minimum(i + 1, nblk - 1)
    kblk = lambda f: pl.BlockSpec((None, None, Q_BLOCK, hd), lambda bi, h, i: (bi, h, f(i), 0))
    vblk = lambda f: pl.BlockSpec((None, None, hd, Q_BLOCK), lambda bi, h, i: (bi, h, 0, f(i)))
    same = lambda i: i
    edge = lambda bi, h, i: (jnp.where(i == 0, 0, jnp.where(i == nblk - 1, 2, 1)), 0, 0)
    qo_spec = pl.BlockSpec((None, None, None, hd, grp * Q_BLOCK), lambda bi, h, i: (bi, h, i, 0, 0))
    ot = pl.pallas_call(
        _window_kernel,
        out_shape=jax.ShapeDtypeStruct((b, hk, nblk, hd, grp * Q_BLOCK), BF16),
        grid=(b, hk, nblk),
        in_specs=[pl.BlockSpec(memory_space=pltpu.SMEM),
                  pl.BlockSpec((None, n_ctx + 3 * Q_BLOCK, Q_BLOCK), edge),
                  qo_spec,
                  kblk(prev), kblk(same), kblk(nxt),
                  pl.BlockSpec((None, None, n_ctx, hd), lambda bi, h, i: (bi, h, 0, 0)),
                  vblk(prev), vblk(same), vblk(nxt),
                  pl.BlockSpec((None, None, hd, n_ctx), lambda bi, h, i: (bi, h, 0, 0))],
        out_specs=qo_spec,
        compiler_params=_cp("arbitrary", "arbitrary", "arbitrary"),
        name="window_attention",
    )(sink, _window_bias(n_ctx), qt, k, k, k, kx, vt, vt, vt, vxt)
    o = ot.reshape(b, hk, nblk, hd, grp, Q_BLOCK).transpose(0, 2, 5, 1, 4, 3)
    return o.reshape(b, s, qw)


def _shortconv_kernel(u_ref, w_ref, b_ref, o_ref):
    u = u_ref[...]
    rows = u.shape[0]
    t = lax.broadcasted_iota(jnp.int32, u.shape, 0)
    before = jnp.where(t == 0, 0.0, pltpu.roll(u, 1, 0))
    after = jnp.where(t == rows - 1, 0.0, pltpu.roll(u, rows - 1, 0))
    o_ref[...] = before * w_ref[0:1, :] + u * w_ref[1:2, :] + after * w_ref[2:3, :] + b_ref[...]


def _shortconv(u, w, bias, tc=256):
    b, t, n = u.shape
    return pl.pallas_call(
        _shortconv_kernel,
        out_shape=jax.ShapeDtypeStruct((b, t, n), F32),
        grid=(b, n // tc),
        in_specs=[pl.BlockSpec((None, t, tc), lambda bi, j: (bi, 0, j)),
                  pl.BlockSpec((3, tc), lambda bi, j: (0, j)),
                  pl.BlockSpec((1, tc), lambda bi, j: (0, j))],
        out_specs=pl.BlockSpec((None, t, tc), lambda bi, j: (bi, 0, j)),
        compiler_params=_cp("arbitrary", "arbitrary"),
        name="short_conv",
    )(u, w, bias.reshape(1, n))


def _filter_kernel(z_ref, t_ref, w1_ref, b1_ref, w2_ref, b2_ref, w3_ref, b3_ref, fr_ref, w4_ref, dec_ref, o_ref, *,
                   zero_row):
    def dense(hv, w_ref):
        a_hi, a_lo = _split(hv)
        w_hi, w_lo = _split(w_ref[...])
        return _dot3(a_hi, a_lo, w_hi, w_lo)

    freq = fr_ref[...]
    hv = jnp.sin(freq * (dense(z_ref[...], w1_ref) + b1_ref[...]))
    hv = jnp.sin(freq * (dense(hv, w2_ref) + b2_ref[...]))
    hv = jnp.sin(freq * (dense(hv, w3_ref) + b3_ref[...]))
    out = dense(hv, w4_ref) * jnp.exp(-t_ref[...] * jnp.abs(dec_ref[...]))
    tr = out.shape[0]
    row = pl.program_id(1) * tr + lax.broadcasted_iota(jnp.int32, out.shape, 0)
    out = jnp.where(row == zero_row, 0.0, out)
    if len(o_ref.shape) == 2:
        o_ref[...] = out
    else:
        n2 = o_ref.shape[0]
        for al in range(o_ref.shape[1]):
            o_ref[:, al, :] = out[al * n2:(al + 1) * n2, :]


def _hyena_filter(seq, f_w1, f_b1, f_w2, f_b2, f_w3, f_b3, f_freq, f_w4, decay, bmajor_rows=None):
    d = f_w4.shape[1] // (2 * HY_ORDER)
    n = 2 * seq
    t = np.linspace(0.0, 1.0, seq, dtype=np.float32)[:, None]
    w = (2.0 * math.pi * np.arange(seq, dtype=np.float32)[:, None] / seq).astype(np.float32)
    f = np.linspace(1e-4, HY_BANDS - 1, HY_BANDS, dtype=np.float32)[None, :]
    z = np.concatenate([t, np.cos(f * w), -np.sin(f * w)], axis=-1).astype(np.float32)
    lag = np.arange(n)
    lag = np.where(lag < seq, lag, n - lag) % seq
    zc = np.zeros((n, LANES), np.float32)
    zc[:, :HY_EMB] = z[lag]
    tc = t[lag]
    w1 = jnp.zeros((LANES, HY_FILTER_W), F32).at[:HY_EMB].set(f_w1)
    tr = SUBLANES * bmajor_rows if bmajor_rows else min(512, seq)
    fw = HY_FILTER_W
    if bmajor_rows:
        out_shape = jax.ShapeDtypeStruct((HY_ORDER, bmajor_rows, n // bmajor_rows, d), F32)
        out_spec = pl.BlockSpec((None, bmajor_rows, SUBLANES, d), lambda o, i: (o, 0, i, 0))
    else:
        out_shape = jax.ShapeDtypeStruct((HY_ORDER, n, d), F32)
        out_spec = pl.BlockSpec((None, tr, d), lambda o, i: (o, i, 0))
    row = lambda v: v.reshape(1, -1)
    const = lambda o, i: (0, 0)
    col = lambda o, i: (0, 2 * o + (i * tr) // seq)
    return pl.pallas_call(
        functools.partial(_filter_kernel, zero_row=seq),
        out_shape=out_shape,
        grid=(HY_ORDER, n // tr),
        in_specs=[pl.BlockSpec((tr, LANES), lambda o, i: (i, 0)),
                  pl.BlockSpec((tr, 1), lambda o, i: (i, 0)),
                  pl.BlockSpec((LANES, fw), const), pl.BlockSpec((1, fw), const),
                  pl.BlockSpec((fw, fw), const), pl.BlockSpec((1, fw), const),
                  pl.BlockSpec((fw, fw), const), pl.BlockSpec((1, fw), const),
                  pl.BlockSpec((1, fw), const),
                  pl.BlockSpec((fw, d), col),
                  pl.BlockSpec((1, d), col)],
        out_specs=out_spec,
        compiler_params=_cp("arbitrary", "arbitrary"),
        name="hyena_filter",
    )(jnp.asarray(zc), jnp.asarray(tc), w1, row(f_b1), f_w2, row(f_b2), f_w3, row(f_b3), row(f_freq), f_w4,
      row(decay))


def _stack_complex(c):
    return np.block([[c.real, -c.imag], [c.imag, c.real]])


def _hi_lo(m):
    m = jnp.asarray(np.asarray(m, np.float32))
    hi = m.astype(BF16)
    lo = (m - hi.astype(F32)).astype(BF16)
    return hi, lo


def _fft_split(n):
    n2 = 128
    return n // n2, n2


@functools.lru_cache(maxsize=None)
def _fft_matrices_np(n):
    n1, n2 = _fft_split(n)
    h = n1 // 2
    k1 = np.arange(n1)
    bb = np.arange(n2)
    tw = np.exp(-2j * np.pi * np.outer(bb, k1) / n)
    f1 = np.exp(-2j * np.pi * np.outer(k1, np.arange(n1)) / n1)
    la = np.stack([_stack_complex(tw[b][:, None] * f1[:, :h]) for b in range(n2)])
    lf = np.stack([np.concatenate([(tw[b][:, None] * f1).real, (tw[b][:, None] * f1).imag], axis=0)
                   for b in range(n2)])
    f2 = np.exp(-2j * np.pi * np.outer(np.arange(n2), np.arange(n2)) / n2)
    lc = _stack_complex(f2)
    lci = _stack_complex(np.conj(f2))
    f1i = np.exp(2j * np.pi * np.outer(np.arange(h), k1) / n1)
    lai = np.stack([_stack_complex(f1i * np.conj(tw[b])[None, :] / n) for b in range(n2)])
    return la, lf, lc, lci, lai


FFT_UNROLL = 8
FFT_LANES = 256
FFT_MID_CHUNK = 512


SUBLANES = 8


def _shortconv_bmajor_kernel(u_ref, prev_ref, next_ref, w_ref, b_ref, o_ref):
    g = pl.program_id(1)
    ng = pl.num_programs(1)
    u = u_ref[...]
    rows = u.shape[0]
    n2 = o_ref.shape[0]
    t = lax.broadcasted_iota(jnp.int32, u.shape, 0)
    prev_row = jnp.where(g > 0, prev_ref[SUBLANES - 1:SUBLANES, :], 0.0)
    next_row = jnp.where(g < ng - 1, next_ref[0:1, :], 0.0)
    before = jnp.where(t == 0, prev_row, pltpu.roll(u, 1, 0))
    after = jnp.where(t == rows - 1, next_row, pltpu.roll(u, rows - 1, 0))
    y = before * w_ref[0:1, :] + u * w_ref[1:2, :] + after * w_ref[2:3, :] + b_ref[...]
    for al in range(rows // n2):
        o_ref[:, al, :] = y[al * n2:(al + 1) * n2, :]


def _shortconv_bmajor(u, w, bias, n2, tc=1024):
    b, t, n = u.shape
    half = t // n2
    rows = SUBLANES * n2
    ng = t // rows
    nj = n // tc
    per = rows // SUBLANES
    halo = lambda f: pl.BlockSpec((None, SUBLANES, tc), lambda bi, g, j: (bi, f(g), j))
    return pl.pallas_call(
        _shortconv_bmajor_kernel,
        out_shape=jax.ShapeDtypeStruct((b, n2, half, n), F32),
        grid=(b, ng, nj),
        in_specs=[pl.BlockSpec((None, rows, tc), lambda bi, g, j: (bi, g, j)),
                  halo(lambda g: jnp.maximum(g * per - 1, 0)),
                  halo(lambda g: jnp.minimum((g + 1) * per, ng * per - 1)),
                  pl.BlockSpec((3, tc), lambda bi, g, j: (0, j)),
                  pl.BlockSpec((1, tc), lambda bi, g, j: (0, j))],
        out_specs=pl.BlockSpec((None, n2, SUBLANES, tc), lambda bi, g, j: (bi, 0, g, j)),
        compiler_params=_cp("arbitrary", "arbitrary", "arbitrary"),
        name="short_conv_bmajor",
    )(u, u, u, w, bias.reshape(1, n))


def _fft_a_kernel(z_ref, m_hi, m_lo, t_ref, *, packed):
    n2, n1 = t_ref.shape[1], t_ref.shape[2]

    def body(b, carry):
        rows = jnp.concatenate([z_ref[0, b], z_ref[1, b]], axis=0) if packed else z_ref[b]
        hi, lo = _split(rows)
        out = _dot3(m_hi[b], m_lo[b], hi, lo)
        t_ref[0, b] = out[:n1]
        t_ref[1, b] = out[n1:]
        return carry
    lax.fori_loop(0, n2, body, 0, unroll=FFT_UNROLL)


def _fft_a(z, col, width, mats, packed):
    m_hi, m_lo = mats
    n2, rows2, k = m_hi.shape
    n1 = rows2 // 2
    td = FFT_LANES
    nblk = width // td
    lead = z.shape[0]
    if packed:
        in_spec = pl.BlockSpec((2, n2, k // 2, td), lambda p, j: (p, 0, 0, col * nblk + j))
        groups = lead // 2
    else:
        in_spec = pl.BlockSpec((None, n2, k, td), lambda p, j: (p, 0, 0, j))
        groups = lead
    c3 = lambda p, j: (0, 0, 0)
    return pl.pallas_call(
        functools.partial(_fft_a_kernel, packed=packed),
        out_shape=jax.ShapeDtypeStruct((groups, 2, n2, n1, width), F32),
        grid=(groups, nblk),
        in_specs=[in_spec, _single(m_hi.shape, c3), _single(m_lo.shape, c3)],
        out_specs=_single((None, 2, n2, n1, td), lambda p, j: (p, 0, 0, 0, j)),
        compiler_params=_cp("arbitrary", "arbitrary"),
        name="fft_strided_stage",
    )(z, m_hi, m_lo)


def _fft_mid_kernel(t_ref, *refs, with_filter):
    if with_filter:
        h_ref, lc_hi, lc_lo, lci_hi, lci_lo, o_ref = refs
    else:
        lc_hi, lc_lo, o_ref = refs
    n2 = t_ref.shape[1]
    for kk in range(t_ref.shape[2]):
        hi, lo = _split(jnp.concatenate([t_ref[0, :, kk, :], t_ref[1, :, kk, :]], axis=0))
        x = _dot3(lc_hi[...], lc_lo[...], hi, lo)
        if with_filter:
            xr, xi = x[:n2], x[n2:]
            hr, hi_ = h_ref[0, kk], h_ref[1, kk]
            y_hi, y_lo = _split(jnp.concatenate([xr * hr - xi * hi_, xr * hi_ + xi * hr], axis=0))
            x = _dot3(lci_hi[...], lci_lo[...], y_hi, y_lo)
        o_ref[0, kk] = x[:n2]
        o_ref[1, kk] = x[n2:]


def _fft_mid(t, mats, spec=None, order=0):
    g, _, n2, n1, width = t.shape
    tw = FFT_MID_CHUNK
    c2 = lambda kg, j, p: (0, 0)
    in_specs = [pl.BlockSpec((None, 2, n2, SUBLANES, tw), lambda kg, j, p: (p, 0, 0, kg, j))]
    args = [t]
    if spec is not None:
        in_specs.append(pl.BlockSpec((None, 2, SUBLANES, n2, tw), lambda kg, j, p: (order, 0, kg, 0, j)))
        args.append(spec)
    in_specs += [pl.BlockSpec(m.shape, c2) for m in mats]
    return pl.pallas_call(
        functools.partial(_fft_mid_kernel, with_filter=spec is not None),
        out_shape=jax.ShapeDtypeStruct((g, 2, n1, n2, width), F32),
        grid=(n1 // SUBLANES, width // tw, g),
        in_specs=in_specs,
        out_specs=pl.BlockSpec((None, 2, SUBLANES, n2, tw), lambda kg, j, p: (p, 0, kg, 0, j)),
        compiler_params=_cp("arbitrary", "arbitrary", "arbitrary"),
        name="fft_contiguous_stage",
    )(*args, *mats)


def _fft_inv_kernel(u_ref, z_ref, g_ref, skip_ref, m_hi, m_lo, o_ref, *, natural_out):
    cols = u_ref.shape[2]
    half = z_ref.shape[2]
    b0 = pl.program_id(1) * cols
    skip = skip_ref[...]
    for c in range(cols):
        hi, lo = _split(jnp.concatenate([u_ref[0, :, c, :], u_ref[1, :, c, :]], axis=0))
        out = _dot3(m_hi[b0 + c], m_lo[b0 + c], hi, lo)
        for plane in range(2):
            y = out[plane * half:(plane + 1) * half]
            val = (g_ref[plane, c] * (y + z_ref[plane, c] * skip)).astype(o_ref.dtype)
            if natural_out:
                o_ref[plane, :, c, :] = val
            else:
                o_ref[plane, c] = val


def _fft_inv(u, z, z_col, g, g_col, skip, order, mats, natural_out):
    m_hi, m_lo = mats
    pairs, _, n1, n2, width = u.shape
    half = n1 // 2
    cols = SUBLANES
    sig = lambda col: pl.BlockSpec((2, cols, half, width), lambda p, i: (p, i, 0, col))
    c3 = lambda p, i: (0, 0, 0)
    if natural_out:
        out_shape = jax.ShapeDtypeStruct((2 * pairs, half, n2, width), BF16)
        out_spec = pl.BlockSpec((2, half, cols, width), lambda p, i: (p, 0, i, 0))
    else:
        out_shape = jax.ShapeDtypeStruct((2 * pairs, n2, half, width), F32)
        out_spec = pl.BlockSpec((2, cols, half, width), lambda p, i: (p, i, 0, 0))
    out = pl.pallas_call(
        functools.partial(_fft_inv_kernel, natural_out=natural_out),
        out_shape=out_shape,
        grid=(pairs, n2 // cols),
        in_specs=[pl.BlockSpec((None, 2, n1, cols, width), lambda p, i: (p, 0, 0, i, 0)),
                  sig(z_col), sig(g_col),
                  pl.BlockSpec((None, 1, width), lambda p, i: (order, 0, 0)),
                  _single(m_hi.shape, c3), _single(m_lo.shape, c3)],
        out_specs=out_spec,
        compiler_params=_cp("arbitrary", "arbitrary"),
        name="fft_inverse_strided_stage",
    )(u, z, g, skip.reshape(HY_ORDER, 1, width), m_hi, m_lo)
    return out.reshape(2 * pairs, half * n2, width) if natural_out else out


def _fft_long_conv(usc, filt_p, skip):
    order, n2, n1, d = filt_p.shape
    la, lf, lc, lci, lai = _fft_matrices_np(n1 * n2)
    fwd = (*_hi_lo(lc),)
    both = (*_hi_lo(lc), *_hi_lo(lci))
    spec = _fft_mid(_fft_a(filt_p, 0, d, _hi_lo(lf), packed=False), fwd)
    la_m, lai_m = _hi_lo(la), _hi_lo(lai)
    z = usc
    for o in range(order):
        t = _fft_a(z, 0, d, la_m, packed=True)
        u = _fft_mid(t, both, spec, o)
        z = _fft_inv(u, z, 0, usc, o + 1, skip, o, lai_m, natural_out=o + 1 == order)
    return z


@functools.lru_cache(maxsize=None)
def _dense_dft_np(seq):
    n = 2 * seq
    k = np.arange(n)
    f = np.exp(-2j * np.pi * np.outer(k, k) / n)
    fwd = _stack_complex(f[:, :seq])
    flt = np.concatenate([f.real, f.imag], axis=0)
    inv = _stack_complex(np.conj(f)[:seq, :] / n)
    return fwd, flt, inv


def _dense_spectrum_kernel(f_ref, m_hi, m_lo, h_ref):
    hi, lo = _split(f_ref[...])
    out = _dot3(m_hi[...], m_lo[...], hi, lo)
    n = f_ref.shape[0]
    h_ref[0] = out[:n]
    h_ref[1] = out[n:]


def _dense_spectrum(filt, td=256):
    order, n, d = filt.shape
    _, flt, _ = _dense_dft_np(n // 2)
    m_hi, m_lo = _hi_lo(flt)
    c2 = lambda o, j: (0, 0)
    return pl.pallas_call(
        _dense_spectrum_kernel,
        out_shape=jax.ShapeDtypeStruct((order, 2, n, d), F32),
        grid=(order, d // td),
        in_specs=[pl.BlockSpec((None, n, td), lambda o, j: (o, 0, j)),
                  pl.BlockSpec(m_hi.shape, c2), pl.BlockSpec(m_lo.shape, c2)],
        out_specs=pl.BlockSpec((None, 2, n, td), lambda o, j: (o, 0, 0, j)),
        compiler_params=_cp("arbitrary", "arbitrary"),
        name="filter_spectrum_dense",
    )(filt, m_hi, m_lo)


def _dense_conv_kernel(z_ref, g_ref, h_ref, skip_ref, f_hi, f_lo, i_hi, i_lo, o_ref):
    seq = z_ref.shape[1]
    n = 2 * seq
    zz = jnp.concatenate([z_ref[0], z_ref[1]], axis=0)
    hi, lo = _split(zz)
    x = _dot3(f_hi[...], f_lo[...], hi, lo)
    xr, xi = x[:n], x[n:]
    hr, hi_ = h_ref[0], h_ref[1]
    y = jnp.concatenate([xr * hr - xi * hi_, xr * hi_ + xi * hr], axis=0)
    y_hi, y_lo = _split(y)
    w = _dot3(i_hi[...], i_lo[...], y_hi, y_lo)
    skip = skip_ref[...]
    for plane in range(2):
        y_p = w[plane * seq:(plane + 1) * seq]
        o_ref[plane] = (g_ref[plane] * (y_p + z_ref[plane] * skip)).astype(o_ref.dtype)


def _dense_conv(z_arr, z_col, g_arr, g_col, spec, order, skip, out_dtype, td=256):
    b, seq, _ = z_arr.shape
    d = spec.shape[3]
    n = 2 * seq
    nblk = d // td
    fwd, _, inv = _dense_dft_np(seq)
    mats = [*_hi_lo(fwd), *_hi_lo(inv)]
    c2 = lambda j, p: (0, 0)
    return pl.pallas_call(
        _dense_conv_kernel,
        out_shape=jax.ShapeDtypeStruct((b, seq, d), out_dtype),
        grid=(nblk, b // 2),
        in_specs=[pl.BlockSpec((2, seq, td), lambda j, p: (p, 0, z_col * nblk + j)),
                  pl.BlockSpec((2, seq, td), lambda j, p: (p, 0, g_col * nblk + j)),
                  pl.BlockSpec((None, 2, n, td), lambda j, p: (order, 0, 0, j)),
                  pl.BlockSpec((None, 1, td), lambda j, p: (order, 0, j))]
                 + [pl.BlockSpec(m.shape, c2) for m in mats],
        out_specs=pl.BlockSpec((2, seq, td), lambda j, p: (p, 0, j)),
        compiler_params=_cp("arbitrary", "arbitrary"),
        name="dense_conv",
    )(z_arr, g_arr, spec, skip.reshape(HY_ORDER, 1, d), *mats)


DENSE_DFT_MAX_SEQ = 512


def _hyena(x, gain, shift, scale, p):
    b, seq, d = x.shape
    u = _nm_matmul(x, gain, shift, scale, p["w_in"], F32, tn=1024)
    if seq <= DENSE_DFT_MAX_SEQ:
        u = _shortconv(u, p["conv_w"], p["conv_b"])
        spec = _dense_spectrum(_hyena_filter(seq, *p["filter"]))
        z1 = _dense_conv(u, 0, u, 1, spec, 0, p["skip"], F32)
        return _dense_conv(z1, 0, u, 2, spec, 1, p["skip"], BF16)
    _, n2 = _fft_split(2 * seq)
    usc = _shortconv_bmajor(u, p["conv_w"], p["conv_b"], n2)
    filt_p = _hyena_filter(seq, *p["filter"], bmajor_rows=n2)
    return _fft_long_conv(usc, filt_p, p["skip"])


def _rope_tables(seq):
    t = np.arange(seq)
    pos = np.stack([t // GRID_W, t % GRID_W], axis=1).astype(np.float32)
    n = 16
    inv = jnp.asarray(ROPE_THETA, F32) ** (-jnp.arange(n, dtype=F32) / n)
    lane = np.arange(LANES)
    axis = (lane % 64) // 32
    idx = lane % 16
    sign = np.where(lane % 32 < 16, -1.0, 1.0).astype(np.float32)
    ang = jnp.asarray(pos)[:, axis] * inv[idx][None, :]
    return jnp.cos(ang), jnp.sin(ang) * sign[None, :]


def _identity_tables(seq):
    return jnp.ones((seq, LANES), F32), jnp.zeros((seq, LANES), F32)


def _mla_weights(w_dq, w_uq, w_dkv, w_ukv):
    d = w_dq.shape[0]
    hh = MLA_HEADS
    wd = jnp.concatenate([w_dq, w_dkv, jnp.zeros((d, LANES - MLA_ROPE), F32)], axis=1).astype(BF16)
    uq = w_uq.reshape(MLA_RANK, hh, MLA_NOPE + MLA_ROPE)
    uq = jnp.pad(uq, ((0, 0), (0, 0), (0, MLA_HEAD_PAD - MLA_NOPE - MLA_ROPE)))
    ukv = w_ukv.reshape(MLA_RANK, hh, MLA_NOPE + MLA_V)
    ukv = jnp.concatenate([ukv[:, :, :MLA_NOPE].reshape(MLA_RANK, hh * MLA_NOPE),
                           ukv[:, :, MLA_NOPE:].reshape(MLA_RANK, hh * MLA_V)], axis=1)
    return wd, uq.reshape(MLA_RANK, hh * MLA_HEAD_PAD).astype(BF16), ukv.astype(BF16)


def _cast_tile_kernel(w_ref, o_ref, *, valid, axis):
    w = w_ref[...]
    idx = pl.program_id(2) * FFN_TILE + lax.broadcasted_iota(jnp.int32, w.shape, axis)
    o_ref[...] = jnp.where(idx < valid, w, 0.0).astype(o_ref.dtype)


def _cast_tile(w, axis):
    depth, two, r, c = w.shape
    valid = w.shape[axis]
    nj = -(-valid // FFN_TILE)
    if axis == 3:
        blk, in_map = (r, FFN_TILE), (lambda l, k, j: (l, k, 0, j))
    else:
        blk, in_map = (FFN_TILE, c), (lambda l, k, j: (l, k, j, 0))
    return pl.pallas_call(
        functools.partial(_cast_tile_kernel, valid=valid, axis=axis - 2),
        out_shape=jax.ShapeDtypeStruct((depth, two, nj) + blk, BF16),
        grid=(depth, two, nj),
        in_specs=[pl.BlockSpec((None, None) + blk, in_map)],
        out_specs=pl.BlockSpec((None, None, None) + blk, lambda l, k, j: (l, k, j, 0, 0)),
        compiler_params=_cp("arbitrary", "arbitrary", "arbitrary"),
        name="cast_tile",
    )(w)


def _to_heads(a, heads):
    b, t, _ = a.shape
    return a.reshape(b, t, heads, -1).transpose(0, 2, 1, 3)


def _from_heads(a):
    b, h, t, e = a.shape
    return a.transpose(0, 2, 1, 3).reshape(b, t, h * e)


def kernel(x, c, ctx, c_ctx, mod_w, mod_b, norm_g, final_g, ffn_wg, ffn_wu, ffn_wd, mla_w_dq, mla_g_q, mla_w_uq, mla_w_dkv, mla_g_kv, mla_w_ukv, mla_w_o, hy_w_in, hy_conv_w, hy_conv_b, hy_f_w1, hy_f_b1, hy_f_w2, hy_f_b2, hy_f_w3, hy_f_b3, hy_f_freq, hy_f_w4, hy_decay, hy_skip, hy_w_out, win_w_qkv, win_sink, win_w_o):
    b, s, d = x.shape
    n_ctx = ctx.shape[1]
    depth = mod_w.shape[0]
    assert b % 2 == 0 and s % 512 == 0 and n_ctx % 256 == 0

    cc = jnp.concatenate([c, c_ctx[None, :], jnp.zeros((8 - (b + 1) % 8, d), F32)], axis=0)
    mods = _modulation(cc, mod_w, mod_b).reshape(depth, cc.shape[0], N_MOD, d)

    ffn_w = (_cast_tile(ffn_wg, 3), _cast_tile(ffn_wu, 3), _cast_tile(ffn_wd, 2))
    cos_l, sin_l = _rope_tables(s)
    cos_c, sin_c = _identity_tables(n_ctx)

    for i in range(depth):
        kind, j = i % N_MIXERS, i // N_MIXERS
        need_ctx = i < depth - 1
        ctx_live = need_ctx or kind != 1
        ml = [mods[i, :b, k][:, None, :] for k in range(N_MOD)]
        mc = [jnp.broadcast_to(mods[i, b, k][None, None, :], (b, 1, d)) for k in range(N_MOD)]

        x = _ffn(x, norm_g[i, 0], ml[0], ml[1], ml[2], *ffn_w, i, 0)
        if ctx_live:
            ctx = _ffn(ctx.reshape(1, b * n_ctx, d), norm_g[i, 0], mc[0][:1], mc[1][:1], mc[2][:1],
                       *ffn_w, i, 0).reshape(b, n_ctx, d)

        g1 = norm_g[i, 1]
        o_c = None
        if kind == 0:
            wd, wuq, wukv = _mla_weights(mla_w_dq[j], mla_w_uq[j], mla_w_dkv[j], mla_w_ukv[j])
            q_l, k_l, v_l = _mla_proj(x, g1, ml[3], ml[4], wd, mla_g_q[j], mla_g_kv[j], wuq, wukv, cos_l, sin_l)
            q_c, k_c, v_c = _mla_proj(ctx, g1, mc[3], mc[4], wd, mla_g_q[j], mla_g_kv[j], wuq, wukv, cos_c, sin_c)
            w_o = mla_w_o[j].astype(BF16)
            o_l = _attention(q_l, [k_c, k_l], [v_c, v_l])
            if need_ctx:
                o_c = _attention(q_c, [k_c], [v_c])
        elif kind == 1:
            hp = dict(w_in=hy_w_in[j].astype(BF16), conv_w=hy_conv_w[j], conv_b=hy_conv_b[j],
                      filter=(hy_f_w1[j], hy_f_b1[j], hy_f_w2[j], hy_f_b2[j], hy_f_w3[j], hy_f_b3[j],
                              hy_f_freq[j], hy_f_w4[j], hy_decay[j]),
                      skip=hy_skip[j])
            w_o = hy_w_out[j].astype(BF16)
            o_l = _hyena(x, g1, ml[3], ml[4], hp)
            if need_ctx:
                o_c = _hyena(ctx, g1, mc[3], mc[4], hp)
        else:
            qw = WIN_Q_HEADS * WIN_HEAD_DIM
            kw = WIN_KV_HEADS * WIN_HEAD_DIM
            w_qkv = win_w_qkv[j].astype(BF16)
            w_o = win_w_o[j].astype(BF16)
            q_chunks = qw // LANES
            rope_l = (cos_l, sin_l, (qw + kw) // LANES, q_chunks, WIN_HEAD_DIM ** -0.5)
            rope_c = (cos_c, sin_c, (qw + kw) // LANES, q_chunks, WIN_HEAD_DIM ** -0.5)
            qkv_l = _nm_matmul(x, g1, ml[3], ml[4], w_qkv, BF16, tn=qw + 2 * kw, rope=rope_l)
            qkv_c = _nm_matmul(ctx, g1, mc[3], mc[4], w_qkv, BF16, tn=qw + 2 * kw, rope=rope_c)
            o_l = _window_attention(qkv_l, qkv_c, win_sink[j])
            if need_ctx:
                q_c = _to_heads(qkv_c[..., :qw], WIN_Q_HEADS)
                k_c = _to_heads(qkv_c[..., qw:qw + kw], WIN_KV_HEADS)
                v_c = _to_heads(qkv_c[..., qw + kw:], WIN_KV_HEADS)
                o_c = _from_heads(_sink_attention(q_c, k_c, v_c, win_sink[j]))

        x = _mm_res(o_l, w_o, x, ml[5])
        x = _ffn(x, norm_g[i, 2], ml[6], ml[7], ml[8], *ffn_w, i, 1)
        if need_ctx:
            ctx = _mm_res(o_c, w_o, ctx, mc[5])
            ctx = _ffn(ctx.reshape(1, b * n_ctx, d), norm_g[i, 2], mc[6][:1], mc[7][:1], mc[8][:1],
                       *ffn_w, i, 1).reshape(b, n_ctx, d)
    return _final_norm(x, final_g)
```

```python
import functools
import math

import numpy as np
import jax
import jax.numpy as jnp
from jax import lax
from jax.experimental import pallas as pl
from jax.experimental.pallas import tpu as pltpu

F32 = jnp.float32
BF16 = jnp.bfloat16

GRID_W = 64
N_MOD = 9
FFN_RES = 0.5
NORM_EPS = 1e-6
ROPE_THETA = 10000.0
N_MIXERS = 3
MLA_HEADS = 16
MLA_NOPE = 128
MLA_ROPE = 64
MLA_V = 128
MLA_RANK = 512
MLA_HEAD_PAD = 256
HY_ORDER = 2
HY_EMB = 33
HY_BANDS = (HY_EMB - 1) // 2
HY_FILTER_W = 64
WIN_HEAD_DIM = 64
WIN_Q_HEADS = 32
WIN_KV_HEADS = 4
WIN_GROUP = WIN_Q_HEADS // WIN_KV_HEADS
WINDOW = 128
Q_BLOCK = 128

LANES = 128
V7X_VMEM_LIMIT = 56 * 1024 * 1024
FFN_TILE = 512
NEG_BIG = -1e30


def _cp(*sem):
    return pltpu.CompilerParams(dimension_semantics=sem, vmem_limit_bytes=V7X_VMEM_LIMIT)


def _single(shape, imap):
    return pl.BlockSpec(shape, imap, pipeline_mode=pl.Buffered(1))


def _dot(a, b):
    return jnp.dot(a, b, preferred_element_type=F32)


def _dot_nt(a, b):
    return lax.dot_general(a, b, (((1,), (1,)), ((), ())), preferred_element_type=F32)


def _split(x):
    hi = x.astype(BF16)
    lo = (x - hi.astype(F32)).astype(BF16)
    return hi, lo


def _dot3(a_hi, a_lo, b_hi, b_lo):
    return _dot(a_hi, b_hi) + (_dot(a_hi, b_lo) + _dot(a_lo, b_hi))


def _modnorm(x, gain, shift, scale):
    r = lax.rsqrt(jnp.mean(x * x, axis=-1, keepdims=True) + NORM_EPS)
    return (x * r * gain) * (1.0 + scale) + shift


def _rms(x, gain):
    return x * lax.rsqrt(jnp.mean(x * x, axis=-1, keepdims=True) + NORM_EPS) * gain


def _rope128(x, cos, sin):
    lane = lax.broadcasted_iota(jnp.int32, x.shape, 1)
    partner = jnp.where(lane % 32 < 16, pltpu.roll(x, LANES - 16, 1), pltpu.roll(x, 16, 1))
    return x * cos + partner * sin


def _mod_kernel(a_ref, w_ref, b_ref, o_ref):
    a = a_ref[...]
    a = a * jax.nn.sigmoid(a)
    a_hi, a_lo = _split(a)
    w_hi, w_lo = _split(w_ref[...])
    o_ref[...] = _dot3(a_hi, a_lo, w_hi, w_lo) + b_ref[...]


def _modulation(cc, mod_w, mod_b):
    depth, d, n = mod_w.shape
    rows = cc.shape[0]
    tn = n // 16
    return pl.pallas_call(
        _mod_kernel,
        out_shape=jax.ShapeDtypeStruct((depth, rows, n), F32),
        grid=(depth, n // tn),
        in_specs=[
            pl.BlockSpec((rows, d), lambda l, j: (0, 0)),
            pl.BlockSpec((None, d, tn), lambda l, j: (l, 0, j)),
            pl.BlockSpec((None, 1, tn), lambda l, j: (l, 0, j)),
        ],
        out_specs=pl.BlockSpec((None, rows, tn), lambda l, j: (l, 0, j)),
        compiler_params=_cp("arbitrary", "arbitrary"),
        name="modulation",
    )(cc, mod_w, mod_b.reshape(depth, 1, n))


def _ffn_kernel(x_ref, g_ref, sh_ref, sc_ref, gt_ref, wg_ref, wu_ref, wd_ref, o_ref, h_scr, acc_scr):
    j = pl.program_id(2)

    @pl.when(j == 0)
    def _():
        h_scr[...] = _modnorm(x_ref[...], g_ref[...], sh_ref[...], sc_ref[...]).astype(BF16)
        acc_scr[...] = jnp.zeros_like(acc_scr)

    h = h_scr[...]
    g = _dot(h, wg_ref[...])
    u = _dot(h, wu_ref[...])
    a = (g * jax.nn.sigmoid(g) * u).astype(BF16)
    acc_scr[...] += _dot(a, wd_ref[...])

    @pl.when(j == pl.num_programs(2) - 1)
    def _():
        o_ref[...] = x_ref[...] + FFN_RES * gt_ref[...] * acc_scr[...]


def _ffn(x, gain, shift, scale, gate, wg, wu, wd, layer, slot, tm=512):
    b, t, d = x.shape
    nj = wg.shape[2]
    tm = min(tm, t)
    vec = pl.BlockSpec((None, 1, d), lambda bi, i, j: (bi, 0, 0))
    return pl.pallas_call(
        _ffn_kernel,
        out_shape=jax.ShapeDtypeStruct((b, t, d), F32),
        grid=(b, t // tm, nj),
        in_specs=[
            pl.BlockSpec((None, tm, d), lambda bi, i, j: (bi, i, 0)),
            pl.BlockSpec((1, d), lambda bi, i, j: (0, 0)),
            vec, vec, vec,
            pl.BlockSpec((None, None, None, d, FFN_TILE), lambda bi, i, j: (layer, slot, j, 0, 0)),
            pl.BlockSpec((None, None, None, d, FFN_TILE), lambda bi, i, j: (layer, slot, j, 0, 0)),
            pl.BlockSpec((None, None, None, FFN_TILE, d), lambda bi, i, j: (layer, slot, j, 0, 0)),
        ],
        out_specs=pl.BlockSpec((None, tm, d), lambda bi, i, j: (bi, i, 0)),
        scratch_shapes=[pltpu.VMEM((tm, d), BF16), pltpu.VMEM((tm, d), F32)],
        compiler_params=_cp("arbitrary", "arbitrary", "arbitrary"),
        name="ffn",
    )(x, gain.reshape(1, d), shift, scale, gate, wg, wu, wd)


def _nm_matmul_kernel(x_ref, g_ref, sh_ref, sc_ref, w_ref, *rest, rope_chunks, scaled_chunks, out_scale):
    if rope_chunks:
        cos_ref, sin_ref, o_ref, h_scr = rest
    else:
        o_ref, h_scr = rest

    @pl.when(pl.program_id(2) == 0)
    def _():
        h_scr[...] = _modnorm(x_ref[...], g_ref[...], sh_ref[...], sc_ref[...]).astype(BF16)

    y = _dot(h_scr[...], w_ref[...])
    if not rope_chunks:
        o_ref[...] = y.astype(o_ref.dtype)
        return
    cos = cos_ref[...]
    sin = sin_ref[...]
    for c in range(y.shape[1] // LANES):
        yc = y[:, c * LANES:(c + 1) * LANES]
        if c < rope_chunks:
            yc = _rope128(yc, cos, sin)
        if c < scaled_chunks:
            yc = yc * out_scale
        o_ref[:, c * LANES:(c + 1) * LANES] = yc.astype(o_ref.dtype)


def _nm_matmul(x, gain, shift, scale, w, out_dtype, tn, tm=512, rope=None):
    b, t, d = x.shape
    n = w.shape[1]
    tm = min(tm, t)
    vec = pl.BlockSpec((None, 1, d), lambda bi, i, j: (bi, 0, 0))
    in_specs = [
        pl.BlockSpec((None, tm, d), lambda bi, i, j: (bi, i, 0)),
        pl.BlockSpec((1, d), lambda bi, i, j: (0, 0)),
        vec, vec,
        pl.BlockSpec((d, tn), lambda bi, i, j: (0, j)),
    ]
    args = [x, gain.reshape(1, d), shift, scale, w]
    rope_chunks = scaled_chunks = 0
    out_scale = 1.0
    if rope is not None:
        cos, sin, rope_chunks, scaled_chunks, out_scale = rope
        in_specs += [pl.BlockSpec((tm, LANES), lambda bi, i, j: (i, 0))] * 2
        args += [cos, sin]
    return pl.pallas_call(
        functools.partial(_nm_matmul_kernel, rope_chunks=rope_chunks, scaled_chunks=scaled_chunks,
                          out_scale=out_scale),
        out_shape=jax.ShapeDtypeStruct((b, t, n), out_dtype),
        grid=(b, t // tm, n // tn),
        in_specs=in_specs,
        out_specs=pl.BlockSpec((None, tm, tn), lambda bi, i, j: (bi, i, j)),
        scratch_shapes=[pltpu.VMEM((tm, d), BF16)],
        compiler_params=_cp("arbitrary", "arbitrary", "arbitrary"),
        name="norm_matmul",
    )(*args)


def _mm_res_kernel(a_ref, w_ref, r_ref, gt_ref, o_ref):
    o_ref[...] = r_ref[...] + gt_ref[...] * _dot(a_ref[...], w_ref[...])


def _mm_res(a, w, res, gate, tm=512):
    b, t, k = a.shape
    n = w.shape[1]
    tm = min(tm, t)
    return pl.pallas_call(
        _mm_res_kernel,
        out_shape=jax.ShapeDtypeStruct((b, t, n), F32),
        grid=(b, t // tm),
        in_specs=[
            pl.BlockSpec((None, tm, k), lambda bi, i: (bi, i, 0)),
            pl.BlockSpec((k, n), lambda bi, i: (0, 0)),
            pl.BlockSpec((None, tm, n), lambda bi, i: (bi, i, 0)),
            pl.BlockSpec((None, 1, n), lambda bi, i: (bi, 0, 0)),
        ],
        out_specs=pl.BlockSpec((None, tm, n), lambda bi, i: (bi, i, 0)),
        compiler_params=_cp("arbitrary", "arbitrary"),
        name="proj_residual",
    )(a, w, res, gate)


def _final_norm_kernel(x_ref, g_ref, o_ref):
    o_ref[...] = _rms(x_ref[...], g_ref[...])


def _final_norm(x, gain, tm=512):
    b, t, d = x.shape
    return pl.pallas_call(
        _final_norm_kernel,
        out_shape=jax.ShapeDtypeStruct((b, t, d), F32),
        grid=(b, t // tm),
        in_specs=[pl.BlockSpec((None, tm, d), lambda bi, i: (bi, i, 0)),
                  pl.BlockSpec((1, d), lambda bi, i: (0, 0))],
        out_specs=pl.BlockSpec((None, tm, d), lambda bi, i: (bi, i, 0)),
        compiler_params=_cp("arbitrary", "arbitrary"),
        name="final_norm",
    )(x, gain.reshape(1, d))


def _mla_proj_kernel(x_ref, g_ref, sh_ref, sc_ref, wd_ref, gq_ref, gkv_ref, wuq_ref, wukv_ref, cos_ref, sin_ref,
                     q_ref, k_ref, v_ref, *, q_scale):
    h = _modnorm(x_ref[...], g_ref[...], sh_ref[...], sc_ref[...]).astype(BF16)
    a = _dot(h, wd_ref[...])
    cq = _rms(a[:, :MLA_RANK], gq_ref[...]).astype(BF16)
    ckv = _rms(a[:, MLA_RANK:2 * MLA_RANK], gkv_ref[...]).astype(BF16)
    cos = cos_ref[...]
    sin = sin_ref[...]
    kr = _rope128(a[:, 2 * MLA_RANK:], cos, sin).astype(BF16)
    q = _dot(cq, wuq_ref[...])
    kv = _dot(ckv, wukv_ref[...])
    for hd in range(MLA_HEADS):
        base = hd * MLA_HEAD_PAD
        q_ref[hd, :, :LANES] = (q[:, base:base + LANES] * q_scale).astype(BF16)
        q_ref[hd, :, LANES:] = (_rope128(q[:, base + LANES:base + 2 * LANES], cos, sin) * q_scale).astype(BF16)
        k_ref[hd, :, :LANES] = kv[:, hd * MLA_NOPE:(hd + 1) * MLA_NOPE].astype(BF16)
        k_ref[hd, :, LANES:] = kr
        voff = MLA_HEADS * MLA_NOPE + hd * MLA_V
        v_ref[hd] = kv[:, voff:voff + MLA_V].T.astype(BF16)


def _mla_proj(x, gain, shift, scale, wd, gq, gkv, wuq, wukv, cos, sin, tm=256):
    b, t, d = x.shape
    tm = min(tm, t)
    vec = pl.BlockSpec((None, 1, d), lambda bi, i: (bi, 0, 0))
    const = lambda bi, i: (0, 0)
    head_out = lambda w: pl.BlockSpec((None, MLA_HEADS, tm, w), lambda bi, i: (bi, 0, i, 0))
    return pl.pallas_call(
        functools.partial(_mla_proj_kernel, q_scale=(MLA_NOPE + MLA_ROPE) ** -0.5 * math.log2(math.e)),
        out_shape=[jax.ShapeDtypeStruct((b, MLA_HEADS, t, MLA_HEAD_PAD), BF16),
                   jax.ShapeDtypeStruct((b, MLA_HEADS, t, MLA_HEAD_PAD), BF16),
                   jax.ShapeDtypeStruct((b, MLA_HEADS, MLA_V, t), BF16)],
        grid=(b, t // tm),
        in_specs=[
            pl.BlockSpec((None, tm, d), lambda bi, i: (bi, i, 0)),
            pl.BlockSpec((1, d), const),
            vec, vec,
            _single(wd.shape, const),
            pl.BlockSpec((1, MLA_RANK), const),
            pl.BlockSpec((1, MLA_RANK), const),
            _single(wuq.shape, const),
            _single(wukv.shape, const),
            pl.BlockSpec((tm, LANES), lambda bi, i: (i, 0)),
            pl.BlockSpec((tm, LANES), lambda bi, i: (i, 0)),
        ],
        out_specs=[head_out(MLA_HEAD_PAD), head_out(MLA_HEAD_PAD),
                   pl.BlockSpec((None, MLA_HEADS, MLA_V, tm), lambda bi, i: (bi, 0, 0, i))],
        compiler_params=_cp("arbitrary", "arbitrary"),
        name="mla_proj",
    )(x, gain.reshape(1, d), shift, scale, wd, gq.reshape(1, -1), gkv.reshape(1, -1), wuq, wukv, cos, sin)


ATTN_CHUNK = 256
ATTN_SUB = 256


def _attn_kernel(q_ref, *refs):
    o_ref, s_scr = refs[-2:]
    nseg = (len(refs) - 2) // 2
    k_refs, vt_refs = refs[:nseg], refs[nseg:2 * nseg]
    tq = q_ref.shape[0]
    sub = s_scr.shape[2]
    for u in range(tq // sub):
        q = q_ref[u * sub:(u + 1) * sub, :]
        base = 0
        for k_ref in k_refs:
            for c in range(k_ref.shape[0] // ATTN_CHUNK):
                rows = slice(c * ATTN_CHUNK, (c + 1) * ATTN_CHUNK)
                s_scr[u, base + rows.start:base + rows.stop, :] = _dot_nt(k_ref[rows, :], q)
            base += k_ref.shape[0]
    for u in range(tq // sub):
        s = s_scr[u]
        m = jnp.max(s, axis=0, keepdims=True)
        p = jnp.exp2(s - m)
        l = jnp.sum(p, axis=0, keepdims=True)
        pb = p.astype(BF16)
        ot = None
        base = 0
        for vt_ref in vt_refs:
            part = _dot(vt_ref[...], pb[base:base + vt_ref.shape[1], :])
            ot = part if ot is None else ot + part
            base += vt_ref.shape[1]
        o_ref[u * sub:(u + 1) * sub, :] = (ot / l).T.astype(o_ref.dtype)


def _attention(q, ks, vts, tq=1024):
    b, h, t, dk = q.shape
    dv = vts[0].shape[2]
    tk = sum(k.shape[2] for k in ks)
    tq = min(tq, t)
    sub = min(ATTN_SUB, tq)
    whole = lambda a: pl.BlockSpec((None, None) + a.shape[2:], lambda bi, hi, i: (bi, hi, 0, 0))
    return pl.pallas_call(
        _attn_kernel,
        out_shape=jax.ShapeDtypeStruct((b, t, h * dv), BF16),
        grid=(b, h, t // tq),
        in_specs=[pl.BlockSpec((None, None, tq, dk), lambda bi, hi, i: (bi, hi, i, 0))]
                 + [whole(k) for k in ks] + [whole(v) for v in vts],
        out_specs=pl.BlockSpec((None, tq, dv), lambda bi, hi, i: (bi, i, hi)),
        scratch_shapes=[pltpu.VMEM((tq // sub, tk, sub), F32)],
        compiler_params=_cp("arbitrary", "arbitrary", "arbitrary"),
        name="attention",
    )(q, *ks, *vts)


def _sink_attn_kernel(sink_ref, q_ref, k_ref, v_ref, o_ref):
    group = q_ref.shape[0]
    k = k_ref[...]
    v = v_ref[...]
    for g in range(group):
        s = _dot_nt(q_ref[g], k)
        sk = sink_ref[pl.program_id(1) * group + g]
        m = jnp.maximum(jnp.max(s, axis=1, keepdims=True), sk)
        p = jnp.exp(s - m)
        l = jnp.sum(p, axis=1, keepdims=True) + jnp.exp(sk - m)
        o_ref[g] = (_dot(p.astype(BF16), v) / l).astype(o_ref.dtype)


def _sink_attention(q, k, v, sink):
    b, hq, t, d = q.shape
    hk = k.shape[1]
    group = hq // hk
    kv_spec = pl.BlockSpec((None, None, t, d), lambda bi, h: (bi, h, 0, 0))
    q_spec = pl.BlockSpec((None, group, t, d), lambda bi, h: (bi, h, 0, 0))
    return pl.pallas_call(
        _sink_attn_kernel,
        out_shape=jax.ShapeDtypeStruct((b, hq, t, d), BF16),
        grid=(b, hk),
        in_specs=[pl.BlockSpec(memory_space=pltpu.SMEM), q_spec, kv_spec, kv_spec],
        out_specs=q_spec,
        compiler_params=_cp("arbitrary", "arbitrary"),
        name="sink_attention",
    )(sink, q, k, v)


def _window_kernel(sink_ref, bias_ref, qt_ref, kp_ref, kc_ref, kn_ref, kx_ref, vp_ref, vc_ref, vn_ref, vx_ref, o_ref):
    hk = pl.program_id(1)
    kk = jnp.concatenate([kx_ref[...], kp_ref[...], kc_ref[...], kn_ref[...]], axis=0)
    vt = jnp.concatenate([vx_ref[...], vp_ref[...], vc_ref[...], vn_ref[...]], axis=1)
    bias = bias_ref[...]
    s = _dot(kk, qt_ref[...])
    s = jnp.concatenate([s[:, g * Q_BLOCK:(g + 1) * Q_BLOCK] + bias for g in range(WIN_GROUP)], axis=1)
    sk = jnp.concatenate([jnp.full((1, Q_BLOCK), sink_ref[hk * WIN_GROUP + g], F32) for g in range(WIN_GROUP)],
                         axis=1)
    m = jnp.maximum(jnp.max(s, axis=0, keepdims=True), sk)
    e = jnp.exp(s - m)
    l = jnp.sum(e, axis=0, keepdims=True) + jnp.exp(sk - m)
    o_ref[...] = (_dot(vt, e.astype(BF16)) / l).astype(o_ref.dtype)


def _window_bias(n_ctx):
    c = np.arange(n_ctx + 3 * Q_BLOCK)[:, None]
    r = np.arange(Q_BLOCK)[None, :]
    band = (c < n_ctx) | (np.abs(c - n_ctx - WINDOW - r) <= WINDOW)
    in_prev = (c >= n_ctx) & (c < n_ctx + Q_BLOCK)
    in_next = c >= n_ctx + 2 * Q_BLOCK
    ok = np.stack([band & ~in_prev, band, band & ~in_next])
    return jnp.asarray(np.where(ok, 0.0, NEG_BIG).astype(np.float32))


def _window_attention(qkv, qkv_ctx, sink):
    b, s, _ = qkv.shape
    n_ctx = qkv_ctx.shape[1]
    hd, hq, hk, grp = WIN_HEAD_DIM, WIN_Q_HEADS, WIN_KV_HEADS, WIN_GROUP
    qw, kw = hq * hd, hk * hd
    nblk = s // Q_BLOCK
    assert nblk >= 2 and WINDOW == Q_BLOCK
    qt = qkv[..., :qw].reshape(b, nblk, Q_BLOCK, hk, grp, hd).transpose(0, 3, 1, 5, 4, 2)
    qt = qt.reshape(b, hk, nblk, hd, grp * Q_BLOCK)
    k = _to_heads(qkv[..., qw:qw + kw], hk)
    kx = _to_heads(qkv_ctx[..., qw:qw + kw], hk)
    vt = qkv[..., qw + kw:].reshape(b, s, hk, hd).transpose(0, 2, 3, 1)
    vxt = qkv_ctx[..., qw + kw:].reshape(b, n_ctx, hk, hd).transpose(0, 2, 3, 1)

    prev = lambda i: jnp.maximum(i - 1, 0)
    nxt = lambda i: jnp.minimum(i + 1, nblk - 1)
    kblk = lambda f: pl.BlockSpec((None, None, Q_BLOCK, hd), lambda bi, h, i: (bi, h, f(i), 0))
    vblk = lambda f: pl.BlockSpec((None, None, hd, Q_BLOCK), lambda bi, h, i: (bi, h, 0, f(i)))
    same = lambda i: i
    edge = lambda bi, h, i: (jnp.where(i == 0, 0, jnp.where(i == nblk - 1, 2, 1)), 0, 0)
    qo_spec = pl.BlockSpec((None, None, None, hd, grp * Q_BLOCK), lambda bi, h, i: (bi, h, i, 0, 0))
    ot = pl.pallas_call(
        _window_kernel,
        out_shape=jax.ShapeDtypeStruct((b, hk, nblk, hd, grp * Q_BLOCK), BF16),
        grid=(b, hk, nblk),
        in_specs=[pl.BlockSpec(memory_space=pltpu.SMEM),
                  pl.BlockSpec((None, n_ctx + 3 * Q_BLOCK, Q_BLOCK), edge),
                  qo_spec,
                  kblk(prev), kblk(same), kblk(nxt),
                  pl.BlockSpec((None, None, n_ctx, hd), lambda bi, h, i: (bi, h, 0, 0)),
                  vblk(prev), vblk(same), vblk(nxt),
                  pl.BlockSpec((None, None, hd, n_ctx), lambda bi, h, i: (bi, h, 0, 0))],
        out_specs=qo_spec,
        compiler_params=_cp("arbitrary", "arbitrary", "arbitrary"),
        name="window_attention",
    )(sink, _window_bias(n_ctx), qt, k, k, k, kx, vt, vt, vt, vxt)
    o = ot.reshape(b, hk, nblk, hd, grp, Q_BLOCK).transpose(0, 2, 5, 1, 4, 3)
    return o.reshape(b, s, qw)


def _shortconv_kernel(u_ref, w_ref, b_ref, o_ref):
    u = u_ref[...]
    rows = u.shape[0]
    t = lax.broadcasted_iota(jnp.int32, u.shape, 0)
    before = jnp.where(t == 0, 0.0, pltpu.roll(u, 1, 0))
    after = jnp.where(t == rows - 1, 0.0, pltpu.roll(u, rows - 1, 0))
    o_ref[...] = before * w_ref[0:1, :] + u * w_ref[1:2, :] + after * w_ref[2:3, :] + b_ref[...]


def _shortconv(u, w, bias, tc=256):
    b, t, n = u.shape
    return pl.pallas_call(
        _shortconv_kernel,
        out_shape=jax.ShapeDtypeStruct((b, t, n), F32),
        grid=(b, n // tc),
        in_specs=[pl.BlockSpec((None, t, tc), lambda bi, j: (bi, 0, j)),
                  pl.BlockSpec((3, tc), lambda bi, j: (0, j)),
                  pl.BlockSpec((1, tc), lambda bi, j: (0, j))],
        out_specs=pl.BlockSpec((None, t, tc), lambda bi, j: (bi, 0, j)),
        compiler_params=_cp("arbitrary", "arbitrary"),
        name="short_conv",
    )(u, w, bias.reshape(1, n))


def _filter_kernel(z_ref, t_ref, w1_ref, b1_ref, w2_ref, b2_ref, w3_ref, b3_ref, fr_ref, w4_ref, dec_ref, o_ref, *,
                   zero_row):
    def dense(hv, w_ref):
        a_hi, a_lo = _split(hv)
        w_hi, w_lo = _split(w_ref[...])
        return _dot3(a_hi, a_lo, w_hi, w_lo)

    freq = fr_ref[...]
    hv = jnp.sin(freq * (dense(z_ref[...], w1_ref) + b1_ref[...]))
    hv = jnp.sin(freq * (dense(hv, w2_ref) + b2_ref[...]))
    hv = jnp.sin(freq * (dense(hv, w3_ref) + b3_ref[...]))
    out = dense(hv, w4_ref) * jnp.exp(-t_ref[...] * jnp.abs(dec_ref[...]))
    tr = out.shape[0]
    row = pl.program_id(1) * tr + lax.broadcasted_iota(jnp.int32, out.shape, 0)
    out = jnp.where(row == zero_row, 0.0, out)
    if len(o_ref.shape) == 2:
        o_ref[...] = out
    else:
        n2 = o_ref.shape[0]
        for al in range(o_ref.shape[1]):
            o_ref[:, al, :] = out[al * n2:(al + 1) * n2, :]


def _hyena_filter(seq, f_w1, f_b1, f_w2, f_b2, f_w3, f_b3, f_freq, f_w4, decay, bmajor_rows=None):
    d = f_w4.shape[1] // (2 * HY_ORDER)
    n = 2 * seq
    t = np.linspace(0.0, 1.0, seq, dtype=np.float32)[:, None]
    w = (2.0 * math.pi * np.arange(seq, dtype=np.float32)[:, None] / seq).astype(np.float32)
    f = np.linspace(1e-4, HY_BANDS - 1, HY_BANDS, dtype=np.float32)[None, :]
    z = np.concatenate([t, np.cos(f * w), -np.sin(f * w)], axis=-1).astype(np.float32)
    lag = np.arange(n)
    lag = np.where(lag < seq, lag, n - lag) % seq
    zc = np.zeros((n, LANES), np.float32)
    zc[:, :HY_EMB] = z[lag]
    tc = t[lag]
    w1 = jnp.zeros((LANES, HY_FILTER_W), F32).at[:HY_EMB].set(f_w1)
    tr = SUBLANES * bmajor_rows if bmajor_rows else min(512, seq)
    fw = HY_FILTER_W
    if bmajor_rows:
        out_shape = jax.ShapeDtypeStruct((HY_ORDER, bmajor_rows, n // bmajor_rows, d), F32)
        out_spec = pl.BlockSpec((None, bmajor_rows, SUBLANES, d), lambda o, i: (o, 0, i, 0))
    else:
        out_shape = jax.ShapeDtypeStruct((HY_ORDER, n, d), F32)
        out_spec = pl.BlockSpec((None, tr, d), lambda o, i: (o, i, 0))
    row = lambda v: v.reshape(1, -1)
    const = lambda o, i: (0, 0)
    col = lambda o, i: (0, 2 * o + (i * tr) // seq)
    return pl.pallas_call(
        functools.partial(_filter_kernel, zero_row=seq),
        out_shape=out_shape,
        grid=(HY_ORDER, n // tr),
        in_specs=[pl.BlockSpec((tr, LANES), lambda o, i: (i, 0)),
                  pl.BlockSpec((tr, 1), lambda o, i: (i, 0)),
                  pl.BlockSpec((LANES, fw), const), pl.BlockSpec((1, fw), const),
                  pl.BlockSpec((fw, fw), const), pl.BlockSpec((1, fw), const),
                  pl.BlockSpec((fw, fw), const), pl.BlockSpec((1, fw), const),
                  pl.BlockSpec((1, fw), const),
                  pl.BlockSpec((fw, d), col),
                  pl.BlockSpec((1, d), col)],
        out_specs=out_spec,
        compiler_params=_cp("arbitrary", "arbitrary"),
        name="hyena_filter",
    )(jnp.asarray(zc), jnp.asarray(tc), w1, row(f_b1), f_w2, row(f_b2), f_w3, row(f_b3), row(f_freq), f_w4,
      row(decay))


def _stack_complex(c):
    return np.block([[c.real, -c.imag], [c.imag, c.real]])


def _hi_lo(m):
    m = jnp.asarray(np.asarray(m, np.float32))
    hi = m.astype(BF16)
    lo = (m - hi.astype(F32)).astype(BF16)
    return hi, lo


def _fft_split(n):
    n2 = 128
    return n // n2, n2


@functools.lru_cache(maxsize=None)
def _fft_matrices_np(n):
    n1, n2 = _fft_split(n)
    h = n1 // 2
    k1 = np.arange(n1)
    bb = np.arange(n2)
    tw = np.exp(-2j * np.pi * np.outer(bb, k1) / n)
    f1 = np.exp(-2j * np.pi * np.outer(k1, np.arange(n1)) / n1)
    la = np.stack([_stack_complex(tw[b][:, None] * f1[:, :h]) for b in range(n2)])
    lf = np.stack([np.concatenate([(tw[b][:, None] * f1).real, (tw[b][:, None] * f1).imag], axis=0)
                   for b in range(n2)])
    f2 = np.exp(-2j * np.pi * np.outer(np.arange(n2), np.arange(n2)) / n2)
    lc = _stack_complex(f2)
    lci = _stack_complex(np.conj(f2))
    f1i = np.exp(2j * np.pi * np.outer(np.arange(h), k1) / n1)
    lai = np.stack([_stack_complex(f1i * np.conj(tw[b])[None, :] / n) for b in range(n2)])
    return la, lf, lc, lci, lai


FFT_UNROLL = 8
FFT_LANES = 256
FFT_MID_CHUNK = 512


SUBLANES = 8


def _hyena_in_kernel(x_ref, xp_ref, xn_ref, g_ref, sh_ref, sc_ref, w_ref, cw_ref, cb_ref, o_ref, h_scr, e_scr):
    g = pl.program_id(1)
    ng = pl.num_programs(1)

    @pl.when(pl.program_id(2) == 0)
    def _():
        gain, shift, scale = g_ref[...], sh_ref[...], sc_ref[...]
        h_scr[...] = _modnorm(x_ref[...], gain, shift, scale).astype(BF16)
        e_scr[:SUBLANES] = _modnorm(xp_ref[...], gain, shift, scale)
        e_scr[SUBLANES:] = _modnorm(xn_ref[...], gain, shift, scale)

    w = w_ref[...]
    u = _dot(h_scr[...], w)
    edge = _dot(e_scr[...].astype(BF16), w)
    rows = u.shape[0]
    n2 = o_ref.shape[0]
    t = lax.broadcasted_iota(jnp.int32, u.shape, 0)
    prev_row = jnp.where(g > 0, edge[SUBLANES - 1:SUBLANES, :], 0.0)
    next_row = jnp.where(g < ng - 1, edge[SUBLANES:SUBLANES + 1, :], 0.0)
    before = jnp.where(t == 0, prev_row, pltpu.roll(u, 1, 0))
    after = jnp.where(t == rows - 1, next_row, pltpu.roll(u, rows - 1, 0))
    y = before * cw_ref[0:1, :] + u * cw_ref[1:2, :] + after * cw_ref[2:3, :] + cb_ref[...]
    for al in range(rows // n2):
        o_ref[:, al, :] = y[al * n2:(al + 1) * n2, :]


def _hyena_in(x, gain, shift, scale, w, conv_w, conv_b, n2, tn=512):
    b, t, d = x.shape
    n = w.shape[1]
    half = t // n2
    rows = SUBLANES * n2
    ng = t // rows
    per = rows // SUBLANES
    halo = lambda f: pl.BlockSpec((None, SUBLANES, d), lambda bi, g, j: (bi, f(g), 0))
    vec = pl.BlockSpec((None, 1, d), lambda bi, g, j: (bi, 0, 0))
    return pl.pallas_call(
        _hyena_in_kernel,
        out_shape=jax.ShapeDtypeStruct((b, n2, half, n), F32),
        grid=(b, ng, n // tn),
        in_specs=[pl.BlockSpec((None, rows, d), lambda bi, g, j: (bi, g, 0)),
                  halo(lambda g: jnp.maximum(g * per - 1, 0)),
                  halo(lambda g: jnp.minimum((g + 1) * per, ng * per - 1)),
                  pl.BlockSpec((1, d), lambda bi, g, j: (0, 0)),
                  vec, vec,
                  pl.BlockSpec((d, tn), lambda bi, g, j: (0, j)),
                  pl.BlockSpec((3, tn), lambda bi, g, j: (0, j)),
                  pl.BlockSpec((1, tn), lambda bi, g, j: (0, j))],
        out_specs=pl.BlockSpec((None, n2, SUBLANES, tn), lambda bi, g, j: (bi, 0, g, j)),
        scratch_shapes=[pltpu.VMEM((rows, d), BF16), pltpu.VMEM((2 * SUBLANES, d), F32)],
        compiler_params=_cp("arbitrary", "arbitrary", "arbitrary"),
        name="hyena_in_proj",
    )(x, x, x, gain.reshape(1, d), shift, scale, w, conv_w, conv_b.reshape(1, n))


FFT_A_SPLIT = 2


def _fft_a_kernel(z_ref, m_hi, m_lo, t_ref, *, packed):
    nb, n1 = t_ref.shape[1], t_ref.shape[2]
    b0 = pl.program_id(2) * nb

    def body(b, carry):
        rows = jnp.concatenate([z_ref[0, b], z_ref[1, b]], axis=0) if packed else z_ref[b]
        hi, lo = _split(rows)
        out = _dot3(m_hi[b0 + b], m_lo[b0 + b], hi, lo)
        t_ref[0, b] = out[:n1]
        t_ref[1, b] = out[n1:]
        return carry
    lax.fori_loop(0, nb, body, 0, unroll=FFT_UNROLL)


def _fft_a(z, col, width, mats, packed):
    m_hi, m_lo = mats
    n2, rows2, k = m_hi.shape
    n1 = rows2 // 2
    td = FFT_LANES
    nblk = width // td
    nb = n2 // FFT_A_SPLIT
    lead = z.shape[0]
    if packed:
        in_spec = pl.BlockSpec((2, nb, k // 2, td), lambda p, j, i: (p, i, 0, col * nblk + j))
        groups = lead // 2
    else:
        in_spec = pl.BlockSpec((None, nb, k, td), lambda p, j, i: (p, i, 0, j))
        groups = lead
    c3 = lambda p, j, i: (0, 0, 0)
    return pl.pallas_call(
        functools.partial(_fft_a_kernel, packed=packed),
        out_shape=jax.ShapeDtypeStruct((groups, 2, n2, n1, width), F32),
        grid=(groups, nblk, FFT_A_SPLIT),
        in_specs=[in_spec, _single(m_hi.shape, c3), _single(m_lo.shape, c3)],
        out_specs=pl.BlockSpec((None, 2, nb, n1, td), lambda p, j, i: (p, 0, i, 0, j)),
        compiler_params=_cp("arbitrary", "arbitrary", "arbitrary"),
        name="fft_strided_stage",
    )(z, m_hi, m_lo)


def _fft_mid_kernel(t_ref, *refs, with_filter):
    if with_filter:
        h_ref, lc_hi, lc_lo, lci_hi, lci_lo, o_ref = refs
    else:
        lc_hi, lc_lo, o_ref = refs
    n2 = t_ref.shape[1]
    for kk in range(t_ref.shape[2]):
        hi, lo = _split(jnp.concatenate([t_ref[0, :, kk, :], t_ref[1, :, kk, :]], axis=0))
        x = _dot3(lc_hi[...], lc_lo[...], hi, lo)
        if with_filter:
            xr, xi = x[:n2], x[n2:]
            hr, hi_ = h_ref[0, kk], h_ref[1, kk]
            y_hi, y_lo = _split(jnp.concatenate([xr * hr - xi * hi_, xr * hi_ + xi * hr], axis=0))
            x = _dot3(lci_hi[...], lci_lo[...], y_hi, y_lo)
        o_ref[0, kk] = x[:n2]
        o_ref[1, kk] = x[n2:]


def _fft_mid(t, mats, spec=None, order=0):
    g, _, n2, n1, width = t.shape
    tw = FFT_MID_CHUNK
    c2 = lambda kg, j, p: (0, 0)
    in_specs = [pl.BlockSpec((None, 2, n2, SUBLANES, tw), lambda kg, j, p: (p, 0, 0, kg, j))]
    args = [t]
    if spec is not None:
        in_specs.append(pl.BlockSpec((None, 2, SUBLANES, n2, tw), lambda kg, j, p: (order, 0, kg, 0, j)))
        args.append(spec)
    in_specs += [pl.BlockSpec(m.shape, c2) for m in mats]
    return pl.pallas_call(
        functools.partial(_fft_mid_kernel, with_filter=spec is not None),
        out_shape=jax.ShapeDtypeStruct((g, 2, n1, n2, width), F32),
        grid=(n1 // SUBLANES, width // tw, g),
        in_specs=in_specs,
        out_specs=pl.BlockSpec((None, 2, SUBLANES, n2, tw), lambda kg, j, p: (p, 0, kg, 0, j)),
        compiler_params=_cp("arbitrary", "arbitrary", "arbitrary"),
        name="fft_contiguous_stage",
    )(*args, *mats)


def _fft_inv_kernel(u_ref, z_ref, g_ref, skip_ref, m_hi, m_lo, o_ref, *, natural_out):
    cols = u_ref.shape[2]
    half = z_ref.shape[2]
    b0 = pl.program_id(1) * cols
    skip = skip_ref[...]
    for c in range(cols):
        hi, lo = _split(jnp.concatenate([u_ref[0, :, c, :], u_ref[1, :, c, :]], axis=0))
        out = _dot3(m_hi[b0 + c], m_lo[b0 + c], hi, lo)
        for plane in range(2):
            y = out[plane * half:(plane + 1) * half]
            val = (g_ref[plane, c] * (y + z_ref[plane, c] * skip)).astype(o_ref.dtype)
            if natural_out:
                o_ref[plane, :, c, :] = val
            else:
                o_ref[plane, c] = val


def _fft_inv(u, z, z_col, g, g_col, skip, order, mats, natural_out):
    m_hi, m_lo = mats
    pairs, _, n1, n2, width = u.shape
    half = n1 // 2
    cols = SUBLANES
    sig = lambda col: pl.BlockSpec((2, cols, half, width), lambda p, i: (p, i, 0, col))
    c3 = lambda p, i: (0, 0, 0)
    if natural_out:
        out_shape = jax.ShapeDtypeStruct((2 * pairs, half, n2, width), BF16)
        out_spec = pl.BlockSpec((2, half, cols, width), lambda p, i: (p, 0, i, 0))
    else:
        out_shape = jax.ShapeDtypeStruct((2 * pairs, n2, half, width), F32)
        out_spec = pl.BlockSpec((2, cols, half, width), lambda p, i: (p, i, 0, 0))
    out = pl.pallas_call(
        functools.partial(_fft_inv_kernel, natural_out=natural_out),
        out_shape=out_shape,
        grid=(pairs, n2 // cols),
        in_specs=[pl.BlockSpec((None, 2, n1, cols, width), lambda p, i: (p, 0, 0, i, 0)),
                  sig(z_col), sig(g_col),
                  pl.BlockSpec((None, 1, width), lambda p, i: (order, 0, 0)),
                  _single(m_hi.shape, c3), _single(m_lo.shape, c3)],
        out_specs=out_spec,
        compiler_params=_cp("arbitrary", "arbitrary"),
        name="fft_inverse_strided_stage",
    )(u, z, g, skip.reshape(HY_ORDER, 1, width), m_hi, m_lo)
    return out.reshape(2 * pairs, half * n2, width) if natural_out else out


def _fft_long_conv(usc, filt_p, skip):
    order, n2, n1, d = filt_p.shape
    la, lf, lc, lci, lai = _fft_matrices_np(n1 * n2)
    fwd = (*_hi_lo(lc),)
    both = (*_hi_lo(lc), *_hi_lo(lci))
    spec = _fft_mid(_fft_a(filt_p, 0, d, _hi_lo(lf), packed=False), fwd)
    la_m, lai_m = _hi_lo(la), _hi_lo(lai)
    z = usc
    for o in range(order):
        t = _fft_a(z, 0, d, la_m, packed=True)
        u = _fft_mid(t, both, spec, o)
        z = _fft_inv(u, z, 0, usc, o + 1, skip, o, lai_m, natural_out=o + 1 == order)
    return z


@functools.lru_cache(maxsize=None)
def _dense_dft_np(seq):
    n = 2 * seq
    k = np.arange(n)
    f = np.exp(-2j * np.pi * np.outer(k, k) / n)
    fwd = _stack_complex(f[:, :seq])
    flt = np.concatenate([f.real, f.imag], axis=0)
    inv = _stack_complex(np.conj(f)[:seq, :] / n)
    return fwd, flt, inv


def _dense_spectrum_kernel(f_ref, m_hi, m_lo, h_ref):
    hi, lo = _split(f_ref[...])
    out = _dot3(m_hi[...], m_lo[...], hi, lo)
    n = f_ref.shape[0]
    h_ref[0] = out[:n]
    h_ref[1] = out[n:]


def _dense_spectrum(filt, td=256):
    order, n, d = filt.shape
    _, flt, _ = _dense_dft_np(n // 2)
    m_hi, m_lo = _hi_lo(flt)
    c2 = lambda o, j: (0, 0)
    return pl.pallas_call(
        _dense_spectrum_kernel,
        out_shape=jax.ShapeDtypeStruct((order, 2, n, d), F32),
        grid=(order, d // td),
        in_specs=[pl.BlockSpec((None, n, td), lambda o, j: (o, 0, j)),
                  pl.BlockSpec(m_hi.shape, c2), pl.BlockSpec(m_lo.shape, c2)],
        out_specs=pl.BlockSpec((None, 2, n, td), lambda o, j: (o, 0, 0, j)),
        compiler_params=_cp("arbitrary", "arbitrary"),
        name="filter_spectrum_dense",
    )(filt, m_hi, m_lo)


def _dense_conv_kernel(z_ref, g_ref, h_ref, skip_ref, f_hi, f_lo, i_hi, i_lo, o_ref):
    seq = z_ref.shape[1]
    n = 2 * seq
    zz = jnp.concatenate([z_ref[0], z_ref[1]], axis=0)
    hi, lo = _split(zz)
    x = _dot3(f_hi[...], f_lo[...], hi, lo)
    xr, xi = x[:n], x[n:]
    hr, hi_ = h_ref[0], h_ref[1]
    y = jnp.concatenate([xr * hr - xi * hi_, xr * hi_ + xi * hr], axis=0)
    y_hi, y_lo = _split(y)
    w = _dot3(i_hi[...], i_lo[...], y_hi, y_lo)
    skip = skip_ref[...]
    for plane in range(2):
        y_p = w[plane * seq:(plane + 1) * seq]
        o_ref[plane] = (g_ref[plane] * (y_p + z_ref[plane] * skip)).astype(o_ref.dtype)


def _dense_conv(z_arr, z_col, g_arr, g_col, spec, order, skip, out_dtype, td=256):
    b, seq, _ = z_arr.shape
    d = spec.shape[3]
    n = 2 * seq
    nblk = d // td
    fwd, _, inv = _dense_dft_np(seq)
    mats = [*_hi_lo(fwd), *_hi_lo(inv)]
    c2 = lambda j, p: (0, 0)
    return pl.pallas_call(
        _dense_conv_kernel,
        out_shape=jax.ShapeDtypeStruct((b, seq, d), out_dtype),
        grid=(nblk, b // 2),
        in_specs=[pl.BlockSpec((2, seq, td), lambda j, p: (p, 0, z_col * nblk + j)),
                  pl.BlockSpec((2, seq, td), lambda j, p: (p, 0, g_col * nblk + j)),
                  pl.BlockSpec((None, 2, n, td), lambda j, p: (order, 0, 0, j)),
                  pl.BlockSpec((None, 1, td), lambda j, p: (order, 0, j))]
                 + [pl.BlockSpec(m.shape, c2) for m in mats],
        out_specs=pl.BlockSpec((2, seq, td), lambda j, p: (p, 0, j)),
        compiler_params=_cp("arbitrary", "arbitrary"),
        name="dense_conv",
    )(z_arr, g_arr, spec, skip.reshape(HY_ORDER, 1, d), *mats)


DENSE_DFT_MAX_SEQ = 512


def _hyena(x, gain, shift, scale, p):
    b, seq, d = x.shape
    if seq <= DENSE_DFT_MAX_SEQ:
        u = _nm_matmul(x, gain, shift, scale, p["w_in"], F32, tn=1024)
        u = _shortconv(u, p["conv_w"], p["conv_b"])
        spec = _dense_spectrum(_hyena_filter(seq, *p["filter"]))
        z1 = _dense_conv(u, 0, u, 1, spec, 0, p["skip"], F32)
        return _dense_conv(z1, 0, u, 2, spec, 1, p["skip"], BF16)
    _, n2 = _fft_split(2 * seq)
    usc = _hyena_in(x, gain, shift, scale, p["w_in"], p["conv_w"], p["conv_b"], n2)
    filt_p = _hyena_filter(seq, *p["filter"], bmajor_rows=n2)
    return _fft_long_conv(usc, filt_p, p["skip"])


def _rope_tables(seq):
    t = np.arange(seq)
    pos = np.stack([t // GRID_W, t % GRID_W], axis=1).astype(np.float32)
    n = 16
    inv = jnp.asarray(ROPE_THETA, F32) ** (-jnp.arange(n, dtype=F32) / n)
    lane = np.arange(LANES)
    axis = (lane % 64) // 32
    idx = lane % 16
    sign = np.where(lane % 32 < 16, -1.0, 1.0).astype(np.float32)
    ang = jnp.asarray(pos)[:, axis] * inv[idx][None, :]
    return jnp.cos(ang), jnp.sin(ang) * sign[None, :]


def _identity_tables(seq):
    return jnp.ones((seq, LANES), F32), jnp.zeros((seq, LANES), F32)


def _mla_weights(w_dq, w_uq, w_dkv, w_ukv):
    d = w_dq.shape[0]
    hh = MLA_HEADS
    wd = jnp.concatenate([w_dq, w_dkv, jnp.zeros((d, LANES - MLA_ROPE), F32)], axis=1).astype(BF16)
    uq = w_uq.reshape(MLA_RANK, hh, MLA_NOPE + MLA_ROPE)
    uq = jnp.pad(uq, ((0, 0), (0, 0), (0, MLA_HEAD_PAD - MLA_NOPE - MLA_ROPE)))
    ukv = w_ukv.reshape(MLA_RANK, hh, MLA_NOPE + MLA_V)
    ukv = jnp.concatenate([ukv[:, :, :MLA_NOPE].reshape(MLA_RANK, hh * MLA_NOPE),
                           ukv[:, :, MLA_NOPE:].reshape(MLA_RANK, hh * MLA_V)], axis=1)
    return wd, uq.reshape(MLA_RANK, hh * MLA_HEAD_PAD).astype(BF16), ukv.astype(BF16)


def _cast_tile_kernel(w_ref, o_ref, *, valid, axis):
    w = w_ref[...]
    idx = pl.program_id(2) * FFN_TILE + lax.broadcasted_iota(jnp.int32, w.shape, axis)
    o_ref[...] = jnp.where(idx < valid, w, 0.0).astype(o_ref.dtype)


def _cast_tile(w, axis):
    depth, two, r, c = w.shape
    valid = w.shape[axis]
    nj = -(-valid // FFN_TILE)
    if axis == 3:
        blk, in_map = (r, FFN_TILE), (lambda l, k, j: (l, k, 0, j))
    else:
        blk, in_map = (FFN_TILE, c), (lambda l, k, j: (l, k, j, 0))
    return pl.pallas_call(
        functools.partial(_cast_tile_kernel, valid=valid, axis=axis - 2),
        out_shape=jax.ShapeDtypeStruct((depth, two, nj) + blk, BF16),
        grid=(depth, two, nj),
        in_specs=[pl.BlockSpec((None, None) + blk, in_map)],
        out_specs=pl.BlockSpec((None, None, None) + blk, lambda l, k, j: (l, k, j, 0, 0)),
        compiler_params=_cp("arbitrary", "arbitrary", "arbitrary"),
        name="cast_tile",
    )(w)


def _to_heads(a, heads):
    b, t, _ = a.shape
    return a.reshape(b, t, heads, -1).transpose(0, 2, 1, 3)


def _from_heads(a):
    b, h, t, e = a.shape
    return a.transpose(0, 2, 1, 3).reshape(b, t, h * e)


def kernel(x, c, ctx, c_ctx, mod_w, mod_b, norm_g, final_g, ffn_wg, ffn_wu, ffn_wd, mla_w_dq, mla_g_q, mla_w_uq, mla_w_dkv, mla_g_kv, mla_w_ukv, mla_w_o, hy_w_in, hy_conv_w, hy_conv_b, hy_f_w1, hy_f_b1, hy_f_w2, hy_f_b2, hy_f_w3, hy_f_b3, hy_f_freq, hy_f_w4, hy_decay, hy_skip, hy_w_out, win_w_qkv, win_sink, win_w_o):
    b, s, d = x.shape
    n_ctx = ctx.shape[1]
    depth = mod_w.shape[0]
    assert b % 2 == 0 and s % 512 == 0 and n_ctx % 256 == 0

    cc = jnp.concatenate([c, c_ctx[None, :], jnp.zeros((8 - (b + 1) % 8, d), F32)], axis=0)
    mods = _modulation(cc, mod_w, mod_b).reshape(depth, cc.shape[0], N_MOD, d)

    ffn_w = (_cast_tile(ffn_wg, 3), _cast_tile(ffn_wu, 3), _cast_tile(ffn_wd, 2))
    cos_l, sin_l = _rope_tables(s)
    cos_c, sin_c = _identity_tables(n_ctx)

    for i in range(depth):
        kind, j = i % N_MIXERS, i // N_MIXERS
        need_ctx = i < depth - 1
        ctx_live = need_ctx or kind != 1
        ml = [mods[i, :b, k][:, None, :] for k in range(N_MOD)]
        mc = [jnp.broadcast_to(mods[i, b, k][None, None, :], (b, 1, d)) for k in range(N_MOD)]

        x = _ffn(x, norm_g[i, 0], ml[0], ml[1], ml[2], *ffn_w, i, 0)
        if ctx_live:
            ctx = _ffn(ctx.reshape(1, b * n_ctx, d), norm_g[i, 0], mc[0][:1], mc[1][:1], mc[2][:1],
                       *ffn_w, i, 0).reshape(b, n_ctx, d)

        g1 = norm_g[i, 1]
        o_c = None
        if kind == 0:
            wd, wuq, wukv = _mla_weights(mla_w_dq[j], mla_w_uq[j], mla_w_dkv[j], mla_w_ukv[j])
            q_l, k_l, v_l = _mla_proj(x, g1, ml[3], ml[4], wd, mla_g_q[j], mla_g_kv[j], wuq, wukv, cos_l, sin_l)
            q_c, k_c, v_c = _mla_proj(ctx, g1, mc[3], mc[4], wd, mla_g_q[j], mla_g_kv[j], wuq, wukv, cos_c, sin_c)
            w_o = mla_w_o[j].astype(BF16)
            o_l = _attention(q_l, [k_c, k_l], [v_c, v_l])
            if need_ctx:
                o_c = _attention(q_c, [k_c], [v_c])
        elif kind == 1:
            hp = dict(w_in=hy_w_in[j].astype(BF16), conv_w=hy_conv_w[j], conv_b=hy_conv_b[j],
                      filter=(hy_f_w1[j], hy_f_b1[j], hy_f_w2[j], hy_f_b2[j], hy_f_w3[j], hy_f_b3[j],
                              hy_f_freq[j], hy_f_w4[j], hy_decay[j]),
                      skip=hy_skip[j])
            w_o = hy_w_out[j].astype(BF16)
            o_l = _hyena(x, g1, ml[3], ml[4], hp)
            if need_ctx:
                o_c = _hyena(ctx, g1, mc[3], mc[4], hp)
        else:
            qw = WIN_Q_HEADS * WIN_HEAD_DIM
            kw = WIN_KV_HEADS * WIN_HEAD_DIM
            w_qkv = win_w_qkv[j].astype(BF16)
            w_o = win_w_o[j].astype(BF16)
            q_chunks = qw // LANES
            rope_l = (cos_l, sin_l, (qw + kw) // LANES, q_chunks, WIN_HEAD_DIM ** -0.5)
            rope_c = (cos_c, sin_c, (qw + kw) // LANES, q_chunks, WIN_HEAD_DIM ** -0.5)
            qkv_l = _nm_matmul(x, g1, ml[3], ml[4], w_qkv, BF16, tn=qw + 2 * kw, rope=rope_l)
            qkv_c = _nm_matmul(ctx, g1, mc[3], mc[4], w_qkv, BF16, tn=qw + 2 * kw, rope=rope_c)
            o_l = _window_attention(qkv_l, qkv_c, win_sink[j])
            if need_ctx:
                q_c = _to_heads(qkv_c[..., :qw], WIN_Q_HEADS)
                k_c = _to_heads(qkv_c[..., qw:qw + kw], WIN_KV_HEADS)
                v_c = _to_heads(qkv_c[..., qw + kw:], WIN_KV_HEADS)
                o_c = _from_heads(_sink_attention(q_c, k_c, v_c, win_sink[j]))

        x = _mm_res(o_l, w_o, x, ml[5])
        x = _ffn(x, norm_g[i, 2], ml[6], ml[7], ml[8], *ffn_w, i, 1)
        if need_ctx:
            ctx = _mm_res(o_c, w_o, ctx, mc[5])
            ctx = _ffn(ctx.reshape(1, b * n_ctx, d), norm_g[i, 2], mc[6][:1], mc[7][:1], mc[8][:1],
                       *ffn_w, i, 1).reshape(b, n_ctx, d)
    return _final_norm(x, final_g)
```

```python
import functools
import math

import numpy as np
import jax
import jax.numpy as jnp
from jax import lax
from jax.experimental import pallas as pl
from jax.experimental.pallas import tpu as pltpu

F32 = jnp.float32
BF16 = jnp.bfloat16

GRID_W = 64
N_MOD = 9
FFN_RES = 0.5
NORM_EPS = 1e-6
ROPE_THETA = 10000.0
N_MIXERS = 3
MLA_HEADS = 16
MLA_NOPE = 128
MLA_ROPE = 64
MLA_V = 128
MLA_RANK = 512
MLA_HEAD_PAD = 256
HY_ORDER = 2
HY_EMB = 33
HY_BANDS = (HY_EMB - 1) // 2
HY_FILTER_W = 64
WIN_HEAD_DIM = 64
WIN_Q_HEADS = 32
WIN_KV_HEADS = 4
WIN_GROUP = WIN_Q_HEADS // WIN_KV_HEADS
WINDOW = 128
Q_BLOCK = 128

LANES = 128
V7X_VMEM_LIMIT = 56 * 1024 * 1024
FFN_TILE = 512
NEG_BIG = -1e30


def _cp(*sem):
    return pltpu.CompilerParams(dimension_semantics=sem, vmem_limit_bytes=V7X_VMEM_LIMIT)


def _single(shape, imap):
    return pl.BlockSpec(shape, imap, pipeline_mode=pl.Buffered(1))


def _dot(a, b):
    return jnp.dot(a, b, preferred_element_type=F32)


def _dot_nt(a, b):
    return lax.dot_general(a, b, (((1,), (1,)), ((), ())), preferred_element_type=F32)


def _split(x):
    hi = x.astype(BF16)
    lo = (x - hi.astype(F32)).astype(BF16)
    return hi, lo


def _dot3(a_hi, a_lo, b_hi, b_lo):
    return _dot(a_hi, b_hi) + (_dot(a_hi, b_lo) + _dot(a_lo, b_hi))


def _modnorm(x, gain, shift, scale):
    r = lax.rsqrt(jnp.mean(x * x, axis=-1, keepdims=True) + NORM_EPS)
    return (x * r * gain) * (1.0 + scale) + shift


def _rms(x, gain):
    return x * lax.rsqrt(jnp.mean(x * x, axis=-1, keepdims=True) + NORM_EPS) * gain


def _rope128(x, cos, sin):
    lane = lax.broadcasted_iota(jnp.int32, x.shape, 1)
    partner = jnp.where(lane % 32 < 16, pltpu.roll(x, LANES - 16, 1), pltpu.roll(x, 16, 1))
    return x * cos + partner * sin


def _mod_kernel(a_ref, w_ref, b_ref, o_ref):
    a = a_ref[...]
    a = a * jax.nn.sigmoid(a)
    a_hi, a_lo = _split(a)
    w_hi, w_lo = _split(w_ref[...])
    o_ref[...] = _dot3(a_hi, a_lo, w_hi, w_lo) + b_ref[...]


def _modulation(cc, mod_w, mod_b):
    depth, d, n = mod_w.shape
    rows = cc.shape[0]
    tn = n // 16
    return pl.pallas_call(
        _mod_kernel,
        out_shape=jax.ShapeDtypeStruct((depth, rows, n), F32),
        grid=(depth, n // tn),
        in_specs=[
            pl.BlockSpec((rows, d), lambda l, j: (0, 0)),
            pl.BlockSpec((None, d, tn), lambda l, j: (l, 0, j)),
            pl.BlockSpec((None, 1, tn), lambda l, j: (l, 0, j)),
        ],
        out_specs=pl.BlockSpec((None, rows, tn), lambda l, j: (l, 0, j)),
        compiler_params=_cp("arbitrary", "arbitrary"),
        name="modulation",
    )(cc, mod_w, mod_b.reshape(depth, 1, n))


def _ffn_kernel(x_ref, g_ref, sh_ref, sc_ref, gt_ref, fg_ref, wg_ref, wu_ref, wd_ref, o_ref, h_scr, acc_scr, *,
                final_norm):
    j = pl.program_id(2)

    @pl.when(j == 0)
    def _():
        h_scr[...] = _modnorm(x_ref[...], g_ref[...], sh_ref[...], sc_ref[...]).astype(BF16)
        acc_scr[...] = jnp.zeros_like(acc_scr)

    h = h_scr[...]
    g = _dot(h, wg_ref[...])
    u = _dot(h, wu_ref[...])
    a = (g * jax.nn.sigmoid(g) * u).astype(BF16)
    acc_scr[...] += _dot(a, wd_ref[...])

    @pl.when(j == pl.num_programs(2) - 1)
    def _():
        y = x_ref[...] + FFN_RES * gt_ref[...] * acc_scr[...]
        o_ref[...] = _rms(y, fg_ref[...]) if final_norm else y


def _ffn(x, gain, shift, scale, gate, final_gain, wg, wu, wd, layer, slot, tm=512, final_norm=False):
    b, t, d = x.shape
    nj = wg.shape[2]
    tm = min(tm, t)
    vec = pl.BlockSpec((None, 1, d), lambda bi, i, j: (bi, 0, 0))
    row = pl.BlockSpec((1, d), lambda bi, i, j: (0, 0))
    return pl.pallas_call(
        functools.partial(_ffn_kernel, final_norm=final_norm),
        out_shape=jax.ShapeDtypeStruct((b, t, d), F32),
        grid=(b, t // tm, nj),
        in_specs=[
            pl.BlockSpec((None, tm, d), lambda bi, i, j: (bi, i, 0)),
            row,
            vec, vec, vec,
            row,
            pl.BlockSpec((None, None, None, d, FFN_TILE), lambda bi, i, j: (layer, slot, j, 0, 0)),
            pl.BlockSpec((None, None, None, d, FFN_TILE), lambda bi, i, j: (layer, slot, j, 0, 0)),
            pl.BlockSpec((None, None, None, FFN_TILE, d), lambda bi, i, j: (layer, slot, j, 0, 0)),
        ],
        out_specs=pl.BlockSpec((None, tm, d), lambda bi, i, j: (bi, i, 0)),
        scratch_shapes=[pltpu.VMEM((tm, d), BF16), pltpu.VMEM((tm, d), F32)],
        compiler_params=_cp("arbitrary", "arbitrary", "arbitrary"),
        name="ffn",
    )(x, gain.reshape(1, d), shift, scale, gate, final_gain.reshape(1, d), wg, wu, wd)


def _nm_matmul_kernel(x_ref, g_ref, sh_ref, sc_ref, w_ref, *rest, rope_chunks, scaled_chunks, out_scale):
    if rope_chunks:
        cos_ref, sin_ref, o_ref, h_scr = rest
    else:
        o_ref, h_scr = rest

    @pl.when(pl.program_id(2) == 0)
    def _():
        h_scr[...] = _modnorm(x_ref[...], g_ref[...], sh_ref[...], sc_ref[...]).astype(BF16)

    y = _dot(h_scr[...], w_ref[...])
    if not rope_chunks:
        o_ref[...] = y.astype(o_ref.dtype)
        return
    cos = cos_ref[...]
    sin = sin_ref[...]
    for c in range(y.shape[1] // LANES):
        yc = y[:, c * LANES:(c + 1) * LANES]
        if c < rope_chunks:
            yc = _rope128(yc, cos, sin)
        if c < scaled_chunks:
            yc = yc * out_scale
        o_ref[:, c * LANES:(c + 1) * LANES] = yc.astype(o_ref.dtype)


def _nm_matmul(x, gain, shift, scale, w, out_dtype, tn, tm=512, rope=None):
    b, t, d = x.shape
    n = w.shape[1]
    tm = min(tm, t)
    vec = pl.BlockSpec((None, 1, d), lambda bi, i, j: (bi, 0, 0))
    in_specs = [
        pl.BlockSpec((None, tm, d), lambda bi, i, j: (bi, i, 0)),
        pl.BlockSpec((1, d), lambda bi, i, j: (0, 0)),
        vec, vec,
        pl.BlockSpec((d, tn), lambda bi, i, j: (0, j)),
    ]
    args = [x, gain.reshape(1, d), shift, scale, w]
    rope_chunks = scaled_chunks = 0
    out_scale = 1.0
    if rope is not None:
        cos, sin, rope_chunks, scaled_chunks, out_scale = rope
        in_specs += [pl.BlockSpec((tm, LANES), lambda bi, i, j: (i, 0))] * 2
        args += [cos, sin]
    return pl.pallas_call(
        functools.partial(_nm_matmul_kernel, rope_chunks=rope_chunks, scaled_chunks=scaled_chunks,
                          out_scale=out_scale),
        out_shape=jax.ShapeDtypeStruct((b, t, n), out_dtype),
        grid=(b, t // tm, n // tn),
        in_specs=in_specs,
        out_specs=pl.BlockSpec((None, tm, tn), lambda bi, i, j: (bi, i, j)),
        scratch_shapes=[pltpu.VMEM((tm, d), BF16)],
        compiler_params=_cp("arbitrary", "arbitrary", "arbitrary"),
        name="norm_matmul",
    )(*args)


def _mm_res_kernel(a_ref, w_ref, r_ref, gt_ref, o_ref):
    o_ref[...] = r_ref[...] + gt_ref[...] * _dot(a_ref[...], w_ref[...])


def _mm_res(a, w, res, gate, tm=512):
    b, t, k = a.shape
    n = w.shape[1]
    tm = min(tm, t)
    return pl.pallas_call(
        _mm_res_kernel,
        out_shape=jax.ShapeDtypeStruct((b, t, n), F32),
        grid=(b, t // tm),
        in_specs=[
            pl.BlockSpec((None, tm, k), lambda bi, i: (bi, i, 0)),
            pl.BlockSpec((k, n), lambda bi, i: (0, 0)),
            pl.BlockSpec((None, tm, n), lambda bi, i: (bi, i, 0)),
            pl.BlockSpec((None, 1, n), lambda bi, i: (bi, 0, 0)),
        ],
        out_specs=pl.BlockSpec((None, tm, n), lambda bi, i: (bi, i, 0)),
        compiler_params=_cp("arbitrary", "arbitrary"),
        name="proj_residual",
    )(a, w, res, gate)


def _mla_proj_kernel(x_ref, g_ref, sh_ref, sc_ref, wd_ref, gq_ref, gkv_ref, wuq_ref, wukv_ref, cos_ref, sin_ref,
                     q_ref, k_ref, v_ref, *, q_scale):
    h = _modnorm(x_ref[...], g_ref[...], sh_ref[...], sc_ref[...]).astype(BF16)
    a = _dot(h, wd_ref[...])
    cq = _rms(a[:, :MLA_RANK], gq_ref[...]).astype(BF16)
    ckv = _rms(a[:, MLA_RANK:2 * MLA_RANK], gkv_ref[...]).astype(BF16)
    cos = cos_ref[...]
    sin = sin_ref[...]
    kr = _rope128(a[:, 2 * MLA_RANK:], cos, sin).astype(BF16)
    q = _dot(cq, wuq_ref[...])
    kv = _dot(ckv, wukv_ref[...])
    for hd in range(MLA_HEADS):
        base = hd * MLA_HEAD_PAD
        q_ref[hd, :, :LANES] = (q[:, base:base + LANES] * q_scale).astype(BF16)
        q_ref[hd, :, LANES:] = (_rope128(q[:, base + LANES:base + 2 * LANES], cos, sin) * q_scale).astype(BF16)
        k_ref[hd, :, :LANES] = kv[:, hd * MLA_NOPE:(hd + 1) * MLA_NOPE].astype(BF16)
        k_ref[hd, :, LANES:] = kr
        voff = MLA_HEADS * MLA_NOPE + hd * MLA_V
        v_ref[hd] = kv[:, voff:voff + MLA_V].T.astype(BF16)


def _mla_proj(x, gain, shift, scale, wd, gq, gkv, wuq, wukv, cos, sin, tm=256):
    b, t, d = x.shape
    tm = min(tm, t)
    vec = pl.BlockSpec((None, 1, d), lambda bi, i: (bi, 0, 0))
    const = lambda bi, i: (0, 0)
    head_out = lambda w: pl.BlockSpec((None, MLA_HEADS, tm, w), lambda bi, i: (bi, 0, i, 0))
    return pl.pallas_call(
        functools.partial(_mla_proj_kernel, q_scale=(MLA_NOPE + MLA_ROPE) ** -0.5 * math.log2(math.e)),
        out_shape=[jax.ShapeDtypeStruct((b, MLA_HEADS, t, MLA_HEAD_PAD), BF16),
                   jax.ShapeDtypeStruct((b, MLA_HEADS, t, MLA_HEAD_PAD), BF16),
                   jax.ShapeDtypeStruct((b, MLA_HEADS, MLA_V, t), BF16)],
        grid=(b, t // tm),
        in_specs=[
            pl.BlockSpec((None, tm, d), lambda bi, i: (bi, i, 0)),
            pl.BlockSpec((1, d), const),
            vec, vec,
            _single(wd.shape, const),
            pl.BlockSpec((1, MLA_RANK), const),
            pl.BlockSpec((1, MLA_RANK), const),
            _single(wuq.shape, const),
            _single(wukv.shape, const),
            pl.BlockSpec((tm, LANES), lambda bi, i: (i, 0)),
            pl.BlockSpec((tm, LANES), lambda bi, i: (i, 0)),
        ],
        out_specs=[head_out(MLA_HEAD_PAD), head_out(MLA_HEAD_PAD),
                   pl.BlockSpec((None, MLA_HEADS, MLA_V, tm), lambda bi, i: (bi, 0, 0, i))],
        compiler_params=_cp("arbitrary", "arbitrary"),
        name="mla_proj",
    )(x, gain.reshape(1, d), shift, scale, wd, gq.reshape(1, -1), gkv.reshape(1, -1), wuq, wukv, cos, sin)


ATTN_CHUNK = 256
ATTN_SUB = 256


def _attn_kernel(q_ref, *refs):
    o_ref, s_scr = refs[-2:]
    nseg = (len(refs) - 2) // 2
    k_refs, vt_refs = refs[:nseg], refs[nseg:2 * nseg]
    tq = q_ref.shape[0]
    sub = s_scr.shape[2]
    for u in range(tq // sub):
        q = q_ref[u * sub:(u + 1) * sub, :]
        base = 0
        for k_ref in k_refs:
            for c in range(k_ref.shape[0] // ATTN_CHUNK):
                rows = slice(c * ATTN_CHUNK, (c + 1) * ATTN_CHUNK)
                s_scr[u, base + rows.start:base + rows.stop, :] = _dot_nt(k_ref[rows, :], q)
            base += k_ref.shape[0]
    for u in range(tq // sub):
        s = s_scr[u]
        m = jnp.max(s, axis=0, keepdims=True)
        p = jnp.exp2(s - m)
        l = jnp.sum(p, axis=0, keepdims=True)
        pb = p.astype(BF16)
        ot = None
        base = 0
        for vt_ref in vt_refs:
            part = _dot(vt_ref[...], pb[base:base + vt_ref.shape[1], :])
            ot = part if ot is None else ot + part
            base += vt_ref.shape[1]
        o_ref[u * sub:(u + 1) * sub, :] = (ot / l).T.astype(o_ref.dtype)


def _attention(q, ks, vts, tq=1024):
    b, h, t, dk = q.shape
    dv = vts[0].shape[2]
    tk = sum(k.shape[2] for k in ks)
    tq = min(tq, t)
    sub = min(ATTN_SUB, tq)
    whole = lambda a: pl.BlockSpec((None, None) + a.shape[2:], lambda bi, hi, i: (bi, hi, 0, 0))
    return pl.pallas_call(
        _attn_kernel,
        out_shape=jax.ShapeDtypeStruct((b, t, h * dv), BF16),
        grid=(b, h, t // tq),
        in_specs=[pl.BlockSpec((None, None, tq, dk), lambda bi, hi, i: (bi, hi, i, 0))]
                 + [whole(k) for k in ks] + [whole(v) for v in vts],
        out_specs=pl.BlockSpec((None, tq, dv), lambda bi, hi, i: (bi, i, hi)),
        scratch_shapes=[pltpu.VMEM((tq // sub, tk, sub), F32)],
        compiler_params=_cp("arbitrary", "arbitrary", "arbitrary"),
        name="attention",
    )(q, *ks, *vts)


def _sink_attn_kernel(sink_ref, q_ref, k_ref, v_ref, o_ref):
    group = q_ref.shape[0]
    k = k_ref[...]
    v = v_ref[...]
    for g in range(group):
        s = _dot_nt(q_ref[g], k)
        sk = sink_ref[pl.program_id(1) * group + g]
        m = jnp.maximum(jnp.max(s, axis=1, keepdims=True), sk)
        p = jnp.exp(s - m)
        l = jnp.sum(p, axis=1, keepdims=True) + jnp.exp(sk - m)
        o_ref[g] = (_dot(p.astype(BF16), v) / l).astype(o_ref.dtype)


def _sink_attention(q, k, v, sink):
    b, hq, t, d = q.shape
    hk = k.shape[1]
    group = hq // hk
    kv_spec = pl.BlockSpec((None, None, t, d), lambda bi, h: (bi, h, 0, 0))
    q_spec = pl.BlockSpec((None, group, t, d), lambda bi, h: (bi, h, 0, 0))
    return pl.pallas_call(
        _sink_attn_kernel,
        out_shape=jax.ShapeDtypeStruct((b, hq, t, d), BF16),
        grid=(b, hk),
        in_specs=[pl.BlockSpec(memory_space=pltpu.SMEM), q_spec, kv_spec, kv_spec],
        out_specs=q_spec,
        compiler_params=_cp("arbitrary", "arbitrary"),
        name="sink_attention",
    )(sink, q, k, v)


def _window_kernel(sink_ref, bias_ref, qt_ref, kp_ref, kc_ref, kn_ref, kx_ref, vp_ref, vc_ref, vn_ref, vx_ref, o_ref):
    hk = pl.program_id(1)
    kk = jnp.concatenate([kx_ref[...], kp_ref[...], kc_ref[...], kn_ref[...]], axis=0)
    vt = jnp.concatenate([vx_ref[...], vp_ref[...], vc_ref[...], vn_ref[...]], axis=1)
    bias = bias_ref[...]
    s = _dot(kk, qt_ref[...])
    s = jnp.concatenate([s[:, g * Q_BLOCK:(g + 1) * Q_BLOCK] + bias for g in range(WIN_GROUP)], axis=1)
    sk = jnp.concatenate([jnp.full((1, Q_BLOCK), sink_ref[hk * WIN_GROUP + g], F32) for g in range(WIN_GROUP)],
                         axis=1)
    m = jnp.maximum(jnp.max(s, axis=0, keepdims=True), sk)
    e = jnp.exp(s - m)
    l = jnp.sum(e, axis=0, keepdims=True) + jnp.exp(sk - m)
    o_ref[...] = (_dot(vt, e.astype(BF16)) / l).astype(o_ref.dtype)


def _window_bias(n_ctx):
    c = np.arange(n_ctx + 3 * Q_BLOCK)[:, None]
    r = np.arange(Q_BLOCK)[None, :]
    band = (c < n_ctx) | (np.abs(c - n_ctx - WINDOW - r) <= WINDOW)
    in_prev = (c >= n_ctx) & (c < n_ctx + Q_BLOCK)
    in_next = c >= n_ctx + 2 * Q_BLOCK
    ok = np.stack([band & ~in_prev, band, band & ~in_next])
    return jnp.asarray(np.where(ok, 0.0, NEG_BIG).astype(np.float32))


def _window_attention(qkv, qkv_ctx, sink):
    b, s, _ = qkv.shape
    n_ctx = qkv_ctx.shape[1]
    hd, hq, hk, grp = WIN_HEAD_DIM, WIN_Q_HEADS, WIN_KV_HEADS, WIN_GROUP
    qw, kw = hq * hd, hk * hd
    nblk = s // Q_BLOCK
    assert nblk >= 2 and WINDOW == Q_BLOCK
    qt = qkv[..., :qw].reshape(b, nblk, Q_BLOCK, hk, grp, hd).transpose(0, 3, 1, 5, 4, 2)
    qt = qt.reshape(b, hk, nblk, hd, grp * Q_BLOCK)
    k = _to_heads(qkv[..., qw:qw + kw], hk)
    kx = _to_heads(qkv_ctx[..., qw:qw + kw], hk)
    vt = qkv[..., qw + kw:].reshape(b, s, hk, hd).transpose(0, 2, 3, 1)
    vxt = qkv_ctx[..., qw + kw:].reshape(b, n_ctx, hk, hd).transpose(0, 2, 3, 1)

    prev = lambda i: jnp.maximum(i - 1, 0)
    nxt = lambda i: jnp.minimum(i + 1, nblk - 1)
    kblk = lambda f: pl.BlockSpec((None, None, Q_BLOCK, hd), lambda bi, h, i: (bi, h, f(i), 0))
    vblk = lambda f: pl.BlockSpec((None, None, hd, Q_BLOCK), lambda bi, h, i: (bi, h, 0, f(i)))
    same = lambda i: i
    edge = lambda bi, h, i: (jnp.where(i == 0, 0, jnp.where(i == nblk - 1, 2, 1)), 0, 0)
    qo_spec = pl.BlockSpec((None, None, None, hd, grp * Q_BLOCK), lambda bi, h, i: (bi, h, i, 0, 0))
    ot = pl.pallas_call(
        _window_kernel,
        out_shape=jax.ShapeDtypeStruct((b, hk, nblk, hd, grp * Q_BLOCK), BF16),
        grid=(b, hk, nblk),
        in_specs=[pl.BlockSpec(memory_space=pltpu.SMEM),
                  pl.BlockSpec((None, n_ctx + 3 * Q_BLOCK, Q_BLOCK), edge),
                  qo_spec,
                  kblk(prev), kblk(same), kblk(nxt),
                  pl.BlockSpec((None, None, n_ctx, hd), lambda bi, h, i: (bi, h, 0, 0)),
                  vblk(prev), vblk(same), vblk(nxt),
                  pl.BlockSpec((None, None, hd, n_ctx), lambda bi, h, i: (bi, h, 0, 0))],
        out_specs=qo_spec,
        compiler_params=_cp("arbitrary", "arbitrary", "arbitrary"),
        name="window_attention",
    )(sink, _window_bias(n_ctx), qt, k, k, k, kx, vt, vt, vt, vxt)
    o = ot.reshape(b, hk, nblk, hd, grp, Q_BLOCK).transpose(0, 2, 5, 1, 4, 3)
    return o.reshape(b, s, qw)


def _shortconv_kernel(u_ref, w_ref, b_ref, o_ref):
    u = u_ref[...]
    rows = u.shape[0]
    t = lax.broadcasted_iota(jnp.int32, u.shape, 0)
    before = jnp.where(t == 0, 0.0, pltpu.roll(u, 1, 0))
    after = jnp.where(t == rows - 1, 0.0, pltpu.roll(u, rows - 1, 0))
    o_ref[...] = before * w_ref[0:1, :] + u * w_ref[1:2, :] + after * w_ref[2:3, :] + b_ref[...]


def _shortconv(u, w, bias, tc=256):
    b, t, n = u.shape
    return pl.pallas_call(
        _shortconv_kernel,
        out_shape=jax.ShapeDtypeStruct((b, t, n), F32),
        grid=(b, n // tc),
        in_specs=[pl.BlockSpec((None, t, tc), lambda bi, j: (bi, 0, j)),
                  pl.BlockSpec((3, tc), lambda bi, j: (0, j)),
                  pl.BlockSpec((1, tc), lambda bi, j: (0, j))],
        out_specs=pl.BlockSpec((None, t, tc), lambda bi, j: (bi, 0, j)),
        compiler_params=_cp("arbitrary", "arbitrary"),
        name="short_conv",
    )(u, w, bias.reshape(1, n))


def _filter_kernel(z_ref, t_ref, w1_ref, b1_ref, w2_ref, b2_ref, w3_ref, b3_ref, fr_ref, w4_ref, dec_ref, o_ref, *,
                   zero_row):
    def dense(hv, w_ref):
        a_hi, a_lo = _split(hv)
        w_hi, w_lo = _split(w_ref[...])
        return _dot3(a_hi, a_lo, w_hi, w_lo)

    freq = fr_ref[...]
    hv = jnp.sin(freq * (dense(z_ref[...], w1_ref) + b1_ref[...]))
    hv = jnp.sin(freq * (dense(hv, w2_ref) + b2_ref[...]))
    hv = jnp.sin(freq * (dense(hv, w3_ref) + b3_ref[...]))
    out = dense(hv, w4_ref) * jnp.exp(-t_ref[...] * jnp.abs(dec_ref[...]))
    tr = out.shape[0]
    row = pl.program_id(1) * tr + lax.broadcasted_iota(jnp.int32, out.shape, 0)
    out = jnp.where(row == zero_row, 0.0, out)
    if len(o_ref.shape) == 2:
        o_ref[...] = out
    else:
        n2 = o_ref.shape[0]
        for al in range(o_ref.shape[1]):
            o_ref[:, al, :] = out[al * n2:(al + 1) * n2, :]


def _hyena_filter(seq, f_w1, f_b1, f_w2, f_b2, f_w3, f_b3, f_freq, f_w4, decay, bmajor_rows=None):
    d = f_w4.shape[1] // (2 * HY_ORDER)
    n = 2 * seq
    t = np.linspace(0.0, 1.0, seq, dtype=np.float32)[:, None]
    w = (2.0 * math.pi * np.arange(seq, dtype=np.float32)[:, None] / seq).astype(np.float32)
    f = np.linspace(1e-4, HY_BANDS - 1, HY_BANDS, dtype=np.float32)[None, :]
    z = np.concatenate([t, np.cos(f * w), -np.sin(f * w)], axis=-1).astype(np.float32)
    lag = np.arange(n)
    lag = np.where(lag < seq, lag, n - lag) % seq
    zc = np.zeros((n, LANES), np.float32)
    zc[:, :HY_EMB] = z[lag]
    tc = t[lag]
    w1 = jnp.zeros((LANES, HY_FILTER_W), F32).at[:HY_EMB].set(f_w1)
    tr = SUBLANES * bmajor_rows if bmajor_rows else min(512, seq)
    fw = HY_FILTER_W
    if bmajor_rows:
        out_shape = jax.ShapeDtypeStruct((HY_ORDER, bmajor_rows, n // bmajor_rows, d), F32)
        out_spec = pl.BlockSpec((None, bmajor_rows, SUBLANES, d), lambda o, i: (o, 0, i, 0))
    else:
        out_shape = jax.ShapeDtypeStruct((HY_ORDER, n, d), F32)
        out_spec = pl.BlockSpec((None, tr, d), lambda o, i: (o, i, 0))
    row = lambda v: v.reshape(1, -1)
    const = lambda o, i: (0, 0)
    col = lambda o, i: (0, 2 * o + (i * tr) // seq)
    return pl.pallas_call(
        functools.partial(_filter_kernel, zero_row=seq),
        out_shape=out_shape,
        grid=(HY_ORDER, n // tr),
        in_specs=[pl.BlockSpec((tr, LANES), lambda o, i: (i, 0)),
                  pl.BlockSpec((tr, 1), lambda o, i: (i, 0)),
                  pl.BlockSpec((LANES, fw), const), pl.BlockSpec((1, fw), const),
                  pl.BlockSpec((fw, fw), const), pl.BlockSpec((1, fw), const),
                  pl.BlockSpec((fw, fw), const), pl.BlockSpec((1, fw), const),
                  pl.BlockSpec((1, fw), const),
                  pl.BlockSpec((fw, d), col),
                  pl.BlockSpec((1, d), col)],
        out_specs=out_spec,
        compiler_params=_cp("arbitrary", "arbitrary"),
        name="hyena_filter",
    )(jnp.asarray(zc), jnp.asarray(tc), w1, row(f_b1), f_w2, row(f_b2), f_w3, row(f_b3), row(f_freq), f_w4,
      row(decay))


def _stack_complex(c):
    return np.block([[c.real, -c.imag], [c.imag, c.real]])


def _hi_lo(m):
    m = jnp.asarray(np.asarray(m, np.float32))
    hi = m.astype(BF16)
    lo = (m - hi.astype(F32)).astype(BF16)
    return hi, lo


def _fft_split(n):
    n2 = 128
    return n // n2, n2


@functools.lru_cache(maxsize=None)
def _fft_matrices_np(n):
    n1, n2 = _fft_split(n)
    h = n1 // 2
    k1 = np.arange(n1)
    bb = np.arange(n2)
    tw = np.exp(-2j * np.pi * np.outer(bb, k1) / n)
    f1 = np.exp(-2j * np.pi * np.outer(k1, np.arange(n1)) / n1)
    la = np.stack([_stack_complex(tw[b][:, None] * f1[:, :h]) for b in range(n2)])
    lf = np.stack([np.concatenate([(tw[b][:, None] * f1).real, (tw[b][:, None] * f1).imag], axis=0)
                   for b in range(n2)])
    f2 = np.exp(-2j * np.pi * np.outer(np.arange(n2), np.arange(n2)) / n2)
    lc = _stack_complex(f2)
    lci = _stack_complex(np.conj(f2))
    f1i = np.exp(2j * np.pi * np.outer(np.arange(h), k1) / n1)
    lai = np.stack([_stack_complex(f1i * np.conj(tw[b])[None, :] / n) for b in range(n2)])
    return la, lf, lc, lci, lai


FFT_UNROLL = 8
FFT_LANES = 256
FFT_MID_CHUNK = 512


SUBLANES = 8


PROJ_ROWS = 128


def _hyena_in_kernel(x_ref, xp_ref, xn_ref, g_ref, sh_ref, sc_ref, w_ref, cw_ref, cb_ref, o_ref, h_scr, e_scr,
                     u_scr):
    g = pl.program_id(1)
    ng = pl.num_programs(1)

    @pl.when(pl.program_id(2) == 0)
    def _():
        gain, shift, scale = g_ref[...], sh_ref[...], sc_ref[...]
        h_scr[...] = _modnorm(x_ref[...], gain, shift, scale).astype(BF16)
        e_scr[:SUBLANES] = _modnorm(xp_ref[...], gain, shift, scale)
        e_scr[SUBLANES:] = _modnorm(xn_ref[...], gain, shift, scale)

    w = w_ref[...]
    rows = h_scr.shape[0]
    n2 = o_ref.shape[0]
    edge = _dot(e_scr[...].astype(BF16), w)
    for c in range(rows // PROJ_ROWS):
        u_scr[SUBLANES + c * PROJ_ROWS:SUBLANES + (c + 1) * PROJ_ROWS, :] = _dot(
            h_scr[c * PROJ_ROWS:(c + 1) * PROJ_ROWS, :], w)
    u_scr[0:SUBLANES, :] = jnp.where(g > 0, edge[:SUBLANES, :], 0.0)
    u_scr[SUBLANES + rows:, :] = jnp.where(g < ng - 1, edge[SUBLANES:, :], 0.0)
    w0, w1, w2, bias = cw_ref[0:1, :], cw_ref[1:2, :], cw_ref[2:3, :], cb_ref[...]
    for al in range(rows // n2):
        base = SUBLANES + al * n2
        y = (u_scr[base - 1:base - 1 + n2, :] * w0 + u_scr[base:base + n2, :] * w1
             + u_scr[base + 1:base + 1 + n2, :] * w2 + bias)
        o_ref[:, al, :] = y


def _hyena_in(x, gain, shift, scale, w, conv_w, conv_b, n2, tn=512):
    b, t, d = x.shape
    n = w.shape[1]
    half = t // n2
    rows = SUBLANES * n2
    ng = t // rows
    per = rows // SUBLANES
    halo = lambda f: pl.BlockSpec((None, SUBLANES, d), lambda bi, g, j: (bi, f(g), 0))
    vec = pl.BlockSpec((None, 1, d), lambda bi, g, j: (bi, 0, 0))
    return pl.pallas_call(
        _hyena_in_kernel,
        out_shape=jax.ShapeDtypeStruct((b, n2, half, n), F32),
        grid=(b, ng, n // tn),
        in_specs=[pl.BlockSpec((None, rows, d), lambda bi, g, j: (bi, g, 0)),
                  halo(lambda g: jnp.maximum(g * per - 1, 0)),
                  halo(lambda g: jnp.minimum((g + 1) * per, ng * per - 1)),
                  pl.BlockSpec((1, d), lambda bi, g, j: (0, 0)),
                  vec, vec,
                  pl.BlockSpec((d, tn), lambda bi, g, j: (0, j)),
                  pl.BlockSpec((3, tn), lambda bi, g, j: (0, j)),
                  pl.BlockSpec((1, tn), lambda bi, g, j: (0, j))],
        out_specs=pl.BlockSpec((None, n2, SUBLANES, tn), lambda bi, g, j: (bi, 0, g, j)),
        scratch_shapes=[pltpu.VMEM((rows, d), BF16), pltpu.VMEM((2 * SUBLANES, d), F32),
                        pltpu.VMEM((rows + 2 * SUBLANES, tn), F32)],
        compiler_params=_cp("arbitrary", "arbitrary", "arbitrary"),
        name="hyena_in_proj",
    )(x, x, x, gain.reshape(1, d), shift, scale, w, conv_w, conv_b.reshape(1, n))


FFT_A_SPLIT = 2


def _fft_a_kernel(z_ref, m_hi, m_lo, t_ref, *, packed):
    nb, n1 = t_ref.shape[1], t_ref.shape[2]
    b0 = pl.program_id(2) * nb

    def body(b, carry):
        rows = jnp.concatenate([z_ref[0, b], z_ref[1, b]], axis=0) if packed else z_ref[b]
        hi, lo = _split(rows)
        out = _dot3(m_hi[b0 + b], m_lo[b0 + b], hi, lo)
        t_ref[0, b] = out[:n1]
        t_ref[1, b] = out[n1:]
        return carry
    lax.fori_loop(0, nb, body, 0, unroll=FFT_UNROLL)


def _fft_a(z, col, width, mats, packed):
    m_hi, m_lo = mats
    n2, rows2, k = m_hi.shape
    n1 = rows2 // 2
    td = FFT_LANES
    nblk = width // td
    nb = n2 // FFT_A_SPLIT
    lead = z.shape[0]
    if packed:
        in_spec = pl.BlockSpec((2, nb, k // 2, td), lambda p, j, i: (p, i, 0, col * nblk + j))
        groups = lead // 2
    else:
        in_spec = pl.BlockSpec((None, nb, k, td), lambda p, j, i: (p, i, 0, j))
        groups = lead
    c3 = lambda p, j, i: (0, 0, 0)
    return pl.pallas_call(
        functools.partial(_fft_a_kernel, packed=packed),
        out_shape=jax.ShapeDtypeStruct((groups, 2, n2, n1, width), F32),
        grid=(groups, nblk, FFT_A_SPLIT),
        in_specs=[in_spec, _single(m_hi.shape, c3), _single(m_lo.shape, c3)],
        out_specs=pl.BlockSpec((None, 2, nb, n1, td), lambda p, j, i: (p, 0, i, 0, j)),
        compiler_params=_cp("arbitrary", "arbitrary", "arbitrary"),
        name="fft_strided_stage",
    )(z, m_hi, m_lo)


def _fft_mid_kernel(t_ref, *refs, with_filter):
    if with_filter:
        h_ref, lc_hi, lc_lo, lci_hi, lci_lo, o_ref = refs
    else:
        lc_hi, lc_lo, o_ref = refs
    n2 = t_ref.shape[1]
    for kk in range(t_ref.shape[2]):
        hi, lo = _split(jnp.concatenate([t_ref[0, :, kk, :], t_ref[1, :, kk, :]], axis=0))
        x = _dot3(lc_hi[...], lc_lo[...], hi, lo)
        if with_filter:
            xr, xi = x[:n2], x[n2:]
            hr, hi_ = h_ref[0, kk], h_ref[1, kk]
            y_hi, y_lo = _split(jnp.concatenate([xr * hr - xi * hi_, xr * hi_ + xi * hr], axis=0))
            x = _dot3(lci_hi[...], lci_lo[...], y_hi, y_lo)
        o_ref[0, kk] = x[:n2]
        o_ref[1, kk] = x[n2:]


def _fft_mid(t, mats, spec=None, order=0):
    g, _, n2, n1, width = t.shape
    tw = FFT_MID_CHUNK
    c2 = lambda kg, j, p: (0, 0)
    in_specs = [pl.BlockSpec((None, 2, n2, SUBLANES, tw), lambda kg, j, p: (p, 0, 0, kg, j))]
    args = [t]
    if spec is not None:
        in_specs.append(pl.BlockSpec((None, 2, SUBLANES, n2, tw), lambda kg, j, p: (order, 0, kg, 0, j)))
        args.append(spec)
    in_specs += [pl.BlockSpec(m.shape, c2) for m in mats]
    return pl.pallas_call(
        functools.partial(_fft_mid_kernel, with_filter=spec is not None),
        out_shape=jax.ShapeDtypeStruct((g, 2, n1, n2, width), F32),
        grid=(n1 // SUBLANES, width // tw, g),
        in_specs=in_specs,
        out_specs=pl.BlockSpec((None, 2, SUBLANES, n2, tw), lambda kg, j, p: (p, 0, kg, 0, j)),
        compiler_params=_cp("arbitrary", "arbitrary", "arbitrary"),
        name="fft_contiguous_stage",
    )(*args, *mats)


def _fft_inv_kernel(u_ref, z_ref, g_ref, skip_ref, m_hi, m_lo, o_ref, *, natural_out):
    cols = u_ref.shape[2]
    half = z_ref.shape[2]
    b0 = pl.program_id(1) * cols
    skip = skip_ref[...]
    for c in range(cols):
        hi, lo = _split(jnp.concatenate([u_ref[0, :, c, :], u_ref[1, :, c, :]], axis=0))
        out = _dot3(m_hi[b0 + c], m_lo[b0 + c], hi, lo)
        for plane in range(2):
            y = out[plane * half:(plane + 1) * half]
            val = (g_ref[plane, c] * (y + z_ref[plane, c] * skip)).astype(o_ref.dtype)
            if natural_out:
                o_ref[plane, :, c, :] = val
            else:
                o_ref[plane, c] = val


def _fft_inv(u, z, z_col, g, g_col, skip, order, mats, natural_out):
    m_hi, m_lo = mats
    pairs, _, n1, n2, width = u.shape
    half = n1 // 2
    cols = SUBLANES
    sig = lambda col: pl.BlockSpec((2, cols, half, width), lambda p, i: (p, i, 0, col))
    c3 = lambda p, i: (0, 0, 0)
    if natural_out:
        out_shape = jax.ShapeDtypeStruct((2 * pairs, half, n2, width), BF16)
        out_spec = pl.BlockSpec((2, half, cols, width), lambda p, i: (p, 0, i, 0))
    else:
        out_shape = jax.ShapeDtypeStruct((2 * pairs, n2, half, width), F32)
        out_spec = pl.BlockSpec((2, cols, half, width), lambda p, i: (p, i, 0, 0))
    out = pl.pallas_call(
        functools.partial(_fft_inv_kernel, natural_out=natural_out),
        out_shape=out_shape,
        grid=(pairs, n2 // cols),
        in_specs=[pl.BlockSpec((None, 2, n1, cols, width), lambda p, i: (p, 0, 0, i, 0)),
                  sig(z_col), sig(g_col),
                  pl.BlockSpec((None, 1, width), lambda p, i: (order, 0, 0)),
                  _single(m_hi.shape, c3), _single(m_lo.shape, c3)],
        out_specs=out_spec,
        compiler_params=_cp("arbitrary", "arbitrary"),
        name="fft_inverse_strided_stage",
    )(u, z, g, skip.reshape(HY_ORDER, 1, width), m_hi, m_lo)
    return out.reshape(2 * pairs, half * n2, width) if natural_out else out


def _fft_long_conv(usc, filt_p, skip):
    order, n2, n1, d = filt_p.shape
    la, lf, lc, lci, lai = _fft_matrices_np(n1 * n2)
    fwd = (*_hi_lo(lc),)
    both = (*_hi_lo(lc), *_hi_lo(lci))
    spec = _fft_mid(_fft_a(filt_p, 0, d, _hi_lo(lf), packed=False), fwd)
    la_m, lai_m = _hi_lo(la), _hi_lo(lai)
    z = usc
    for o in range(order):
        t = _fft_a(z, 0, d, la_m, packed=True)
        u = _fft_mid(t, both, spec, o)
        z = _fft_inv(u, z, 0, usc, o + 1, skip, o, lai_m, natural_out=o + 1 == order)
    return z


@functools.lru_cache(maxsize=None)
def _dense_dft_np(seq):
    n = 2 * seq
    k = np.arange(n)
    f = np.exp(-2j * np.pi * np.outer(k, k) / n)
    fwd = _stack_complex(f[:, :seq])
    flt = np.concatenate([f.real, f.imag], axis=0)
    inv = _stack_complex(np.conj(f)[:seq, :] / n)
    return fwd, flt, inv


def _dense_spectrum_kernel(f_ref, m_hi, m_lo, h_ref):
    hi, lo = _split(f_ref[...])
    out = _dot3(m_hi[...], m_lo[...], hi, lo)
    n = f_ref.shape[0]
    h_ref[0] = out[:n]
    h_ref[1] = out[n:]


def _dense_spectrum(filt, td=256):
    order, n, d = filt.shape
    _, flt, _ = _dense_dft_np(n // 2)
    m_hi, m_lo = _hi_lo(flt)
    c2 = lambda o, j: (0, 0)
    return pl.pallas_call(
        _dense_spectrum_kernel,
        out_shape=jax.ShapeDtypeStruct((order, 2, n, d), F32),
        grid=(order, d // td),
        in_specs=[pl.BlockSpec((None, n, td), lambda o, j: (o, 0, j)),
                  pl.BlockSpec(m_hi.shape, c2), pl.BlockSpec(m_lo.shape, c2)],
        out_specs=pl.BlockSpec((None, 2, n, td), lambda o, j: (o, 0, 0, j)),
        compiler_params=_cp("arbitrary", "arbitrary"),
        name="filter_spectrum_dense",
    )(filt, m_hi, m_lo)


def _dense_conv_kernel(z_ref, g_ref, h_ref, skip_ref, f_hi, f_lo, i_hi, i_lo, o_ref):
    seq = z_ref.shape[1]
    n = 2 * seq
    zz = jnp.concatenate([z_ref[0], z_ref[1]], axis=0)
    hi, lo = _split(zz)
    x = _dot3(f_hi[...], f_lo[...], hi, lo)
    xr, xi = x[:n], x[n:]
    hr, hi_ = h_ref[0], h_ref[1]
    y = jnp.concatenate([xr * hr - xi * hi_, xr * hi_ + xi * hr], axis=0)
    y_hi, y_lo = _split(y)
    w = _dot3(i_hi[...], i_lo[...], y_hi, y_lo)
    skip = skip_ref[...]
    for plane in range(2):
        y_p = w[plane * seq:(plane + 1) * seq]
        o_ref[plane] = (g_ref[plane] * (y_p + z_ref[plane] * skip)).astype(o_ref.dtype)


def _dense_conv(z_arr, z_col, g_arr, g_col, spec, order, skip, out_dtype, td=256):
    b, seq, _ = z_arr.shape
    d = spec.shape[3]
    n = 2 * seq
    nblk = d // td
    fwd, _, inv = _dense_dft_np(seq)
    mats = [*_hi_lo(fwd), *_hi_lo(inv)]
    c2 = lambda j, p: (0, 0)
    return pl.pallas_call(
        _dense_conv_kernel,
        out_shape=jax.ShapeDtypeStruct((b, seq, d), out_dtype),
        grid=(nblk, b // 2),
        in_specs=[pl.BlockSpec((2, seq, td), lambda j, p: (p, 0, z_col * nblk + j)),
                  pl.BlockSpec((2, seq, td), lambda j, p: (p, 0, g_col * nblk + j)),
                  pl.BlockSpec((None, 2, n, td), lambda j, p: (order, 0, 0, j)),
                  pl.BlockSpec((None, 1, td), lambda j, p: (order, 0, j))]
                 + [pl.BlockSpec(m.shape, c2) for m in mats],
        out_specs=pl.BlockSpec((2, seq, td), lambda j, p: (p, 0, j)),
        compiler_params=_cp("arbitrary", "arbitrary"),
        name="dense_conv",
    )(z_arr, g_arr, spec, skip.reshape(HY_ORDER, 1, d), *mats)


DENSE_DFT_MAX_SEQ = 512


def _hyena(x, gain, shift, scale, p):
    b, seq, d = x.shape
    if seq <= DENSE_DFT_MAX_SEQ:
        u = _nm_matmul(x, gain, shift, scale, p["w_in"], F32, tn=1024)
        u = _shortconv(u, p["conv_w"], p["conv_b"])
        spec = _dense_spectrum(_hyena_filter(seq, *p["filter"]))
        z1 = _dense_conv(u, 0, u, 1, spec, 0, p["skip"], F32)
        return _dense_conv(z1, 0, u, 2, spec, 1, p["skip"], BF16)
    _, n2 = _fft_split(2 * seq)
    usc = _hyena_in(x, gain, shift, scale, p["w_in"], p["conv_w"], p["conv_b"], n2)
    filt_p = _hyena_filter(seq, *p["filter"], bmajor_rows=n2)
    return _fft_long_conv(usc, filt_p, p["skip"])


def _rope_tables(seq):
    t = np.arange(seq)
    pos = np.stack([t // GRID_W, t % GRID_W], axis=1).astype(np.float32)
    n = 16
    inv = jnp.asarray(ROPE_THETA, F32) ** (-jnp.arange(n, dtype=F32) / n)
    lane = np.arange(LANES)
    axis = (lane % 64) // 32
    idx = lane % 16
    sign = np.where(lane % 32 < 16, -1.0, 1.0).astype(np.float32)
    ang = jnp.asarray(pos)[:, axis] * inv[idx][None, :]
    return jnp.cos(ang), jnp.sin(ang) * sign[None, :]


def _identity_tables(seq):
    return jnp.ones((seq, LANES), F32), jnp.zeros((seq, LANES), F32)


def _mla_weights(w_dq, w_uq, w_dkv, w_ukv):
    d = w_dq.shape[0]
    hh = MLA_HEADS
    wd = jnp.concatenate([w_dq, w_dkv, jnp.zeros((d, LANES - MLA_ROPE), F32)], axis=1).astype(BF16)
    uq = w_uq.reshape(MLA_RANK, hh, MLA_NOPE + MLA_ROPE)
    uq = jnp.pad(uq, ((0, 0), (0, 0), (0, MLA_HEAD_PAD - MLA_NOPE - MLA_ROPE)))
    ukv = w_ukv.reshape(MLA_RANK, hh, MLA_NOPE + MLA_V)
    ukv = jnp.concatenate([ukv[:, :, :MLA_NOPE].reshape(MLA_RANK, hh * MLA_NOPE),
                           ukv[:, :, MLA_NOPE:].reshape(MLA_RANK, hh * MLA_V)], axis=1)
    return wd, uq.reshape(MLA_RANK, hh * MLA_HEAD_PAD).astype(BF16), ukv.astype(BF16)


def _cast_tile_kernel(w_ref, o_ref, *, valid, axis):
    w = w_ref[...]
    idx = pl.program_id(2) * FFN_TILE + lax.broadcasted_iota(jnp.int32, w.shape, axis)
    o_ref[...] = jnp.where(idx < valid, w, 0.0).astype(o_ref.dtype)


def _cast_tile(w, axis):
    depth, two, r, c = w.shape
    valid = w.shape[axis]
    nj = -(-valid // FFN_TILE)
    if axis == 3:
        blk, in_map = (r, FFN_TILE), (lambda l, k, j: (l, k, 0, j))
    else:
        blk, in_map = (FFN_TILE, c), (lambda l, k, j: (l, k, j, 0))
    return pl.pallas_call(
        functools.partial(_cast_tile_kernel, valid=valid, axis=axis - 2),
        out_shape=jax.ShapeDtypeStruct((depth, two, nj) + blk, BF16),
        grid=(depth, two, nj),
        in_specs=[pl.BlockSpec((None, None) + blk, in_map)],
        out_specs=pl.BlockSpec((None, None, None) + blk, lambda l, k, j: (l, k, j, 0, 0)),
        compiler_params=_cp("arbitrary", "arbitrary", "arbitrary"),
        name="cast_tile",
    )(w)


def _to_heads(a, heads):
    b, t, _ = a.shape
    return a.reshape(b, t, heads, -1).transpose(0, 2, 1, 3)


def _from_heads(a):
    b, h, t, e = a.shape
    return a.transpose(0, 2, 1, 3).reshape(b, t, h * e)


def kernel(x, c, ctx, c_ctx, mod_w, mod_b, norm_g, final_g, ffn_wg, ffn_wu, ffn_wd, mla_w_dq, mla_g_q, mla_w_uq, mla_w_dkv, mla_g_kv, mla_w_ukv, mla_w_o, hy_w_in, hy_conv_w, hy_conv_b, hy_f_w1, hy_f_b1, hy_f_w2, hy_f_b2, hy_f_w3, hy_f_b3, hy_f_freq, hy_f_w4, hy_decay, hy_skip, hy_w_out, win_w_qkv, win_sink, win_w_o):
    b, s, d = x.shape
    n_ctx = ctx.shape[1]
    depth = mod_w.shape[0]
    assert b % 2 == 0 and s % 512 == 0 and n_ctx % 256 == 0

    cc = jnp.concatenate([c, c_ctx[None, :], jnp.zeros((8 - (b + 1) % 8, d), F32)], axis=0)
    mods = _modulation(cc, mod_w, mod_b).reshape(depth, cc.shape[0], N_MOD, d)

    ffn_w = (_cast_tile(ffn_wg, 3), _cast_tile(ffn_wu, 3), _cast_tile(ffn_wd, 2))
    cos_l, sin_l = _rope_tables(s)
    cos_c, sin_c = _identity_tables(n_ctx)

    for i in range(depth):
        kind, j = i % N_MIXERS, i // N_MIXERS
        need_ctx = i < depth - 1
        ctx_live = need_ctx or kind != 1
        ml = [mods[i, :b, k][:, None, :] for k in range(N_MOD)]
        mc = [jnp.broadcast_to(mods[i, b, k][None, None, :], (b, 1, d)) for k in range(N_MOD)]

        x = _ffn(x, norm_g[i, 0], ml[0], ml[1], ml[2], final_g, *ffn_w, i, 0)
        if ctx_live:
            ctx = _ffn(ctx.reshape(1, b * n_ctx, d), norm_g[i, 0], mc[0][:1], mc[1][:1], mc[2][:1],
                       final_g, *ffn_w, i, 0).reshape(b, n_ctx, d)

        g1 = norm_g[i, 1]
        o_c = None
        if kind == 0:
            wd, wuq, wukv = _mla_weights(mla_w_dq[j], mla_w_uq[j], mla_w_dkv[j], mla_w_ukv[j])
            q_l, k_l, v_l = _mla_proj(x, g1, ml[3], ml[4], wd, mla_g_q[j], mla_g_kv[j], wuq, wukv, cos_l, sin_l)
            q_c, k_c, v_c = _mla_proj(ctx, g1, mc[3], mc[4], wd, mla_g_q[j], mla_g_kv[j], wuq, wukv, cos_c, sin_c)
            w_o = mla_w_o[j].astype(BF16)
            o_l = _attention(q_l, [k_c, k_l], [v_c, v_l])
            if need_ctx:
                o_c = _attention(q_c, [k_c], [v_c])
        elif kind == 1:
            hp = dict(w_in=hy_w_in[j].astype(BF16), conv_w=hy_conv_w[j], conv_b=hy_conv_b[j],
                      filter=(hy_f_w1[j], hy_f_b1[j], hy_f_w2[j], hy_f_b2[j], hy_f_w3[j], hy_f_b3[j],
                              hy_f_freq[j], hy_f_w4[j], hy_decay[j]),
                      skip=hy_skip[j])
            w_o = hy_w_out[j].astype(BF16)
            o_l = _hyena(x, g1, ml[3], ml[4], hp)
            if need_ctx:
                o_c = _hyena(ctx, g1, mc[3], mc[4], hp)
        else:
            qw = WIN_Q_HEADS * WIN_HEAD_DIM
            kw = WIN_KV_HEADS * WIN_HEAD_DIM
            w_qkv = win_w_qkv[j].astype(BF16)
            w_o = win_w_o[j].astype(BF16)
            q_chunks = qw // LANES
            rope_l = (cos_l, sin_l, (qw + kw) // LANES, q_chunks, WIN_HEAD_DIM ** -0.5)
            rope_c = (cos_c, sin_c, (qw + kw) // LANES, q_chunks, WIN_HEAD_DIM ** -0.5)
            qkv_l = _nm_matmul(x, g1, ml[3], ml[4], w_qkv, BF16, tn=qw + 2 * kw, rope=rope_l)
            qkv_c = _nm_matmul(ctx, g1, mc[3], mc[4], w_qkv, BF16, tn=qw + 2 * kw, rope=rope_c)
            o_l = _window_attention(qkv_l, qkv_c, win_sink[j])
            if need_ctx:
                q_c = _to_heads(qkv_c[..., :qw], WIN_Q_HEADS)
                k_c = _to_heads(qkv_c[..., qw:qw + kw], WIN_KV_HEADS)
                v_c = _to_heads(qkv_c[..., qw + kw:], WIN_KV_HEADS)
                o_c = _from_heads(_sink_attention(q_c, k_c, v_c, win_sink[j]))

        x = _mm_res(o_l, w_o, x, ml[5])
        x = _ffn(x, norm_g[i, 2], ml[6], ml[7], ml[8], final_g, *ffn_w, i, 1, final_norm=i == depth - 1)
        if need_ctx:
            ctx = _mm_res(o_c, w_o, ctx, mc[5])
            ctx = _ffn(ctx.reshape(1, b * n_ctx, d), norm_g[i, 2], mc[6][:1], mc[7][:1], mc[8][:1],
                       final_g, *ffn_w, i, 1).reshape(b, n_ctx, d)
    return x
```

```python
import functools
import math

import numpy as np
import jax
import jax.numpy as jnp
from jax import lax
from jax.experimental import pallas as pl
from jax.experimental.pallas import tpu as pltpu

F32 = jnp.float32
BF16 = jnp.bfloat16

GRID_W = 64
N_MOD = 9
FFN_RES = 0.5
NORM_EPS = 1e-6
ROPE_THETA = 10000.0
N_MIXERS = 3
MLA_HEADS = 16
MLA_NOPE = 128
MLA_ROPE = 64
MLA_V = 128
MLA_RANK = 512
MLA_HEAD_PAD = 256
HY_ORDER = 2
HY_EMB = 33
HY_BANDS = (HY_EMB - 1) // 2
HY_FILTER_W = 64
WIN_HEAD_DIM = 64
WIN_Q_HEADS = 32
WIN_KV_HEADS = 4
WIN_GROUP = WIN_Q_HEADS // WIN_KV_HEADS
WINDOW = 128
Q_BLOCK = 128

LANES = 128
V7X_VMEM_LIMIT = 56 * 1024 * 1024
FFN_VMEM_LIMIT = 63 * 1024 * 1024
FFN_TILE = 512
NEG_BIG = -1e30


def _cp(*sem):
    return pltpu.CompilerParams(dimension_semantics=sem, vmem_limit_bytes=V7X_VMEM_LIMIT)


def _single(shape, imap):
    return pl.BlockSpec(shape, imap, pipeline_mode=pl.Buffered(1))


def _dot(a, b):
    return jnp.dot(a, b, preferred_element_type=F32)


def _dot_nt(a, b):
    return lax.dot_general(a, b, (((1,), (1,)), ((), ())), preferred_element_type=F32)


def _split(x):
    hi = x.astype(BF16)
    lo = (x - hi.astype(F32)).astype(BF16)
    return hi, lo


def _dot3(a_hi, a_lo, b_hi, b_lo):
    return _dot(a_hi, b_hi) + (_dot(a_hi, b_lo) + _dot(a_lo, b_hi))


def _modnorm(x, gain, shift, scale):
    r = lax.rsqrt(jnp.mean(x * x, axis=-1, keepdims=True) + NORM_EPS)
    return (x * r * gain) * (1.0 + scale) + shift


def _rms(x, gain):
    return x * lax.rsqrt(jnp.mean(x * x, axis=-1, keepdims=True) + NORM_EPS) * gain


def _rope128(x, cos, sin):
    lane = lax.broadcasted_iota(jnp.int32, x.shape, 1)
    partner = jnp.where(lane % 32 < 16, pltpu.roll(x, LANES - 16, 1), pltpu.roll(x, 16, 1))
    return x * cos + partner * sin


def _mod_kernel(a_ref, w_ref, b_ref, o_ref):
    a = a_ref[...]
    a = a * jax.nn.sigmoid(a)
    a_hi, a_lo = _split(a)
    w_hi, w_lo = _split(w_ref[...])
    o_ref[...] = _dot3(a_hi, a_lo, w_hi, w_lo) + b_ref[...]


def _modulation(cc, mod_w, mod_b):
    depth, d, n = mod_w.shape
    rows = cc.shape[0]
    tn = n // 16
    return pl.pallas_call(
        _mod_kernel,
        out_shape=jax.ShapeDtypeStruct((depth, rows, n), F32),
        grid=(depth, n // tn),
        in_specs=[
            pl.BlockSpec((rows, d), lambda l, j: (0, 0)),
            pl.BlockSpec((None, d, tn), lambda l, j: (l, 0, j)),
            pl.BlockSpec((None, 1, tn), lambda l, j: (l, 0, j)),
        ],
        out_specs=pl.BlockSpec((None, rows, tn), lambda l, j: (l, 0, j)),
        compiler_params=_cp("arbitrary", "arbitrary"),
        name="modulation",
    )(cc, mod_w, mod_b.reshape(depth, 1, n))


def _ffn_kernel(x_ref, g_ref, sh_ref, sc_ref, gt_ref, wg_ref, wu_ref, wd_ref, o_ref, h_scr, acc_scr):
    j = pl.program_id(2)

    @pl.when(j == 0)
    def _():
        h_scr[...] = _modnorm(x_ref[...], g_ref[...], sh_ref[...], sc_ref[...]).astype(BF16)
        acc_scr[...] = jnp.zeros_like(acc_scr)

    h = h_scr[...]
    g = _dot(h, wg_ref[...])
    u = _dot(h, wu_ref[...])
    a = (g * jax.nn.sigmoid(g) * u).astype(BF16)
    acc_scr[...] += _dot(a, wd_ref[...])

    @pl.when(j == pl.num_programs(2) - 1)
    def _():
        o_ref[...] = x_ref[...] + FFN_RES * gt_ref[...] * acc_scr[...]


def _ffn(x, gain, shift, scale, gate, wg, wu, wd, layer, slot, tm=1024):
    b, t, d = x.shape
    nj = wg.shape[2]
    tm = min(tm, t)
    vec = pl.BlockSpec((None, 1, d), lambda bi, i, j: (bi, 0, 0))
    return pl.pallas_call(
        _ffn_kernel,
        out_shape=jax.ShapeDtypeStruct((b, t, d), F32),
        grid=(b, t // tm, nj),
        in_specs=[
            pl.BlockSpec((None, tm, d), lambda bi, i, j: (bi, i, 0)),
            pl.BlockSpec((1, d), lambda bi, i, j: (0, 0)),
            vec, vec, vec,
            pl.BlockSpec((None, None, None, d, FFN_TILE), lambda bi, i, j: (layer, slot, j, 0, 0)),
            pl.BlockSpec((None, None, None, d, FFN_TILE), lambda bi, i, j: (layer, slot, j, 0, 0)),
            pl.BlockSpec((None, None, None, FFN_TILE, d), lambda bi, i, j: (layer, slot, j, 0, 0)),
        ],
        out_specs=_single((None, tm, d), lambda bi, i, j: (bi, i, 0)),
        scratch_shapes=[pltpu.VMEM((tm, d), BF16), pltpu.VMEM((tm, d), F32)],
        compiler_params=pltpu.CompilerParams(dimension_semantics=("arbitrary",) * 3, vmem_limit_bytes=FFN_VMEM_LIMIT),
        name="ffn",
    )(x, gain.reshape(1, d), shift, scale, gate, wg, wu, wd)


def _nm_matmul_kernel(x_ref, g_ref, sh_ref, sc_ref, w_ref, *rest, rope_chunks, scaled_chunks, out_scale):
    if rope_chunks:
        cos_ref, sin_ref, o_ref, h_scr = rest
    else:
        o_ref, h_scr = rest

    @pl.when(pl.program_id(2) == 0)
    def _():
        h_scr[...] = _modnorm(x_ref[...], g_ref[...], sh_ref[...], sc_ref[...]).astype(BF16)

    y = _dot(h_scr[...], w_ref[...])
    if not rope_chunks:
        o_ref[...] = y.astype(o_ref.dtype)
        return
    cos = cos_ref[...]
    sin = sin_ref[...]
    for c in range(y.shape[1] // LANES):
        yc = y[:, c * LANES:(c + 1) * LANES]
        if c < rope_chunks:
            yc = _rope128(yc, cos, sin)
        if c < scaled_chunks:
            yc = yc * out_scale
        o_ref[:, c * LANES:(c + 1) * LANES] = yc.astype(o_ref.dtype)


def _nm_matmul(x, gain, shift, scale, w, out_dtype, tn, tm=512, rope=None):
    b, t, d = x.shape
    n = w.shape[1]
    tm = min(tm, t)
    vec = pl.BlockSpec((None, 1, d), lambda bi, i, j: (bi, 0, 0))
    in_specs = [
        pl.BlockSpec((None, tm, d), lambda bi, i, j: (bi, i, 0)),
        pl.BlockSpec((1, d), lambda bi, i, j: (0, 0)),
        vec, vec,
        pl.BlockSpec((d, tn), lambda bi, i, j: (0, j)),
    ]
    args = [x, gain.reshape(1, d), shift, scale, w]
    rope_chunks = scaled_chunks = 0
    out_scale = 1.0
    if rope is not None:
        cos, sin, rope_chunks, scaled_chunks, out_scale = rope
        in_specs += [pl.BlockSpec((tm, LANES), lambda bi, i, j: (i, 0))] * 2
        args += [cos, sin]
    return pl.pallas_call(
        functools.partial(_nm_matmul_kernel, rope_chunks=rope_chunks, scaled_chunks=scaled_chunks,
                          out_scale=out_scale),
        out_shape=jax.ShapeDtypeStruct((b, t, n), out_dtype),
        grid=(b, t // tm, n // tn),
        in_specs=in_specs,
        out_specs=pl.BlockSpec((None, tm, tn), lambda bi, i, j: (bi, i, j)),
        scratch_shapes=[pltpu.VMEM((tm, d), BF16)],
        compiler_params=_cp("arbitrary", "arbitrary", "arbitrary"),
        name="norm_matmul",
    )(*args)


def _mm_res_kernel(a_ref, w_ref, r_ref, gt_ref, o_ref):
    o_ref[...] = r_ref[...] + gt_ref[...] * _dot(a_ref[...], w_ref[...])


def _mm_res(a, w, res, gate, tm=512):
    b, t, k = a.shape
    n = w.shape[1]
    tm = min(tm, t)
    return pl.pallas_call(
        _mm_res_kernel,
        out_shape=jax.ShapeDtypeStruct((b, t, n), F32),
        grid=(b, t // tm),
        in_specs=[
            pl.BlockSpec((None, tm, k), lambda bi, i: (bi, i, 0)),
            pl.BlockSpec((k, n), lambda bi, i: (0, 0)),
            pl.BlockSpec((None, tm, n), lambda bi, i: (bi, i, 0)),
            pl.BlockSpec((None, 1, n), lambda bi, i: (bi, 0, 0)),
        ],
        out_specs=pl.BlockSpec((None, tm, n), lambda bi, i: (bi, i, 0)),
        compiler_params=_cp("arbitrary", "arbitrary"),
        name="proj_residual",
    )(a, w, res, gate)


def _final_norm_kernel(x_ref, g_ref, o_ref):
    o_ref[...] = _rms(x_ref[...], g_ref[...])


def _final_norm(x, gain, tm=512):
    b, t, d = x.shape
    return pl.pallas_call(
        _final_norm_kernel,
        out_shape=jax.ShapeDtypeStruct((b, t, d), F32),
        grid=(b, t // tm),
        in_specs=[pl.BlockSpec((None, tm, d), lambda bi, i: (bi, i, 0)),
                  pl.BlockSpec((1, d), lambda bi, i: (0, 0))],
        out_specs=pl.BlockSpec((None, tm, d), lambda bi, i: (bi, i, 0)),
        compiler_params=_cp("arbitrary", "arbitrary"),
        name="final_norm",
    )(x, gain.reshape(1, d))


def _mla_proj_kernel(x_ref, g_ref, sh_ref, sc_ref, wd_ref, gq_ref, gkv_ref, wuq_ref, wukv_ref, cos_ref, sin_ref,
                     q_ref, k_ref, v_ref, *, q_scale):
    h = _modnorm(x_ref[...], g_ref[...], sh_ref[...], sc_ref[...]).astype(BF16)
    a = _dot(h, wd_ref[...])
    cq = _rms(a[:, :MLA_RANK], gq_ref[...]).astype(BF16)
    ckv = _rms(a[:, MLA_RANK:2 * MLA_RANK], gkv_ref[...]).astype(BF16)
    cos = cos_ref[...]
    sin = sin_ref[...]
    kr = _rope128(a[:, 2 * MLA_RANK:], cos, sin).astype(BF16)
    q = _dot(cq, wuq_ref[...])
    kv = _dot(ckv, wukv_ref[...])
    for hd in range(MLA_HEADS):
        base = hd * MLA_HEAD_PAD
        q_ref[hd, :, :LANES] = (q[:, base:base + LANES] * q_scale).astype(BF16)
        q_ref[hd, :, LANES:] = (_rope128(q[:, base + LANES:base + 2 * LANES], cos, sin) * q_scale).astype(BF16)
        k_ref[hd, :, :LANES] = kv[:, hd * MLA_NOPE:(hd + 1) * MLA_NOPE].astype(BF16)
        k_ref[hd, :, LANES:] = kr
        voff = MLA_HEADS * MLA_NOPE + hd * MLA_V
        v_ref[hd] = kv[:, voff:voff + MLA_V].T.astype(BF16)


def _mla_proj(x, gain, shift, scale, wd, gq, gkv, wuq, wukv, cos, sin, tm=256):
    b, t, d = x.shape
    tm = min(tm, t)
    vec = pl.BlockSpec((None, 1, d), lambda bi, i: (bi, 0, 0))
    const = lambda bi, i: (0, 0)
    head_out = lambda w: pl.BlockSpec((None, MLA_HEADS, tm, w), lambda bi, i: (bi, 0, i, 0))
    return pl.pallas_call(
        functools.partial(_mla_proj_kernel, q_scale=(MLA_NOPE + MLA_ROPE) ** -0.5 * math.log2(math.e)),
        out_shape=[jax.ShapeDtypeStruct((b, MLA_HEADS, t, MLA_HEAD_PAD), BF16),
                   jax.ShapeDtypeStruct((b, MLA_HEADS, t, MLA_HEAD_PAD), BF16),
                   jax.ShapeDtypeStruct((b, MLA_HEADS, MLA_V, t), BF16)],
        grid=(b, t // tm),
        in_specs=[
            pl.BlockSpec((None, tm, d), lambda bi, i: (bi, i, 0)),
            pl.BlockSpec((1, d), const),
            vec, vec,
            _single(wd.shape, const),
            pl.BlockSpec((1, MLA_RANK), const),
            pl.BlockSpec((1, MLA_RANK), const),
            _single(wuq.shape, const),
            _single(wukv.shape, const),
            pl.BlockSpec((tm, LANES), lambda bi, i: (i, 0)),
            pl.BlockSpec((tm, LANES), lambda bi, i: (i, 0)),
        ],
        out_specs=[head_out(MLA_HEAD_PAD), head_out(MLA_HEAD_PAD),
                   pl.BlockSpec((None, MLA_HEADS, MLA_V, tm), lambda bi, i: (bi, 0, 0, i))],
        compiler_params=_cp("arbitrary", "arbitrary"),
        name="mla_proj",
    )(x, gain.reshape(1, d), shift, scale, wd, gq.reshape(1, -1), gkv.reshape(1, -1), wuq, wukv, cos, sin)


ATTN_CHUNK = 256
ATTN_SUB = 256


def _attn_kernel(q_ref, *refs):
    o_ref, s_scr = refs[-2:]
    nseg = (len(refs) - 2) // 2
    k_refs, vt_refs = refs[:nseg], refs[nseg:2 * nseg]
    tq = q_ref.shape[0]
    sub = s_scr.shape[2]
    for u in range(tq // sub):
        q = q_ref[u * sub:(u + 1) * sub, :]
        base = 0
        for k_ref in k_refs:
            for c in range(k_ref.shape[0] // ATTN_CHUNK):
                rows = slice(c * ATTN_CHUNK, (c + 1) * ATTN_CHUNK)
                s_scr[u, base + rows.start:base + rows.stop, :] = _dot_nt(k_ref[rows, :], q)
            base += k_ref.shape[0]
    for u in range(tq // sub):
        s = s_scr[u]
        m = jnp.max(s, axis=0, keepdims=True)
        p = jnp.exp2(s - m)
        l = jnp.sum(p, axis=0, keepdims=True)
        pb = p.astype(BF16)
        ot = None
        base = 0
        for vt_ref in vt_refs:
            part = _dot(vt_ref[...], pb[base:base + vt_ref.shape[1], :])
            ot = part if ot is None else ot + part
            base += vt_ref.shape[1]
        o_ref[u * sub:(u + 1) * sub, :] = (ot / l).T.astype(o_ref.dtype)


def _attention(q, ks, vts, tq=1024):
    b, h, t, dk = q.shape
    dv = vts[0].shape[2]
    tk = sum(k.shape[2] for k in ks)
    tq = min(tq, t)
    sub = min(ATTN_SUB, tq)
    whole = lambda a: pl.BlockSpec((None, None) + a.shape[2:], lambda bi, hi, i: (bi, hi, 0, 0))
    return pl.pallas_call(
        _attn_kernel,
        out_shape=jax.ShapeDtypeStruct((b, t, h * dv), BF16),
        grid=(b, h, t // tq),
        in_specs=[pl.BlockSpec((None, None, tq, dk), lambda bi, hi, i: (bi, hi, i, 0))]
                 + [whole(k) for k in ks] + [whole(v) for v in vts],
        out_specs=pl.BlockSpec((None, tq, dv), lambda bi, hi, i: (bi, i, hi)),
        scratch_shapes=[pltpu.VMEM((tq // sub, tk, sub), F32)],
        compiler_params=_cp("arbitrary", "arbitrary", "arbitrary"),
        name="attention",
    )(q, *ks, *vts)


def _sink_attn_kernel(sink_ref, q_ref, k_ref, v_ref, o_ref):
    group = q_ref.shape[0]
    k = k_ref[...]
    v = v_ref[...]
    for g in range(group):
        s = _dot_nt(q_ref[g], k)
        sk = sink_ref[pl.program_id(1) * group + g]
        m = jnp.maximum(jnp.max(s, axis=1, keepdims=True), sk)
        p = jnp.exp(s - m)
        l = jnp.sum(p, axis=1, keepdims=True) + jnp.exp(sk - m)
        o_ref[g] = (_dot(p.astype(BF16), v) / l).astype(o_ref.dtype)


def _sink_attention(q, k, v, sink):
    b, hq, t, d = q.shape
    hk = k.shape[1]
    group = hq // hk
    kv_spec = pl.BlockSpec((None, None, t, d), lambda bi, h: (bi, h, 0, 0))
    q_spec = pl.BlockSpec((None, group, t, d), lambda bi, h: (bi, h, 0, 0))
    return pl.pallas_call(
        _sink_attn_kernel,
        out_shape=jax.ShapeDtypeStruct((b, hq, t, d), BF16),
        grid=(b, hk),
        in_specs=[pl.BlockSpec(memory_space=pltpu.SMEM), q_spec, kv_spec, kv_spec],
        out_specs=q_spec,
        compiler_params=_cp("arbitrary", "arbitrary"),
        name="sink_attention",
    )(sink, q, k, v)


def _window_kernel(sink_ref, bias_ref, qt_ref, kp_ref, kc_ref, kn_ref, kx_ref, vp_ref, vc_ref, vn_ref, vx_ref, o_ref):
    hk = pl.program_id(1)
    kk = jnp.concatenate([kx_ref[...], kp_ref[...], kc_ref[...], kn_ref[...]], axis=0)
    vt = jnp.concatenate([vx_ref[...], vp_ref[...], vc_ref[...], vn_ref[...]], axis=1)
    bias = bias_ref[...]
    s = _dot(kk, qt_ref[...])
    s = jnp.concatenate([s[:, g * Q_BLOCK:(g + 1) * Q_BLOCK] + bias for g in range(WIN_GROUP)], axis=1)
    sk = jnp.concatenate([jnp.full((1, Q_BLOCK), sink_ref[hk * WIN_GROUP + g], F32) for g in range(WIN_GROUP)],
                         axis=1)
    m = jnp.maximum(jnp.max(s, axis=0, keepdims=True), sk)
    e = jnp.exp(s - m)
    l = jnp.sum(e, axis=0, keepdims=True) + jnp.exp(sk - m)
    o_ref[...] = (_dot(vt, e.astype(BF16)) / l).astype(o_ref.dtype)


def _window_bias(n_ctx):
    c = np.arange(n_ctx + 3 * Q_BLOCK)[:, None]
    r = np.arange(Q_BLOCK)[None, :]
    band = (c < n_ctx) | (np.abs(c - n_ctx - WINDOW - r) <= WINDOW)
    in_prev = (c >= n_ctx) & (c < n_ctx + Q_BLOCK)
    in_next = c >= n_ctx + 2 * Q_BLOCK
    ok = np.stack([band & ~in_prev, band, band & ~in_next])
    return jnp.asarray(np.where(ok, 0.0, NEG_BIG).astype(np.float32))


def _window_attention(qkv, qkv_ctx, sink):
    b, s, _ = qkv.shape
    n_ctx = qkv_ctx.shape[1]
    hd, hq, hk, grp = WIN_HEAD_DIM, WIN_Q_HEADS, WIN_KV_HEADS, WIN_GROUP
    qw, kw = hq * hd, hk * hd
    nblk = s // Q_BLOCK
    assert nblk >= 2 and WINDOW == Q_BLOCK
    qt = qkv[..., :qw].reshape(b, nblk, Q_BLOCK, hk, grp, hd).transpose(0, 3, 1, 5, 4, 2)
    qt = qt.reshape(b, hk, nblk, hd, grp * Q_BLOCK)
    k = _to_heads(qkv[..., qw:qw + kw], hk)
    kx = _to_heads(qkv_ctx[..., qw:qw + kw], hk)
    vt = qkv[..., qw + kw:].reshape(b, s, hk, hd).transpose(0, 2, 3, 1)
    vxt = qkv_ctx[..., qw + kw:].reshape(b, n_ctx, hk, hd).transpose(0, 2, 3, 1)

    prev = lambda i: jnp.maximum(i - 1, 0)
    nxt = lambda i: jnp.minimum(i + 1, nblk - 1)
    kblk = lambda f: pl.BlockSpec((None, None, Q_BLOCK, hd), lambda bi, h, i: (bi, h, f(i), 0))
    vblk = lambda f: pl.BlockSpec((None, None, hd, Q_BLOCK), lambda bi, h, i: (bi, h, 0, f(i)))
    same = lambda i: i
    edge = lambda bi, h, i: (jnp.where(i == 0, 0, jnp.where(i == nblk - 1, 2, 1)), 0, 0)
    qo_spec = pl.BlockSpec((None, None, None, hd, grp * Q_BLOCK), lambda bi, h, i: (bi, h, i, 0, 0))
    ot = pl.pallas_call(
        _window_kernel,
        out_shape=jax.ShapeDtypeStruct((b, hk, nblk, hd, grp * Q_BLOCK), BF16),
        grid=(b, hk, nblk),
        in_specs=[pl.BlockSpec(memory_space=pltpu.SMEM),
                  pl.BlockSpec((None, n_ctx + 3 * Q_BLOCK, Q_BLOCK), edge),
                  qo_spec,
                  kblk(prev), kblk(same), kblk(nxt),
                  pl.BlockSpec((None, None, n_ctx, hd), lambda bi, h, i: (bi, h, 0, 0)),
                  vblk(prev), vblk(same), vblk(nxt),
                  pl.BlockSpec((None, None, hd, n_ctx), lambda bi, h, i: (bi, h, 0, 0))],
        out_specs=qo_spec,
        compiler_params=_cp("arbitrary", "arbitrary", "arbitrary"),
        name="window_attention",
    )(sink, _window_bias(n_ctx), qt, k, k, k, kx, vt, vt, vt, vxt)
    o = ot.reshape(b, hk, nblk, hd, grp, Q_BLOCK).transpose(0, 2, 5, 1, 4, 3)
    return o.reshape(b, s, qw)


def _shortconv_kernel(u_ref, w_ref, b_ref, o_ref):
    u = u_ref[...]
    rows = u.shape[0]
    t = lax.broadcasted_iota(jnp.int32, u.shape, 0)
    before = jnp.where(t == 0, 0.0, pltpu.roll(u, 1, 0))
    after = jnp.where(t == rows - 1, 0.0, pltpu.roll(u, rows - 1, 0))
    o_ref[...] = before * w_ref[0:1, :] + u * w_ref[1:2, :] + after * w_ref[2:3, :] + b_ref[...]


def _shortconv(u, w, bias, tc=256):
    b, t, n = u.shape
    return pl.pallas_call(
        _shortconv_kernel,
        out_shape=jax.ShapeDtypeStruct((b, t, n), F32),
        grid=(b, n // tc),
        in_specs=[pl.BlockSpec((None, t, tc), lambda bi, j: (bi, 0, j)),
                  pl.BlockSpec((3, tc), lambda bi, j: (0, j)),
                  pl.BlockSpec((1, tc), lambda bi, j: (0, j))],
        out_specs=pl.BlockSpec((None, t, tc), lambda bi, j: (bi, 0, j)),
        compiler_params=_cp("arbitrary", "arbitrary"),
        name="short_conv",
    )(u, w, bias.reshape(1, n))


def _filter_kernel(z_ref, t_ref, w1_ref, b1_ref, w2_ref, b2_ref, w3_ref, b3_ref, fr_ref, w4_ref, dec_ref, o_ref, *,
                   zero_row):
    def dense(hv, w_ref):
        a_hi, a_lo = _split(hv)
        w_hi, w_lo = _split(w_ref[...])
        return _dot3(a_hi, a_lo, w_hi, w_lo)

    freq = fr_ref[...]
    hv = jnp.sin(freq * (dense(z_ref[...], w1_ref) + b1_ref[...]))
    hv = jnp.sin(freq * (dense(hv, w2_ref) + b2_ref[...]))
    hv = jnp.sin(freq * (dense(hv, w3_ref) + b3_ref[...]))
    out = dense(hv, w4_ref) * jnp.exp(-t_ref[...] * jnp.abs(dec_ref[...]))
    tr = out.shape[0]
    row = pl.program_id(1) * tr + lax.broadcasted_iota(jnp.int32, out.shape, 0)
    out = jnp.where(row == zero_row, 0.0, out)
    if len(o_ref.shape) == 2:
        o_ref[...] = out
    else:
        n2 = o_ref.shape[0]
        for al in range(o_ref.shape[1]):
            o_ref[:, al, :] = out[al * n2:(al + 1) * n2, :]


def _hyena_filter(seq, f_w1, f_b1, f_w2, f_b2, f_w3, f_b3, f_freq, f_w4, decay, bmajor_rows=None):
    d = f_w4.shape[1] // (2 * HY_ORDER)
    n = 2 * seq
    t = np.linspace(0.0, 1.0, seq, dtype=np.float32)[:, None]
    w = (2.0 * math.pi * np.arange(seq, dtype=np.float32)[:, None] / seq).astype(np.float32)
    f = np.linspace(1e-4, HY_BANDS - 1, HY_BANDS, dtype=np.float32)[None, :]
    z = np.concatenate([t, np.cos(f * w), -np.sin(f * w)], axis=-1).astype(np.float32)
    lag = np.arange(n)
    lag = np.where(lag < seq, lag, n - lag) % seq
    zc = np.zeros((n, LANES), np.float32)
    zc[:, :HY_EMB] = z[lag]
    tc = t[lag]
    w1 = jnp.zeros((LANES, HY_FILTER_W), F32).at[:HY_EMB].set(f_w1)
    tr = SUBLANES * bmajor_rows if bmajor_rows else min(512, seq)
    fw = HY_FILTER_W
    if bmajor_rows:
        out_shape = jax.ShapeDtypeStruct((HY_ORDER, bmajor_rows, n // bmajor_rows, d), F32)
        out_spec = pl.BlockSpec((None, bmajor_rows, SUBLANES, d), lambda o, i: (o, 0, i, 0))
    else:
        out_shape = jax.ShapeDtypeStruct((HY_ORDER, n, d), F32)
        out_spec = pl.BlockSpec((None, tr, d), lambda o, i: (o, i, 0))
    row = lambda v: v.reshape(1, -1)
    const = lambda o, i: (0, 0)
    col = lambda o, i: (0, 2 * o + (i * tr) // seq)
    return pl.pallas_call(
        functools.partial(_filter_kernel, zero_row=seq),
        out_shape=out_shape,
        grid=(HY_ORDER, n // tr),
        in_specs=[pl.BlockSpec((tr, LANES), lambda o, i: (i, 0)),
                  pl.BlockSpec((tr, 1), lambda o, i: (i, 0)),
                  pl.BlockSpec((LANES, fw), const), pl.BlockSpec((1, fw), const),
                  pl.BlockSpec((fw, fw), const), pl.BlockSpec((1, fw), const),
                  pl.BlockSpec((fw, fw), const), pl.BlockSpec((1, fw), const),
                  pl.BlockSpec((1, fw), const),
                  pl.BlockSpec((fw, d), col),
                  pl.BlockSpec((1, d), col)],
        out_specs=out_spec,
        compiler_params=_cp("arbitrary", "arbitrary"),
        name="hyena_filter",
    )(jnp.asarray(zc), jnp.asarray(tc), w1, row(f_b1), f_w2, row(f_b2), f_w3, row(f_b3), row(f_freq), f_w4,
      row(decay))


def _stack_complex(c):
    return np.block([[c.real, -c.imag], [c.imag, c.real]])


def _hi_lo(m):
    m = jnp.asarray(np.asarray(m, np.float32))
    hi = m.astype(BF16)
    lo = (m - hi.astype(F32)).astype(BF16)
    return hi, lo


def _fft_split(n):
    n2 = 128
    return n // n2, n2


@functools.lru_cache(maxsize=None)
def _fft_matrices_np(n):
    n1, n2 = _fft_split(n)
    h = n1 // 2
    k1 = np.arange(n1)
    bb = np.arange(n2)
    tw = np.exp(-2j * np.pi * np.outer(bb, k1) / n)
    f1 = np.exp(-2j * np.pi * np.outer(k1, np.arange(n1)) / n1)
    la = np.stack([_stack_complex(tw[b][:, None] * f1[:, :h]) for b in range(n2)])
    lf = np.stack([np.concatenate([(tw[b][:, None] * f1).real, (tw[b][:, None] * f1).imag], axis=0)
                   for b in range(n2)])
    f2 = np.exp(-2j * np.pi * np.outer(np.arange(n2), np.arange(n2)) / n2)
    lc = _stack_complex(f2)
    lci = _stack_complex(np.conj(f2))
    f1i = np.exp(2j * np.pi * np.outer(np.arange(h), k1) / n1)
    lai = np.stack([_stack_complex(f1i * np.conj(tw[b])[None, :] / n) for b in range(n2)])
    return la, lf, lc, lci, lai


FFT_UNROLL = 8
FFT_LANES = 256
FFT_MID_CHUNK = 512


SUBLANES = 8


def _hyena_in_kernel(x_ref, xp_ref, xn_ref, g_ref, sh_ref, sc_ref, w_ref, cw_ref, cb_ref, o_ref, h_scr, e_scr):
    g = pl.program_id(1)
    ng = pl.num_programs(1)

    @pl.when(pl.program_id(2) == 0)
    def _():
        gain, shift, scale = g_ref[...], sh_ref[...], sc_ref[...]
        h_scr[...] = _modnorm(x_ref[...], gain, shift, scale).astype(BF16)
        e_scr[:SUBLANES] = _modnorm(xp_ref[...], gain, shift, scale)
        e_scr[SUBLANES:] = _modnorm(xn_ref[...], gain, shift, scale)

    w = w_ref[...]
    u = _dot(h_scr[...], w)
    edge = _dot(e_scr[...].astype(BF16), w)
    rows = u.shape[0]
    n2 = o_ref.shape[0]
    t = lax.broadcasted_iota(jnp.int32, u.shape, 0)
    prev_row = jnp.where(g > 0, edge[SUBLANES - 1:SUBLANES, :], 0.0)
    next_row = jnp.where(g < ng - 1, edge[SUBLANES:SUBLANES + 1, :], 0.0)
    before = jnp.where(t == 0, prev_row, pltpu.roll(u, 1, 0))
    after = jnp.where(t == rows - 1, next_row, pltpu.roll(u, rows - 1, 0))
    y = before * cw_ref[0:1, :] + u * cw_ref[1:2, :] + after * cw_ref[2:3, :] + cb_ref[...]
    for al in range(rows // n2):
        o_ref[:, al, :] = y[al * n2:(al + 1) * n2, :]


def _hyena_in(x, gain, shift, scale, w, conv_w, conv_b, n2, tn=512):
    b, t, d = x.shape
    n = w.shape[1]
    half = t // n2
    rows = SUBLANES * n2
    ng = t // rows
    per = rows // SUBLANES
    halo = lambda f: pl.BlockSpec((None, SUBLANES, d), lambda bi, g, j: (bi, f(g), 0))
    vec = pl.BlockSpec((None, 1, d), lambda bi, g, j: (bi, 0, 0))
    return pl.pallas_call(
        _hyena_in_kernel,
        out_shape=jax.ShapeDtypeStruct((b, n2, half, n), F32),
        grid=(b, ng, n // tn),
        in_specs=[pl.BlockSpec((None, rows, d), lambda bi, g, j: (bi, g, 0)),
                  halo(lambda g: jnp.maximum(g * per - 1, 0)),
                  halo(lambda g: jnp.minimum((g + 1) * per, ng * per - 1)),
                  pl.BlockSpec((1, d), lambda bi, g, j: (0, 0)),
                  vec, vec,
                  pl.BlockSpec((d, tn), lambda bi, g, j: (0, j)),
                  pl.BlockSpec((3, tn), lambda bi, g, j: (0, j)),
                  pl.BlockSpec((1, tn), lambda bi, g, j: (0, j))],
        out_specs=pl.BlockSpec((None, n2, SUBLANES, tn), lambda bi, g, j: (bi, 0, g, j)),
        scratch_shapes=[pltpu.VMEM((rows, d), BF16), pltpu.VMEM((2 * SUBLANES, d), F32)],
        compiler_params=_cp("arbitrary", "arbitrary", "arbitrary"),
        name="hyena_in_proj",
    )(x, x, x, gain.reshape(1, d), shift, scale, w, conv_w, conv_b.reshape(1, n))


FFT_A_SPLIT = 2


def _fft_a_kernel(z_ref, m_hi, m_lo, t_ref, *, packed):
    nb, n1 = t_ref.shape[1], t_ref.shape[2]
    b0 = pl.program_id(2) * nb

    def body(b, carry):
        rows = jnp.concatenate([z_ref[0, b], z_ref[1, b]], axis=0) if packed else z_ref[b]
        hi, lo = _split(rows)
        out = _dot3(m_hi[b0 + b], m_lo[b0 + b], hi, lo)
        t_ref[0, b] = out[:n1]
        t_ref[1, b] = out[n1:]
        return carry
    lax.fori_loop(0, nb, body, 0, unroll=FFT_UNROLL)


def _fft_a(z, col, width, mats, packed):
    m_hi, m_lo = mats
    n2, rows2, k = m_hi.shape
    n1 = rows2 // 2
    td = FFT_LANES
    nblk = width // td
    nb = n2 // FFT_A_SPLIT
    lead = z.shape[0]
    if packed:
        in_spec = pl.BlockSpec((2, nb, k // 2, td), lambda p, j, i: (p, i, 0, col * nblk + j))
        groups = lead // 2
    else:
        in_spec = pl.BlockSpec((None, nb, k, td), lambda p, j, i: (p, i, 0, j))
        groups = lead
    c3 = lambda p, j, i: (0, 0, 0)
    return pl.pallas_call(
        functools.partial(_fft_a_kernel, packed=packed),
        out_shape=jax.ShapeDtypeStruct((groups, 2, n2, n1, width), F32),
        grid=(groups, nblk, FFT_A_SPLIT),
        in_specs=[in_spec, _single(m_hi.shape, c3), _single(m_lo.shape, c3)],
        out_specs=pl.BlockSpec((None, 2, nb, n1, td), lambda p, j, i: (p, 0, i, 0, j)),
        compiler_params=_cp("arbitrary", "arbitrary", "arbitrary"),
        name="fft_strided_stage",
    )(z, m_hi, m_lo)


def _fft_mid_kernel(t_ref, *refs, with_filter):
    if with_filter:
        h_ref, lc_hi, lc_lo, lci_hi, lci_lo, o_ref = refs
    else:
        lc_hi, lc_lo, o_ref = refs
    n2 = t_ref.shape[1]
    for kk in range(t_ref.shape[2]):
        hi, lo = _split(jnp.concatenate([t_ref[0, :, kk, :], t_ref[1, :, kk, :]], axis=0))
        x = _dot3(lc_hi[...], lc_lo[...], hi, lo)
        if with_filter:
            xr, xi = x[:n2], x[n2:]
            hr, hi_ = h_ref[0, kk], h_ref[1, kk]
            y_hi, y_lo = _split(jnp.concatenate([xr * hr - xi * hi_, xr * hi_ + xi * hr], axis=0))
            x = _dot3(lci_hi[...], lci_lo[...], y_hi, y_lo)
        o_ref[0, kk] = x[:n2]
        o_ref[1, kk] = x[n2:]


def _fft_mid(t, mats, spec=None, order=0):
    g, _, n2, n1, width = t.shape
    tw = FFT_MID_CHUNK
    c2 = lambda kg, j, p: (0, 0)
    in_specs = [pl.BlockSpec((None, 2, n2, SUBLANES, tw), lambda kg, j, p: (p, 0, 0, kg, j))]
    args = [t]
    if spec is not None:
        in_specs.append(pl.BlockSpec((None, 2, SUBLANES, n2, tw), lambda kg, j, p: (order, 0, kg, 0, j)))
        args.append(spec)
    in_specs += [pl.BlockSpec(m.shape, c2) for m in mats]
    return pl.pallas_call(
        functools.partial(_fft_mid_kernel, with_filter=spec is not None),
        out_shape=jax.ShapeDtypeStruct((g, 2, n1, n2, width), F32),
        grid=(n1 // SUBLANES, width // tw, g),
        in_specs=in_specs,
        out_specs=pl.BlockSpec((None, 2, SUBLANES, n2, tw), lambda kg, j, p: (p, 0, kg, 0, j)),
        compiler_params=_cp("arbitrary", "arbitrary", "arbitrary"),
        name="fft_contiguous_stage",
    )(*args, *mats)


def _fft_inv_kernel(u_ref, z_ref, g_ref, skip_ref, m_hi, m_lo, o_ref, *, natural_out):
    cols = u_ref.shape[2]
    half = z_ref.shape[2]
    b0 = pl.program_id(1) * cols
    skip = skip_ref[...]
    for c in range(cols):
        hi, lo = _split(jnp.concatenate([u_ref[0, :, c, :], u_ref[1, :, c, :]], axis=0))
        out = _dot3(m_hi[b0 + c], m_lo[b0 + c], hi, lo)
        for plane in range(2):
            y = out[plane * half:(plane + 1) * half]
            val = (g_ref[plane, c] * (y + z_ref[plane, c] * skip)).astype(o_ref.dtype)
            if natural_out:
                o_ref[plane, :, c, :] = val
            else:
                o_ref[plane, c] = val


def _fft_inv(u, z, z_col, g, g_col, skip, order, mats, natural_out):
    m_hi, m_lo = mats
    pairs, _, n1, n2, width = u.shape
    half = n1 // 2
    cols = SUBLANES
    sig = lambda col: pl.BlockSpec((2, cols, half, width), lambda p, i: (p, i, 0, col))
    c3 = lambda p, i: (0, 0, 0)
    if natural_out:
        out_shape = jax.ShapeDtypeStruct((2 * pairs, half, n2, width), BF16)
        out_spec = pl.BlockSpec((2, half, cols, width), lambda p, i: (p, 0, i, 0))
    else:
        out_shape = jax.ShapeDtypeStruct((2 * pairs, n2, half, width), F32)
        out_spec = pl.BlockSpec((2, cols, half, width), lambda p, i: (p, i, 0, 0))
    out = pl.pallas_call(
        functools.partial(_fft_inv_kernel, natural_out=natural_out),
        out_shape=out_shape,
        grid=(pairs, n2 // cols),
        in_specs=[pl.BlockSpec((None, 2, n1, cols, width), lambda p, i: (p, 0, 0, i, 0)),
                  sig(z_col), sig(g_col),
                  pl.BlockSpec((None, 1, width), lambda p, i: (order, 0, 0)),
                  _single(m_hi.shape, c3), _single(m_lo.shape, c3)],
        out_specs=out_spec,
        compiler_params=_cp("arbitrary", "arbitrary"),
        name="fft_inverse_strided_stage",
    )(u, z, g, skip.reshape(HY_ORDER, 1, width), m_hi, m_lo)
    return out.reshape(2 * pairs, half * n2, width) if natural_out else out


def _fft_long_conv(usc, filt_p, skip):
    order, n2, n1, d = filt_p.shape
    la, lf, lc, lci, lai = _fft_matrices_np(n1 * n2)
    fwd = (*_hi_lo(lc),)
    both = (*_hi_lo(lc), *_hi_lo(lci))
    spec = _fft_mid(_fft_a(filt_p, 0, d, _hi_lo(lf), packed=False), fwd)
    la_m, lai_m = _hi_lo(la), _hi_lo(lai)
    z = usc
    for o in range(order):
        t = _fft_a(z, 0, d, la_m, packed=True)
        u = _fft_mid(t, both, spec, o)
        z = _fft_inv(u, z, 0, usc, o + 1, skip, o, lai_m, natural_out=o + 1 == order)
    return z


@functools.lru_cache(maxsize=None)
def _dense_dft_np(seq):
    n = 2 * seq
    k = np.arange(n)
    f = np.exp(-2j * np.pi * np.outer(k, k) / n)
    fwd = _stack_complex(f[:, :seq])
    flt = np.concatenate([f.real, f.imag], axis=0)
    inv = _stack_complex(np.conj(f)[:seq, :] / n)
    return fwd, flt, inv


def _dense_spectrum_kernel(f_ref, m_hi, m_lo, h_ref):
    hi, lo = _split(f_ref[...])
    out = _dot3(m_hi[...], m_lo[...], hi, lo)
    n = f_ref.shape[0]
    h_ref[0] = out[:n]
    h_ref[1] = out[n:]


def _dense_spectrum(filt, td=256):
    order, n, d = filt.shape
    _, flt, _ = _dense_dft_np(n // 2)
    m_hi, m_lo = _hi_lo(flt)
    c2 = lambda o, j: (0, 0)
    return pl.pallas_call(
        _dense_spectrum_kernel,
        out_shape=jax.ShapeDtypeStruct((order, 2, n, d), F32),
        grid=(order, d // td),
        in_specs=[pl.BlockSpec((None, n, td), lambda o, j: (o, 0, j)),
                  pl.BlockSpec(m_hi.shape, c2), pl.BlockSpec(m_lo.shape, c2)],
        out_specs=pl.BlockSpec((None, 2, n, td), lambda o, j: (o, 0, 0, j)),
        compiler_params=_cp("arbitrary", "arbitrary"),
        name="filter_spectrum_dense",
    )(filt, m_hi, m_lo)


def _dense_conv_kernel(z_ref, g_ref, h_ref, skip_ref, f_hi, f_lo, i_hi, i_lo, o_ref):
    seq = z_ref.shape[1]
    n = 2 * seq
    zz = jnp.concatenate([z_ref[0], z_ref[1]], axis=0)
    hi, lo = _split(zz)
    x = _dot3(f_hi[...], f_lo[...], hi, lo)
    xr, xi = x[:n], x[n:]
    hr, hi_ = h_ref[0], h_ref[1]
    y = jnp.concatenate([xr * hr - xi * hi_, xr * hi_ + xi * hr], axis=0)
    y_hi, y_lo = _split(y)
    w = _dot3(i_hi[...], i_lo[...], y_hi, y_lo)
    skip = skip_ref[...]
    for plane in range(2):
        y_p = w[plane * seq:(plane + 1) * seq]
        o_ref[plane] = (g_ref[plane] * (y_p + z_ref[plane] * skip)).astype(o_ref.dtype)


def _dense_conv(z_arr, z_col, g_arr, g_col, spec, order, skip, out_dtype, td=256):
    b, seq, _ = z_arr.shape
    d = spec.shape[3]
    n = 2 * seq
    nblk = d // td
    fwd, _, inv = _dense_dft_np(seq)
    mats = [*_hi_lo(fwd), *_hi_lo(inv)]
    c2 = lambda j, p: (0, 0)
    return pl.pallas_call(
        _dense_conv_kernel,
        out_shape=jax.ShapeDtypeStruct((b, seq, d), out_dtype),
        grid=(nblk, b // 2),
        in_specs=[pl.BlockSpec((2, seq, td), lambda j, p: (p, 0, z_col * nblk + j)),
                  pl.BlockSpec((2, seq, td), lambda j, p: (p, 0, g_col * nblk + j)),
                  pl.BlockSpec((None, 2, n, td), lambda j, p: (order, 0, 0, j)),
                  pl.BlockSpec((None, 1, td), lambda j, p: (order, 0, j))]
                 + [pl.BlockSpec(m.shape, c2) for m in mats],
        out_specs=pl.BlockSpec((2, seq, td), lambda j, p: (p, 0, j)),
        compiler_params=_cp("arbitrary", "arbitrary"),
        name="dense_conv",
    )(z_arr, g_arr, spec, skip.reshape(HY_ORDER, 1, d), *mats)


DENSE_DFT_MAX_SEQ = 512


def _hyena(x, gain, shift, scale, p):
    b, seq, d = x.shape
    if seq <= DENSE_DFT_MAX_SEQ:
        u = _nm_matmul(x, gain, shift, scale, p["w_in"], F32, tn=1024)
        u = _shortconv(u, p["conv_w"], p["conv_b"])
        spec = _dense_spectrum(_hyena_filter(seq, *p["filter"]))
        z1 = _dense_conv(u, 0, u, 1, spec, 0, p["skip"], F32)
        return _dense_conv(z1, 0, u, 2, spec, 1, p["skip"], BF16)
    _, n2 = _fft_split(2 * seq)
    usc = _hyena_in(x, gain, shift, scale, p["w_in"], p["conv_w"], p["conv_b"], n2)
    filt_p = _hyena_filter(seq, *p["filter"], bmajor_rows=n2)
    return _fft_long_conv(usc, filt_p, p["skip"])


def _rope_tables(seq):
    t = np.arange(seq)
    pos = np.stack([t // GRID_W, t % GRID_W], axis=1).astype(np.float32)
    n = 16
    inv = jnp.asarray(ROPE_THETA, F32) ** (-jnp.arange(n, dtype=F32) / n)
    lane = np.arange(LANES)
    axis = (lane % 64) // 32
    idx = lane % 16
    sign = np.where(lane % 32 < 16, -1.0, 1.0).astype(np.float32)
    ang = jnp.asarray(pos)[:, axis] * inv[idx][None, :]
    return jnp.cos(ang), jnp.sin(ang) * sign[None, :]


def _identity_tables(seq):
    return jnp.ones((seq, LANES), F32), jnp.zeros((seq, LANES), F32)


def _mla_weights(w_dq, w_uq, w_dkv, w_ukv):
    d = w_dq.shape[0]
    hh = MLA_HEADS
    wd = jnp.concatenate([w_dq, w_dkv, jnp.zeros((d, LANES - MLA_ROPE), F32)], axis=1).astype(BF16)
    uq = w_uq.reshape(MLA_RANK, hh, MLA_NOPE + MLA_ROPE)
    uq = jnp.pad(uq, ((0, 0), (0, 0), (0, MLA_HEAD_PAD - MLA_NOPE - MLA_ROPE)))
    ukv = w_ukv.reshape(MLA_RANK, hh, MLA_NOPE + MLA_V)
    ukv = jnp.concatenate([ukv[:, :, :MLA_NOPE].reshape(MLA_RANK, hh * MLA_NOPE),
                           ukv[:, :, MLA_NOPE:].reshape(MLA_RANK, hh * MLA_V)], axis=1)
    return wd, uq.reshape(MLA_RANK, hh * MLA_HEAD_PAD).astype(BF16), ukv.astype(BF16)


def _cast_tile_kernel(w_ref, o_ref, *, valid, axis):
    w = w_ref[...]
    idx = pl.program_id(2) * FFN_TILE + lax.broadcasted_iota(jnp.int32, w.shape, axis)
    o_ref[...] = jnp.where(idx < valid, w, 0.0).astype(o_ref.dtype)


def _cast_tile(w, axis):
    depth, two, r, c = w.shape
    valid = w.shape[axis]
    nj = -(-valid // FFN_TILE)
    if axis == 3:
        blk, in_map = (r, FFN_TILE), (lambda l, k, j: (l, k, 0, j))
    else:
        blk, in_map = (FFN_TILE, c), (lambda l, k, j: (l, k, j, 0))
    return pl.pallas_call(
        functools.partial(_cast_tile_kernel, valid=valid, axis=axis - 2),
        out_shape=jax.ShapeDtypeStruct((depth, two, nj) + blk, BF16),
        grid=(depth, two, nj),
        in_specs=[pl.BlockSpec((None, None) + blk, in_map)],
        out_specs=pl.BlockSpec((None, None, None) + blk, lambda l, k, j: (l, k, j, 0, 0)),
        compiler_params=_cp("arbitrary", "arbitrary", "arbitrary"),
        name="cast_tile",
    )(w)


def _to_heads(a, heads):
    b, t, _ = a.shape
    return a.reshape(b, t, heads, -1).transpose(0, 2, 1, 3)


def _from_heads(a):
    b, h, t, e = a.shape
    return a.transpose(0, 2, 1, 3).reshape(b, t, h * e)


def kernel(x, c, ctx, c_ctx, mod_w, mod_b, norm_g, final_g, ffn_wg, ffn_wu, ffn_wd, mla_w_dq, mla_g_q, mla_w_uq, mla_w_dkv, mla_g_kv, mla_w_ukv, mla_w_o, hy_w_in, hy_conv_w, hy_conv_b, hy_f_w1, hy_f_b1, hy_f_w2, hy_f_b2, hy_f_w3, hy_f_b3, hy_f_freq, hy_f_w4, hy_decay, hy_skip, hy_w_out, win_w_qkv, win_sink, win_w_o):
    b, s, d = x.shape
    n_ctx = ctx.shape[1]
    depth = mod_w.shape[0]
    assert b % 2 == 0 and s % 512 == 0 and n_ctx % 256 == 0

    cc = jnp.concatenate([c, c_ctx[None, :], jnp.zeros((8 - (b + 1) % 8, d), F32)], axis=0)
    mods = _modulation(cc, mod_w, mod_b).reshape(depth, cc.shape[0], N_MOD, d)

    ffn_w = (_cast_tile(ffn_wg, 3), _cast_tile(ffn_wu, 3), _cast_tile(ffn_wd, 2))
    cos_l, sin_l = _rope_tables(s)
    cos_c, sin_c = _identity_tables(n_ctx)

    for i in range(depth):
        kind, j = i % N_MIXERS, i // N_MIXERS
        need_ctx = i < depth - 1
        ctx_live = need_ctx or kind != 1
        ml = [mods[i, :b, k][:, None, :] for k in range(N_MOD)]
        mc = [jnp.broadcast_to(mods[i, b, k][None, None, :], (b, 1, d)) for k in range(N_MOD)]

        x = _ffn(x, norm_g[i, 0], ml[0], ml[1], ml[2], *ffn_w, i, 0)
        if ctx_live:
            ctx = _ffn(ctx.reshape(1, b * n_ctx, d), norm_g[i, 0], mc[0][:1], mc[1][:1], mc[2][:1],
                       *ffn_w, i, 0).reshape(b, n_ctx, d)

        g1 = norm_g[i, 1]
        o_c = None
        if kind == 0:
            wd, wuq, wukv = _mla_weights(mla_w_dq[j], mla_w_uq[j], mla_w_dkv[j], mla_w_ukv[j])
            q_l, k_l, v_l = _mla_proj(x, g1, ml[3], ml[4], wd, mla_g_q[j], mla_g_kv[j], wuq, wukv, cos_l, sin_l)
            q_c, k_c, v_c = _mla_proj(ctx, g1, mc[3], mc[4], wd, mla_g_q[j], mla_g_kv[j], wuq, wukv, cos_c, sin_c)
            w_o = mla_w_o[j].astype(BF16)
            o_l = _attention(q_l, [k_c, k_l], [v_c, v_l])
            if need_ctx:
                o_c = _attention(q_c, [k_c], [v_c])
        elif kind == 1:
            hp = dict(w_in=hy_w_in[j].astype(BF16), conv_w=hy_conv_w[j], conv_b=hy_conv_b[j],
                      filter=(hy_f_w1[j], hy_f_b1[j], hy_f_w2[j], hy_f_b2[j], hy_f_w3[j], hy_f_b3[j],
                              hy_f_freq[j], hy_f_w4[j], hy_decay[j]),
                      skip=hy_skip[j])
            w_o = hy_w_out[j].astype(BF16)
            o_l = _hyena(x, g1, ml[3], ml[4], hp)
            if need_ctx:
                o_c = _hyena(ctx, g1, mc[3], mc[4], hp)
        else:
            qw = WIN_Q_HEADS * WIN_HEAD_DIM
            kw = WIN_KV_HEADS * WIN_HEAD_DIM
            w_qkv = win_w_qkv[j].astype(BF16)
            w_o = win_w_o[j].astype(BF16)
            q_chunks = qw // LANES
            rope_l = (cos_l, sin_l, (qw + kw) // LANES, q_chunks, WIN_HEAD_DIM ** -0.5)
            rope_c = (cos_c, sin_c, (qw + kw) // LANES, q_chunks, WIN_HEAD_DIM ** -0.5)
            qkv_l = _nm_matmul(x, g1, ml[3], ml[4], w_qkv, BF16, tn=qw + 2 * kw, rope=rope_l)
            qkv_c = _nm_matmul(ctx, g1, mc[3], mc[4], w_qkv, BF16, tn=qw + 2 * kw, rope=rope_c)
            o_l = _window_attention(qkv_l, qkv_c, win_sink[j])
            if need_ctx:
                q_c = _to_heads(qkv_c[..., :qw], WIN_Q_HEADS)
                k_c = _to_heads(qkv_c[..., qw:qw + kw], WIN_KV_HEADS)
                v_c = _to_heads(qkv_c[..., qw + kw:], WIN_KV_HEADS)
                o_c = _from_heads(_sink_attention(q_c, k_c, v_c, win_sink[j]))

        x = _mm_res(o_l, w_o, x, ml[5])
        x = _ffn(x, norm_g[i, 2], ml[6], ml[7], ml[8], *ffn_w, i, 1)
        if need_ctx:
            ctx = _mm_res(o_c, w_o, ctx, mc[5])
            ctx = _ffn(ctx.reshape(1, b * n_ctx, d), norm_g[i, 2], mc[6][:1], mc[7][:1], mc[8][:1],
                       *ffn_w, i, 1).reshape(b, n_ctx, d)
    return _final_norm(x, final_g)
```
